```python
import jax
import jax.numpy as jnp
from jax import lax
import numpy as np

D_MODEL = 1024
BATCH = 16
SEQ = 4096
DEPTH = 1
DEC_BATCH = 128
DEC_SEQ = 1
PAST_LEN = 8192
PAGE_SIZE = 128

MIX_WIDTH = D_MODEL
ATTN_WIDTH = MIX_WIDTH // 2
SSM_WIDTH = MIX_WIDTH - ATTN_WIDTH
HEAD_DIM = 64
N_HEADS = ATTN_WIDTH // HEAD_DIM
DILATED_PATTERNS = ((128, 1), (512, 4), (2048, 16))
WINDOW_MAX = 2048
ATTN_BLOCK = 128
ROPE_THETA = 10000.0
SSM_GROUP = 16
SSM_GROUPS = SSM_WIDTH // SSM_GROUP
SSM_STATE = 64
N_EXPERTS = 32
TOP_K = 4
D_FF = D_MODEL
SWIGLU_LIMIT = 7.0
SWIGLU_ALPHA = 1.702
MOE_BLOCK = 128
RMS_EPS = 1e-6

kernel_name = 'hymba_s5_dilated_moe_step'


def rms_norm(x, g):
    xf = x.astype(jnp.float32)
    y = xf * lax.rsqrt(jnp.mean(xf * xf, axis=-1, keepdims=True) + RMS_EPS)
    return (y * g.astype(jnp.float32)).astype(x.dtype)


def apply_rope(x, pos):
    half = HEAD_DIM // 2
    inv_freq = ROPE_THETA ** (-jnp.arange(half, dtype=jnp.float32) / half)
    ang = pos.astype(jnp.float32)[:, None] * inv_freq[None, :]
    cos = jnp.cos(ang)[:, None, :]
    sin = jnp.sin(ang)[:, None, :]
    xf = x.astype(jnp.float32)
    x1, x2 = xf[..., :half], xf[..., half:]
    return jnp.concatenate([x1 * cos - x2 * sin, x2 * cos + x1 * sin], axis=-1).astype(x.dtype)


def band_dilated_attention(q, k, v, window, dil):
    b, s, h, e = q.shape
    n_keys = window // dil
    span = dil * ATTN_BLOCK
    s_pad = -(-s // span) * span
    nb = s_pad // span

    def to_blocks(t):
        t = jnp.pad(t, ((0, 0), (0, s_pad - s), (0, 0), (0, 0)))
        t = t.reshape(b, nb, ATTN_BLOCK, dil, h, e)
        return t.transpose(0, 3, 4, 1, 2, 5)

    def with_prev(t):
        prev = jnp.pad(t[:, :, :, :-1], ((0, 0), (0, 0), (0, 0), (1, 0), (0, 0), (0, 0)))
        return jnp.concatenate([prev, t], axis=4)

    qb = to_blocks(q)
    kb = with_prev(to_blocks(k))
    vb = with_prev(to_blocks(v))
    scores = jnp.einsum('brhnqe,brhnke->brhnqk', qb, kb,
                        preferred_element_type=jnp.float32) * (e ** -0.5)
    qi = jnp.arange(ATTN_BLOCK)[:, None]
    ki = jnp.arange(2 * ATTN_BLOCK)[None, :]
    dist = qi + ATTN_BLOCK - ki
    key_sub = jnp.arange(nb)[:, None, None] * ATTN_BLOCK + ki[None] - ATTN_BLOCK
    valid = (dist >= 0) & (dist <= n_keys) & (key_sub >= 0)
    scores = jnp.where(valid, scores, -jnp.inf)
    lse = jax.nn.logsumexp(scores, axis=-1)
    p = jnp.exp(scores - lse[..., None])
    o = jnp.einsum('brhnqk,brhnke->brhnqe', p.astype(vb.dtype), vb,
                   preferred_element_type=jnp.float32)
    o = o.transpose(0, 3, 4, 1, 2, 5).reshape(b, s_pad, h, e)[:, :s]
    lse = lse.transpose(0, 3, 4, 1, 2).reshape(b, s_pad, h)[:, :s]
    return o, lse


def gathered_dilated_attention(q, k_all, v_all, q_index, window, dil):
    n_keys = window // dil
    j = jnp.arange(n_keys + 1)
    idx = q_index[:, None] - j[None, :] * dil
    valid = idx >= 0
    idx = jnp.maximum(idx, 0)
    kg = k_all[:, idx]
    vg = v_all[:, idx]
    scores = jnp.einsum('bthe,btjhe->bthj', q, kg,
                        preferred_element_type=jnp.float32) * (q.shape[-1] ** -0.5)
    scores = jnp.where(valid[None, :, None, :], scores, -jnp.inf)
    lse = jax.nn.logsumexp(scores, axis=-1)
    p = jnp.exp(scores - lse[..., None])
    o = jnp.einsum('bthj,btjhe->bthe', p.astype(vg.dtype), vg,
                   preferred_element_type=jnp.float32)
    return o, lse


def merge_dilations(results):
    o = jnp.stack([r[0] for r in results])
    lse = jnp.stack([r[1] for r in results])
    w = jax.nn.softmax(lse, axis=0)
    return jnp.sum(w[..., None] * o, axis=0)


def _scan_combine(left, right):
    a1r, a1i, b1r, b1i = left
    a2r, a2i, b2r, b2i = right
    return (a2r * a1r - a2i * a1i,
            a2r * a1i + a2i * a1r,
            a2r * b1r - a2i * b1i + b2r,
            a2r * b1i + a2i * b1r + b2i)


def s5_mixer(u, h0_re, h0_im, lam_re, lam_im, log_dt, b_re, b_im, c_re, c_im, d_skip, w_glu, b_glu):
    bsz, t, _ = u.shape
    f32 = jnp.float32
    lam_re = lam_re.astype(f32)
    lam_im = lam_im.astype(f32)
    dt = jnp.exp(log_dt.astype(f32))[:, None]
    mag = jnp.exp(lam_re * dt)
    ar = mag * jnp.cos(lam_im * dt)
    ai = mag * jnp.sin(lam_im * dt)
    den = lam_re * lam_re + lam_im * lam_im
    fr = ((ar - 1.0) * lam_re + ai * lam_im) / den
    fi = (ai * lam_re - (ar - 1.0) * lam_im) / den
    b_re = b_re.astype(f32)
    b_im = b_im.astype(f32)
    bbr = fr[..., None] * b_re - fi[..., None] * b_im
    bbi = fr[..., None] * b_im + fi[..., None] * b_re
    uf = u.astype(f32)
    ug = uf.reshape(bsz, t, SSM_GROUPS, SSM_GROUP)
    bu_r = jnp.einsum('btgc,gpc->tbgp', ug, bbr)
    bu_i = jnp.einsum('btgc,gpc->tbgp', ug, bbi)
    br = jnp.concatenate([h0_re.astype(f32)[None], bu_r], axis=0)
    bi = jnp.concatenate([h0_im.astype(f32)[None], bu_i], axis=0)
    a_r = jnp.broadcast_to(ar[None, None], (t + 1, 1, SSM_GROUPS, SSM_STATE))
    a_i = jnp.broadcast_to(ai[None, None], (t + 1, 1, SSM_GROUPS, SSM_STATE))
    _, _, xr, xi = lax.associative_scan(_scan_combine, (a_r, a_i, br, bi), axis=0)
    xr, xi = xr[1:], xi[1:]
    y = (jnp.einsum('tbgp,gcp->btgc', xr, c_re.astype(f32))
         - jnp.einsum('tbgp,gcp->btgc', xi, c_im.astype(f32)))
    y = y.reshape(bsz, t, SSM_WIDTH) + d_skip.astype(f32) * uf
    y = jax.nn.gelu(y)
    y = y * jax.nn.sigmoid(y @ w_glu.astype(f32) + b_glu.astype(f32))
    return y.astype(u.dtype), xr[-1], xi[-1]


def moe_ffn(x, w_router, b_router, w_gate, b_gate, w_up, b_up, w_down, b_down):
    m, d = x.shape
    logits = x.astype(jnp.float32) @ w_router.astype(jnp.float32) + b_router.astype(jnp.float32)
    top_v, top_e = lax.top_k(logits, TOP_K)
    gates = jax.nn.softmax(top_v, axis=-1)
    n_assign = m * TOP_K
    flat_e = top_e.reshape(-1)
    flat_tok = jnp.arange(n_assign) // TOP_K
    flat_w = gates.reshape(-1)
    order = jnp.argsort(flat_e)
    se = flat_e[order]
    counts = jnp.bincount(flat_e, length=N_EXPERTS)
    padded = (counts + MOE_BLOCK - 1) // MOE_BLOCK * MOE_BLOCK
    pend = jnp.cumsum(padded)
    pstart = pend - padded
    ustart = jnp.cumsum(counts) - counts
    dest = pstart[se] + jnp.arange(n_assign) - ustart[se]
    n_rows = (-(-n_assign // MOE_BLOCK) + N_EXPERTS) * MOE_BLOCK
    nb = n_rows // MOE_BLOCK
    row_tok = jnp.full((n_rows,), m, jnp.int32).at[dest].set(flat_tok[order].astype(jnp.int32))
    row_w = jnp.zeros((n_rows,), jnp.float32).at[dest].set(flat_w[order])
    block_e = jnp.minimum(jnp.searchsorted(pend, jnp.arange(nb) * MOE_BLOCK, side='right'),
                          N_EXPERTS - 1)
    x_pad = jnp.concatenate([x, jnp.zeros((1, d), x.dtype)], axis=0)
    xb = x_pad[row_tok].reshape(nb, MOE_BLOCK, d)

    def expert_block(args):
        xe, e = args
        g = xe @ w_gate[e] + b_gate[e]
        up = xe @ w_up[e] + b_up[e]
        g = jnp.minimum(g, SWIGLU_LIMIT)
        up = jnp.clip(up, -SWIGLU_LIMIT, SWIGLU_LIMIT)
        hid = (up + 1.0) * g * jax.nn.sigmoid(SWIGLU_ALPHA * g)
        return hid @ w_down[e] + b_down[e]

    yb = lax.map(expert_block, (xb, block_e)).reshape(n_rows, d)
    y = jax.ops.segment_sum(yb * row_w[:, None].astype(yb.dtype), row_tok, num_segments=m + 1)[:m]
    return y.astype(x.dtype)


def decoder_layer(x, c, pos, k_past, v_past, h0_re, h0_im, p):
    bsz, t, _ = x.shape
    mod = (jax.nn.silu(c) @ p['w_ada'] + p['b_ada'])[:, None, :]
    shift1, scale1, gate1, shift2, scale2, gate2 = jnp.split(mod, 6, axis=-1)

    h = rms_norm(x, p['g_norm1']) * (1.0 + scale1) + shift1
    proj = h @ p['w_in']
    q = proj[..., :ATTN_WIDTH].reshape(bsz, t, N_HEADS, HEAD_DIM)
    k = proj[..., ATTN_WIDTH:2 * ATTN_WIDTH].reshape(bsz, t, N_HEADS, HEAD_DIM)
    v = proj[..., 2 * ATTN_WIDTH:3 * ATTN_WIDTH].reshape(bsz, t, N_HEADS, HEAD_DIM)
    u = proj[..., 3 * ATTN_WIDTH:]
    q = apply_rope(q, pos)
    k = apply_rope(k, pos)

    if k_past is None:
        results = [band_dilated_attention(q, k, v, w, dl) for w, dl in DILATED_PATTERNS]
        keep = min(WINDOW_MAX, t)
        new_k = k[:, t - keep:]
        new_v = v[:, t - keep:]
        h0_re = jnp.zeros((bsz, SSM_GROUPS, SSM_STATE), jnp.float32)
        h0_im = jnp.zeros((bsz, SSM_GROUPS, SSM_STATE), jnp.float32)
    else:
        wbuf = k_past.shape[1]
        k_all = jnp.concatenate([k_past.astype(k.dtype), k], axis=1)
        v_all = jnp.concatenate([v_past.astype(v.dtype), v], axis=1)
        q_index = wbuf + jnp.arange(t)
        results = [gathered_dilated_attention(q, k_all, v_all, q_index, w, dl)
                   for w, dl in DILATED_PATTERNS]
        new_k = k_all[:, t:]
        new_v = v_all[:, t:]
    attn = merge_dilations(results).reshape(bsz, t, ATTN_WIDTH).astype(x.dtype)

    ssm, h_re, h_im = s5_mixer(u, h0_re, h0_im, p['lambda_re'], p['lambda_im'], p['log_dt'],
                               p['b_ssm_re'], p['b_ssm_im'], p['c_ssm_re'], p['c_ssm_im'],
                               p['d_ssm'], p['w_glu'], p['b_glu'])

    mixed = jnp.concatenate([rms_norm(attn, p['g_attn_out']),
                             rms_norm(ssm, p['g_ssm_out'])], axis=-1) @ p['w_out']
    x = x + gate1 * mixed

    h2 = rms_norm(x, p['g_norm2']) * (1.0 + scale2) + shift2
    ffn = moe_ffn(h2.reshape(bsz * t, D_MODEL), p['w_router'], p['b_router'], p['w_gate'],
                  p['b_gate'], p['w_up'], p['b_up'], p['w_down'], p['b_down'])
    x = x + gate2 * ffn.reshape(bsz, t, D_MODEL)
    return x, new_k, new_v, h_re, h_im


def setup_inputs(seed: int = 0) -> dict:
    key = jax.random.key(seed)
    ks = jax.random.split(key, 40)
    f32 = jnp.float32
    L = DEPTH
    wbuf = min(WINDOW_MAX, PAST_LEN)

    def nrm(k, shape, scale=1.0):
        return jax.random.normal(k, shape, f32) * scale

    lam_im = (jnp.broadcast_to(jnp.pi * jnp.arange(SSM_STATE, dtype=f32), (L, SSM_GROUPS, SSM_STATE))
              + nrm(ks[9], (L, SSM_GROUPS, SSM_STATE), 0.01))
    return {
        'x_prompt': nrm(ks[0], (BATCH, SEQ, D_MODEL)),
        'x_sample': nrm(ks[1], (DEC_BATCH, DEC_SEQ, D_MODEL)),
        'c_prompt': nrm(ks[2], (BATCH, D_MODEL)),
        'c_sample': nrm(ks[3], (DEC_BATCH, D_MODEL)),
        'cache_k': nrm(ks[4], (L, DEC_BATCH, wbuf, N_HEADS, HEAD_DIM)),
        'cache_v': nrm(ks[5], (L, DEC_BATCH, wbuf, N_HEADS, HEAD_DIM)),
        'state_ssm_re': nrm(ks[6], (L, DEC_BATCH, SSM_GROUPS, SSM_STATE), 0.5),
        'state_ssm_im': nrm(ks[7], (L, DEC_BATCH, SSM_GROUPS, SSM_STATE), 0.5),
        'w_ada': nrm(ks[10], (L, D_MODEL, 6 * D_MODEL), D_MODEL ** -0.5),
        'b_ada': nrm(ks[11], (L, 6 * D_MODEL), 0.01),
        'g_norm1': 1.0 + nrm(ks[12], (L, D_MODEL), 0.01),
        'g_norm2': 1.0 + nrm(ks[13], (L, D_MODEL), 0.01),
        'w_in': nrm(ks[14], (L, D_MODEL, 3 * ATTN_WIDTH + SSM_WIDTH), D_MODEL ** -0.5),
        'lambda_re': -0.5 + nrm(ks[8], (L, SSM_GROUPS, SSM_STATE), 0.01),
        'lambda_im': lam_im,
        'log_dt': jax.random.uniform(ks[15], (L, SSM_GROUPS), f32,
                                     minval=float(np.log(1e-3)), maxval=float(np.log(1e-1))),
        'b_ssm_re': nrm(ks[16], (L, SSM_GROUPS, SSM_STATE, SSM_GROUP), (2 * SSM_GROUP) ** -0.5),
        'b_ssm_im': nrm(ks[17], (L, SSM_GROUPS, SSM_STATE, SSM_GROUP), (2 * SSM_GROUP) ** -0.5),
        'c_ssm_re': nrm(ks[18], (L, SSM_GROUPS, SSM_GROUP, SSM_STATE), (2 * SSM_STATE) ** -0.5),
        'c_ssm_im': nrm(ks[19], (L, SSM_GROUPS, SSM_GROUP, SSM_STATE), (2 * SSM_STATE) ** -0.5),
        'd_ssm': nrm(ks[20], (L, SSM_WIDTH), 0.5),
        'w_glu': nrm(ks[21], (L, SSM_WIDTH, SSM_WIDTH), SSM_WIDTH ** -0.5),
        'b_glu': nrm(ks[22], (L, SSM_WIDTH), 0.01),
        'g_attn_out': 1.0 + nrm(ks[23], (L, ATTN_WIDTH), 0.01),
        'g_ssm_out': 1.0 + nrm(ks[24], (L, SSM_WIDTH), 0.01),
        'w_out': nrm(ks[25], (L, MIX_WIDTH, D_MODEL), MIX_WIDTH ** -0.5),
        'w_router': nrm(ks[26], (L, D_MODEL, N_EXPERTS), D_MODEL ** -0.5),
        'b_router': nrm(ks[27], (L, N_EXPERTS), 0.01),
        'w_gate': nrm(ks[28], (L, N_EXPERTS, D_MODEL, D_FF), D_MODEL ** -0.5),
        'b_gate': nrm(ks[29], (L, N_EXPERTS, D_FF), 0.01),
        'w_up': nrm(ks[30], (L, N_EXPERTS, D_MODEL, D_FF), D_MODEL ** -0.5),
        'b_up': nrm(ks[31], (L, N_EXPERTS, D_FF), 0.01),
        'w_down': nrm(ks[32], (L, N_EXPERTS, D_FF, D_MODEL), D_FF ** -0.5),
        'b_down': nrm(ks[33], (L, N_EXPERTS, D_MODEL), 0.01),
        'g_final': 1.0 + nrm(ks[34], (D_MODEL,), 0.01),
    }


def reference(x_prompt, x_sample, c_prompt, c_sample, cache_k, cache_v, state_ssm_re, state_ssm_im,
              w_ada, b_ada, g_norm1, g_norm2, w_in, lambda_re, lambda_im, log_dt,
              b_ssm_re, b_ssm_im, c_ssm_re, c_ssm_im, d_ssm, w_glu, b_glu,
              g_attn_out, g_ssm_out, w_out, w_router, b_router,
              w_gate, b_gate, w_up, b_up, w_down, b_down, g_final):
    seq = x_prompt.shape[1]
    dec_seq = x_sample.shape[1]
    pos_prompt = jnp.arange(seq)
    pos_sample = PAST_LEN + jnp.arange(dec_seq)
    hp, hs = x_prompt, x_sample
    kp_l, vp_l, srp_l, sip_l = [], [], [], []
    ks_l, vs_l, srs_l, sis_l = [], [], [], []
    for l in range(DEPTH):
        p = {
            'w_ada': w_ada[l], 'b_ada': b_ada[l], 'g_norm1': g_norm1[l], 'g_norm2': g_norm2[l],
            'w_in': w_in[l], 'lambda_re': lambda_re[l], 'lambda_im': lambda_im[l],
            'log_dt': log_dt[l], 'b_ssm_re': b_ssm_re[l], 'b_ssm_im': b_ssm_im[l],
            'c_ssm_re': c_ssm_re[l], 'c_ssm_im': c_ssm_im[l], 'd_ssm': d_ssm[l],
            'w_glu': w_glu[l], 'b_glu': b_glu[l], 'g_attn_out': g_attn_out[l],
            'g_ssm_out': g_ssm_out[l], 'w_out': w_out[l], 'w_router': w_router[l],
            'b_router': b_router[l], 'w_gate': w_gate[l], 'b_gate': b_gate[l],
            'w_up': w_up[l], 'b_up': b_up[l], 'w_down': w_down[l], 'b_down': b_down[l],
        }
        hp, kp, vp, srp, sip = decoder_layer(hp, c_prompt, pos_prompt, None, None, None, None, p)
        hs, kss, vss, srs, sis = decoder_layer(hs, c_sample, pos_sample, cache_k[l], cache_v[l],
                                               state_ssm_re[l], state_ssm_im[l], p)
        kp_l.append(kp)
        vp_l.append(vp)
        srp_l.append(srp)
        sip_l.append(sip)
        ks_l.append(kss)
        vs_l.append(vss)
        srs_l.append(srs)
        sis_l.append(sis)
    y_prompt = rms_norm(hp, g_final)
    y_sample = rms_norm(hs, g_final)
    return (y_prompt, y_sample,
            jnp.stack(kp_l), jnp.stack(vp_l), jnp.stack(srp_l), jnp.stack(sip_l),
            jnp.stack(ks_l), jnp.stack(vs_l), jnp.stack(srs_l), jnp.stack(sis_l))
```

```python
import functools

import jax
import jax.numpy as jnp
import numpy as np
from jax import lax
from jax.experimental import pallas as pl
from jax.experimental.pallas import tpu as pltpu

F32 = jnp.float32
BF16 = jnp.bfloat16
HIGHEST = lax.Precision.HIGHEST

HEAD_DIM = 64
DILATED_PATTERNS = ((128, 1), (512, 4), (2048, 16))
ROPE_THETA = 10000.0
PAST_LEN = 8192
SSM_GROUP = 16
SSM_STATE = 64
TOP_K = 4
SWIGLU_LIMIT = 7.0
SWIGLU_ALPHA = 1.702
RMS_EPS = 1e-6

LANES = 128
ATTN_BLOCK = 128
ATTN_Q_TILE = 1024
SSM_CHUNK = 16
ROW_TILE = 512
MOE_BLOCK = 512
VMEM_LIMIT = 48 * 1024 * 1024
NEG_BIG = -1e30


def _cparams(*sem):
    return pltpu.CompilerParams(dimension_semantics=sem, vmem_limit_bytes=VMEM_LIMIT)


def _rms(x):
    return x * lax.rsqrt(jnp.mean(x * x, axis=-1, keepdims=True) + RMS_EPS)


def _ada_kernel(c_ref, w_ref, b_ref, o_ref):
    c = c_ref[...]
    s = c * jax.nn.sigmoid(c)
    o_ref[...] = jnp.dot(s, w_ref[...], precision=HIGHEST, preferred_element_type=F32) + b_ref[...]


def _ada(c, w, b):
    n, d = c.shape
    nout = w.shape[1]
    return pl.pallas_call(
        _ada_kernel,
        grid=(nout // d,),
        in_specs=[pl.BlockSpec((n, d), lambda j: (0, 0)),
                  pl.BlockSpec((d, d), lambda j: (0, j)),
                  pl.BlockSpec((1, d), lambda j: (0, j))],
        out_specs=pl.BlockSpec((n, d), lambda j: (0, j)),
        out_shape=jax.ShapeDtypeStruct((n, nout), F32),
        compiler_params=_cparams("arbitrary"),
    )(c, w, b.reshape(1, nout))


def _inproj_kernel(x_ref, mod_ref, g_ref, w_ref, cos_ref, sin_ref,
                   q_ref, kb_ref, vb_ref, kf_ref, vf_ref, u_ref, *, aw):
    x = x_ref[0]
    h = _rms(x) * g_ref[...]
    h = h * (1.0 + mod_ref[0, 1]) + mod_ref[0, 0]
    proj = jnp.dot(h.astype(BF16), w_ref[...], preferred_element_type=F32)
    cos = cos_ref[...]
    sin = sin_ref[...]
    lane = lax.broadcasted_iota(jnp.int32, (1, aw), 1)
    first_half = (lane % HEAD_DIM) < (HEAD_DIM // 2)

    def rope(t):
        rot = jnp.where(first_half, pltpu.roll(t, aw - HEAD_DIM // 2, 1), pltpu.roll(t, HEAD_DIM // 2, 1))
        return t * cos + rot * sin

    q = rope(proj[:, :aw]) * (HEAD_DIM ** -0.5)
    k = rope(proj[:, aw:2 * aw])
    v = proj[:, 2 * aw:3 * aw]
    q_ref[0] = q.astype(q_ref.dtype)
    kb_ref[0] = k.astype(BF16)
    vb_ref[0] = v.astype(BF16)
    kf_ref[0] = k
    vf_ref[0] = v
    u_ref[0] = proj[:, 3 * aw:]


def _inproj(x, mod, g, w_bf, cos, sin, tm, q_dtype):
    b, t, d = x.shape
    nproj = w_bf.shape[1]
    aw = cos.shape[1]
    r = mod.shape[2]
    per_pos = cos.shape[0] != 1
    cs_block = (tm, aw) if per_pos else (1, aw)
    cs_map = (lambda i, j: (i, 0)) if per_pos else (lambda i, j: (0, 0))
    mod_map = (lambda i, j: (j, 0, 0, 0)) if r == 1 else (lambda i, j: (j, 0, i, 0))
    tok = lambda i, j: (j, i, 0)
    out_shapes = [jax.ShapeDtypeStruct((b, t, aw), q_dtype),
                  jax.ShapeDtypeStruct((b, t, aw), BF16),
                  jax.ShapeDtypeStruct((b, t, aw), BF16),
                  jax.ShapeDtypeStruct((b, t, aw), F32),
                  jax.ShapeDtypeStruct((b, t, aw), F32),
                  jax.ShapeDtypeStruct((b, t, nproj - 3 * aw), F32)]
    return pl.pallas_call(
        functools.partial(_inproj_kernel, aw=aw),
        grid=(t // tm, b),
        in_specs=[pl.BlockSpec((1, tm, d), tok),
                  pl.BlockSpec((1, 6, r, d), mod_map),
                  pl.BlockSpec((1, d), lambda i, j: (0, 0)),
                  pl.BlockSpec((d, nproj), lambda i, j: (0, 0)),
                  pl.BlockSpec(cs_block, cs_map),
                  pl.BlockSpec(cs_block, cs_map)],
        out_specs=[pl.BlockSpec((1, tm, s.shape[2]), tok) for s in out_shapes],
        out_shape=out_shapes,
        compiler_params=_cparams("arbitrary", "arbitrary"),
    )(x, mod, g, w_bf, cos, sin)


def _rope_tables(pos, n_heads):
    half = HEAD_DIM // 2
    inv_freq = ROPE_THETA ** (-jnp.arange(half, dtype=F32) / half)
    ang = pos.astype(F32)[:, None] * inv_freq[None, :]
    cos = jnp.cos(ang)
    sin = jnp.sin(ang)
    cos_h = jnp.concatenate([cos, cos], axis=-1)
    sin_h = jnp.concatenate([-sin, sin], axis=-1)
    return jnp.tile(cos_h, (1, n_heads)), jnp.tile(sin_h, (1, n_heads))


def _band_attn_kernel(q_ref, k_ref, v_ref, o_ref, lse_ref, *, lq, n_keys):
    j = pl.program_id(2)
    aw = q_ref.shape[2]
    blk = ATTN_BLOCK
    lane = lax.broadcasted_iota(jnp.int32, (1, LANES), 1)
    head_mask = [(lane < HEAD_DIM).astype(BF16), (lane >= HEAD_DIM).astype(BF16)]
    lane_o = lax.broadcasted_iota(jnp.int32, (blk, LANES), 1)
    row = lax.broadcasted_iota(jnp.int32, (blk, 2 * blk), 0)
    col = lax.broadcasted_iota(jnp.int32, (blk, 2 * blk), 1)
    rel = row - col

    def block(qb, carry):
        qs = pl.multiple_of(qb * blk, blk)
        gq = j * lq + qs
        ks = pl.multiple_of(jnp.maximum(gq - blk, 0), blk)
        dist = rel + (gq - ks)
        bias = jnp.where(dist >= 0, jnp.where(dist <= n_keys, 0.0, NEG_BIG), NEG_BIG)
        lse_tile = jnp.zeros((blk, LANES), F32)
        for hp in range(aw // LANES):
            cols = slice(hp * LANES, (hp + 1) * LANES)
            q2 = q_ref[0, pl.ds(qs, blk), cols]
            k2 = k_ref[0, pl.ds(ks, 2 * blk), cols]
            v2 = v_ref[0, pl.ds(ks, 2 * blk), cols]
            outs = []
            for half in range(2):
                qh = q2 * head_mask[half]
                s = lax.dot_general(qh, k2, (((1,), (1,)), ((), ())), preferred_element_type=F32) + bias
                m = jnp.max(s, axis=1, keepdims=True)
                p = jnp.exp(s - m)
                l = jnp.sum(p, axis=1, keepdims=True)
                o = jnp.dot(p.astype(BF16), v2, preferred_element_type=F32)
                outs.append(o * (1.0 / l))
                lse_tile = jnp.where(lane_o == hp * 2 + half, m + jnp.log(l), lse_tile)
            o2 = jnp.where(lane_o < HEAD_DIM, outs[0], outs[1])
            o_ref[0, pl.ds(qs, blk), cols] = o2.astype(o_ref.dtype)
        lse_ref[0, pl.ds(qs, blk), :] = lse_tile
        return carry

    lax.fori_loop(0, lq // blk, block, 0)


def _band_attention(q, k, v, window, dil):
    b, t, aw = q.shape
    n_keys = window // dil
    assert n_keys == ATTN_BLOCK and t % (dil * 2 * ATTN_BLOCK) == 0
    l = t // dil
    lq = min(l, ATTN_Q_TILE)
    view = lambda a: a.reshape(b, l, dil * aw)
    o, lse = pl.pallas_call(
        functools.partial(_band_attn_kernel, lq=lq, n_keys=n_keys),
        grid=(b, dil, l // lq),
        in_specs=[pl.BlockSpec((1, lq, aw), lambda i, c, j: (i, j, c)),
                  pl.BlockSpec((1, l, aw), lambda i, c, j: (i, 0, c)),
                  pl.BlockSpec((1, l, aw), lambda i, c, j: (i, 0, c))],
        out_specs=[pl.BlockSpec((1, lq, aw), lambda i, c, j: (i, j, c)),
                   pl.BlockSpec((1, lq, LANES), lambda i, c, j: (i, j, c))],
        out_shape=[jax.ShapeDtypeStruct((b, l, dil * aw), BF16),
                   jax.ShapeDtypeStruct((b, l, dil * LANES), F32)],
        compiler_params=_cparams("arbitrary", "arbitrary", "arbitrary"),
    )(view(q), view(k), view(v))
    return o.reshape(b, t, aw), lse.reshape(b, t, LANES)


def _decode_attn_kernel(q_ref, kn_ref, vn_ref, k1_ref, k2_ref, k3_ref, v1_ref, v2_ref, v3_ref,
                        e_ref, et_ref, o_ref):
    q = q_ref[0]
    e = e_ref[...]
    et = et_ref[...]
    n_pat = 3

    def scores(kc):
        return jnp.dot(kc * q, e, precision=HIGHEST, preferred_element_type=F32)

    s_new = scores(jnp.broadcast_to(kn_ref[0], (8, q.shape[1])))[0:1]
    s_pat = [scores(r[0]) for r in (k1_ref, k2_ref, k3_ref)]
    m = s_new
    for s in s_pat:
        m = jnp.maximum(m, jnp.max(s, axis=0, keepdims=True))
    p_new = n_pat * jnp.exp(s_new - m)
    l = p_new
    acc = jnp.dot(jnp.broadcast_to(p_new, (8, LANES)), et, precision=HIGHEST,
                  preferred_element_type=F32)[0:1] * vn_ref[0]
    for s, v_ref in zip(s_pat, (v1_ref, v2_ref, v3_ref)):
        p = jnp.exp(s - m)
        l = l + jnp.sum(p, axis=0, keepdims=True)
        pe = jnp.dot(p, et, precision=HIGHEST, preferred_element_type=F32)
        acc = acc + jnp.sum(pe * v_ref[0], axis=0, keepdims=True)
    le = jnp.dot(jnp.broadcast_to(l, (8, LANES)), et, precision=HIGHEST, preferred_element_type=F32)[0:1]
    o_ref[0] = acc / le


def _decode_attention(q, k_new, v_new, cache_k, cache_v):
    bs, w, aw = cache_k.shape
    n_heads = aw // HEAD_DIM
    head_of_lane = jnp.arange(aw) // HEAD_DIM
    e = (head_of_lane[:, None] == jnp.arange(LANES)[None, :]).astype(F32)
    et = e.T
    views, specs = [], []
    for cache in (cache_k, cache_v):
        for window, dil in DILATED_PATTERNS:
            n_keys = window // dil
            assert n_keys == ATTN_BLOCK and w % (dil * n_keys) == 0
            rows = w // dil
            views.append(cache.reshape(bs, rows, dil * aw))
            specs.append(pl.BlockSpec((1, n_keys, aw), functools.partial(
                lambda i, rb: (i, rb, 0), rb=rows // n_keys - 1)))
    row = pl.BlockSpec((1, 1, aw), lambda i: (i, 0, 0))
    return pl.pallas_call(
        _decode_attn_kernel,
        grid=(bs,),
        in_specs=[row, row, row] + specs + [pl.BlockSpec((aw, LANES), lambda i: (0, 0)),
                                            pl.BlockSpec((LANES, aw), lambda i: (0, 0))],
        out_specs=row,
        out_shape=jax.ShapeDtypeStruct((bs, 1, aw), F32),
        compiler_params=_cparams("arbitrary"),
    )(q, k_new, v_new, *views, e, et)


def _s5_discretise(lam_re, lam_im, log_dt, b_re, b_im):
    dt = jnp.exp(log_dt)[:, None]
    mag = jnp.exp(lam_re * dt)
    ar = mag * jnp.cos(lam_im * dt)
    ai = mag * jnp.sin(lam_im * dt)
    den = lam_re * lam_re + lam_im * lam_im
    fr = ((ar - 1.0) * lam_re + ai * lam_im) / den
    fi = (ai * lam_re - (ar - 1.0) * lam_im) / den
    bbr = fr[..., None] * b_re - fi[..., None] * b_im
    bbi = fr[..., None] * b_im + fi[..., None] * b_re
    return dt, ar, ai, bbr, bbi


def _s5_chunk_matrices(lam_re, lam_im, log_dt, b_re, b_im, c_re, c_im):
    lc = SSM_CHUNK
    g, p, c = b_re.shape
    dt, _, _, bbr, bbi = _s5_discretise(lam_re, lam_im, log_dt, b_re, b_im)
    kk = jnp.arange(lc + 1, dtype=F32)[:, None, None]
    mag = jnp.exp(kk * lam_re * dt)
    apr = mag * jnp.cos(kk * lam_im * dt)
    api = mag * jnp.sin(kk * lam_im * dt)
    akb_r = apr[:lc, :, :, None] * bbr - api[:lc, :, :, None] * bbi
    akb_i = apr[:lc, :, :, None] * bbi + api[:lc, :, :, None] * bbr
    kern = (jnp.einsum('gop,kgpc->kgoc', c_re, akb_r, precision=HIGHEST)
            - jnp.einsum('gop,kgpc->kgoc', c_im, akb_i, precision=HIGHEST))
    ii = jnp.arange(lc)
    lag = ii[None, :] - ii[:, None]
    toe = jnp.where((lag >= 0)[:, :, None, None, None], kern[jnp.clip(lag, 0, lc - 1)], 0.0)
    toe = toe.transpose(2, 0, 4, 1, 3).reshape(g, lc * c, lc * c)
    rev = lc - 1 - ii
    ws_r = akb_r[rev].transpose(1, 0, 3, 2).reshape(g, lc * c, p)
    ws_i = akb_i[rev].transpose(1, 0, 3, 2).reshape(g, lc * c, p)
    a1r, a1i = apr[1:], api[1:]
    ca_r = c_re[None] * a1r[:, :, None, :] - c_im[None] * a1i[:, :, None, :]
    ca_i = c_re[None] * a1i[:, :, None, :] + c_im[None] * a1r[:, :, None, :]
    wy_r = ca_r.transpose(1, 3, 0, 2).reshape(g, p, lc * c)
    wy_i = (-ca_i).transpose(1, 3, 0, 2).reshape(g, p, lc * c)

    def pair_diag(m):
        r, s = m.shape[1:]
        m2 = m.reshape(g // 2, 2, r, s)
        z = jnp.zeros((g // 2, r, s), m.dtype)
        top = jnp.concatenate([m2[:, 0], z], axis=2)
        bot = jnp.concatenate([z, m2[:, 1]], axis=2)
        return jnp.concatenate([top, bot], axis=1)

    alc_r = apr[lc].reshape(1, g * p)
    alc_i = api[lc].reshape(1, g * p)
    return (toe.astype(BF16), pair_diag(ws_r).astype(BF16), pair_diag(ws_i).astype(BF16),
            pair_diag(wy_r).astype(BF16), pair_diag(wy_i).astype(BF16), alc_r, alc_i)


def _s5_step_matrices(lam_re, lam_im, log_dt, b_re, b_im, c_re, c_im):
    g, p, c = b_re.shape
    _, ar, ai, bbr, bbi = _s5_discretise(lam_re, lam_im, log_dt, b_re, b_im)
    eye = jnp.eye(g, dtype=F32)
    bd_br = jnp.einsum('gpc,gh->gchp', bbr, eye).reshape(g * c, g * p)
    bd_bi = jnp.einsum('gpc,gh->gchp', bbi, eye).reshape(g * c, g * p)
    bd_cr = jnp.einsum('gcp,gh->gphc', c_re, eye).reshape(g * p, g * c)
    bd_ci = jnp.einsum('gcp,gh->gphc', c_im, eye).reshape(g * p, g * c)
    return bd_br, bd_bi, bd_cr, bd_ci, ar.reshape(1, g * p), ai.reshape(1, g * p)


def _s5_prompt_kernel(u_ref, toe_ref, wsr_ref, wsi_ref, wyr_ref, wyi_ref, ar_ref, ai_ref,
                      y_ref, hr_ref, hi_ref, sr, si, hr_hist, hi_hist):
    nc = u_ref.shape[1]
    n_pairs = wsr_ref.shape[0]
    pw = wsr_ref.shape[1]
    sw = wsr_ref.shape[2]
    gw = toe_ref.shape[1]
    for pr in range(n_pairs):
        up = u_ref[0, :, pr * pw:(pr + 1) * pw]
        sr[:, pr * sw:(pr + 1) * sw] = jnp.dot(up, wsr_ref[pr], preferred_element_type=F32)
        si[:, pr * sw:(pr + 1) * sw] = jnp.dot(up, wsi_ref[pr], preferred_element_type=F32)

    ar = ar_ref[...]
    ai = ai_ref[...]
    nstate = ar.shape[1]
    rowid = lax.broadcasted_iota(jnp.int32, (8, nstate), 0)

    def eight_chunks(c8, carry):
        h_r, h_i = carry
        base = pl.multiple_of(c8 * 8, 8)
        s_r8 = sr[pl.ds(base, 8), :]
        s_i8 = si[pl.ds(base, 8), :]
        t_r = jnp.zeros((8, nstate), F32)
        t_i = jnp.zeros((8, nstate), F32)
        for r in range(8):
            t_r = jnp.where(rowid == r, h_r, t_r)
            t_i = jnp.where(rowid == r, h_i, t_i)
            n_r = ar * h_r - ai * h_i + s_r8[r:r + 1, :]
            n_i = ar * h_i + ai * h_r + s_i8[r:r + 1, :]
            h_r, h_i = n_r, n_i
        hr_hist[pl.ds(base, 8), :] = t_r
        hi_hist[pl.ds(base, 8), :] = t_i
        return h_r, h_i

    zero = jnp.zeros((1, nstate), F32)
    h_r, h_i = lax.fori_loop(0, nc // 8, eight_chunks, (zero, zero))
    hr_ref[0] = h_r
    hi_ref[0] = h_i

    for pr in range(n_pairs):
        st = slice(pr * sw, (pr + 1) * sw)
        y2 = (jnp.dot(hr_hist[:, st].astype(BF16), wyr_ref[pr], preferred_element_type=F32)
              + jnp.dot(hi_hist[:, st].astype(BF16), wyi_ref[pr], preferred_element_type=F32))
        for gg in range(2):
            g = pr * 2 + gg
            cols = slice(g * gw, (g + 1) * gw)
            yi = jnp.dot(u_ref[0, :, cols], toe_ref[g], preferred_element_type=F32)
            y_ref[0, :, cols] = (yi + y2[:, gg * gw:(gg + 1) * gw]).astype(y_ref.dtype)


def _s5_prompt(u_perm, mats):
    toe, wsr, wsi, wyr, wyi, alr, ali = mats
    b, nc, wid = u_perm.shape
    nstate = alr.shape[1]
    full = lambda a: pl.BlockSpec(a.shape, lambda i: (0,) * a.ndim)
    return pl.pallas_call(
        _s5_prompt_kernel,
        grid=(b,),
        in_specs=[pl.BlockSpec((1, nc, wid), lambda i: (i, 0, 0))] + [full(a) for a in mats],
        out_specs=[pl.BlockSpec((1, nc, wid), lambda i: (i, 0, 0)),
                   pl.BlockSpec((1, 1, nstate), lambda i: (i, 0, 0)),
                   pl.BlockSpec((1, 1, nstate), lambda i: (i, 0, 0))],
        out_shape=[jax.ShapeDtypeStruct((b, nc, wid), F32),
                   jax.ShapeDtypeStruct((b, 1, nstate), F32),
                   jax.ShapeDtypeStruct((b, 1, nstate), F32)],
        scratch_shapes=[pltpu.VMEM((nc, nstate), F32) for _ in range(4)],
        compiler_params=_cparams("arbitrary"),
    )(u_perm, *mats)


def _s5_step_kernel(u_ref, h0r_ref, h0i_ref, bbr_ref, bbi_ref, cr_ref, ci_ref, ar_ref, ai_ref,
                    y_ref, xr_ref, xi_ref):
    u = u_ref[...]
    ar = ar_ref[...]
    ai = ai_ref[...]
    h0r = h0r_ref[...]
    h0i = h0i_ref[...]
    xr = ar * h0r - ai * h0i + jnp.dot(u, bbr_ref[...], precision=HIGHEST, preferred_element_type=F32)
    xi = ar * h0i + ai * h0r + jnp.dot(u, bbi_ref[...], precision=HIGHEST, preferred_element_type=F32)
    xr_ref[...] = xr
    xi_ref[...] = xi
    y_ref[...] = (jnp.dot(xr, cr_ref[...], precision=HIGHEST, preferred_element_type=F32)
                  - jnp.dot(xi, ci_ref[...], precision=HIGHEST, preferred_element_type=F32))


def _s5_step(u, h0r, h0i, mats):
    n, wid = u.shape
    nstate = h0r.shape[1]
    return pl.pallas_call(
        _s5_step_kernel,
        out_shape=[jax.ShapeDtypeStruct((n, wid), F32),
                   jax.ShapeDtypeStruct((n, nstate), F32),
                   jax.ShapeDtypeStruct((n, nstate), F32)],
        compiler_params=pltpu.CompilerParams(vmem_limit_bytes=VMEM_LIMIT),
    )(u, h0r, h0i, *mats)


def _mix_kernel(*refs, merge):
    if merge:
        (x_ref, mod_ref, o1_ref, o2_ref, o3_ref, l1_ref, l2_ref, l3_ref, ys_ref, u_ref) = refs[:10]
        rest = refs[10:]
    else:
        (x_ref, mod_ref, attn_ref, ys_ref, u_ref) = refs[:5]
        rest = refs[5:]
    (eh_ref, d_ref, wglu_ref, bglu_ref, ga_ref, gs_ref, wout_ref, g2_ref, wr_ref, br_ref,
     x1_ref, h2_ref, gate_ref, idx_ref) = rest
    aw = ys_ref.shape[1]
    if merge:
        ls = [l1_ref[...], l2_ref[...], l3_ref[...]]
        m = jnp.maximum(jnp.maximum(ls[0], ls[1]), ls[2])
        es = [jnp.exp(l - m) for l in ls]
        inv = 1.0 / (es[0] + es[1] + es[2])
        attn = jnp.zeros(ys_ref.shape, F32)
        for e, o_ref in zip(es, (o1_ref, o2_ref, o3_ref)):
            wexp = jnp.dot(e * inv, eh_ref[...], precision=HIGHEST, preferred_element_type=F32)
            attn = attn + wexp * o_ref[...].astype(F32)
    else:
        attn = attn_ref[...]
    a_n = _rms(attn) * ga_ref[...]

    y = ys_ref[...].astype(F32) + d_ref[...] * u_ref[...]
    y = 0.5 * y * (1.0 + jnp.tanh(np.sqrt(2.0 / np.pi).astype(np.float32) * (y + 0.044715 * (y * y * y))))
    z = jnp.dot(y.astype(BF16), wglu_ref[...], preferred_element_type=F32) + bglu_ref[...]
    ssm = y * jax.nn.sigmoid(z)
    s_n = _rms(ssm) * gs_ref[...]

    mixed = (jnp.dot(a_n.astype(BF16), wout_ref[0:aw, :], preferred_element_type=F32)
             + jnp.dot(s_n.astype(BF16), wout_ref[aw:, :], preferred_element_type=F32))
    x1 = x_ref[...] + mod_ref[0, 2] * mixed
    x1_ref[...] = x1
    h2 = _rms(x1) * g2_ref[...] * (1.0 + mod_ref[0, 4]) + mod_ref[0, 3]
    h2_ref[...] = h2.astype(BF16)

    logits = jnp.dot(h2, wr_ref[...], precision=HIGHEST, preferred_element_type=F32) + br_ref[...]
    lane = lax.broadcasted_iota(jnp.int32, logits.shape, 1)
    lane_f = lane.astype(F32)
    cur = logits
    vals, idxs = [], []
    for _ in range(TOP_K):
        mx = jnp.max(cur, axis=1, keepdims=True)
        ix = jnp.min(jnp.where(cur == mx, lane_f, float(LANES)), axis=1, keepdims=True)
        vals.append(mx)
        idxs.append(ix)
        cur = jnp.where(lane_f == ix, -jnp.inf, cur)
    exps = [jnp.exp(v - vals[0]) for v in vals]
    den = exps[0]
    for e in exps[1:]:
        den = den + e
    gate = jnp.zeros(logits.shape, F32)
    idx = jnp.zeros(logits.shape, F32)
    for k in range(TOP_K):
        gate = jnp.where(lane == k, exps[k] / den, gate)
        idx = jnp.where(lane == k, idxs[k], idx)
    gate_ref[...] = gate
    idx_ref[...] = idx.astype(jnp.int32)


def _mix(x2d, mod, attn_parts, ys, u, weights, tm, rows_per_mod):
    n, d = x2d.shape
    aw = ys.shape[1]
    r = mod.shape[2]
    nt = n // tm
    tiles_per_mod = max(rows_per_mod // tm, 1)
    mod_map = (lambda i: (i // tiles_per_mod, 0, 0, 0)) if r == 1 else (lambda i: (0, 0, i, 0))
    rowspec = lambda w: pl.BlockSpec((tm, w), lambda i: (i, 0))
    full = lambda a: pl.BlockSpec(a.shape, lambda i: (0,) * a.ndim)
    merge = len(attn_parts) > 1
    acts = [x2d, mod] + list(attn_parts) + [ys, u]
    act_specs = [rowspec(d), pl.BlockSpec((1, 6, r, d), mod_map)] + [rowspec(a.shape[1]) for a in attn_parts] \
        + [rowspec(aw), rowspec(aw)]
    return pl.pallas_call(
        functools.partial(_mix_kernel, merge=merge),
        grid=(nt,),
        in_specs=act_specs + [full(w) for w in weights],
        out_specs=[rowspec(d), rowspec(d), rowspec(LANES), rowspec(LANES)],
        out_shape=[jax.ShapeDtypeStruct((n, d), F32), jax.ShapeDtypeStruct((n, d), BF16),
                   jax.ShapeDtypeStruct((n, LANES), F32), jax.ShapeDtypeStruct((n, LANES), jnp.int32)],
        compiler_params=_cparams("arbitrary"),
    )(*acts, *weights)


def _moe_kernel(be_ref, nb_ref, x_ref, wg_ref, bg_ref, wu_ref, bu_ref, wd_ref, bd_ref, o_ref):
    i = pl.program_id(0)

    @pl.when(i < nb_ref[0])
    def _():
        x = x_ref[...]
        g = jnp.dot(x, wg_ref[0], preferred_element_type=F32) + bg_ref[0]
        up = jnp.dot(x, wu_ref[0], preferred_element_type=F32) + bu_ref[0]
        g = jnp.minimum(g, SWIGLU_LIMIT)
        up = jnp.clip(up, -SWIGLU_LIMIT, SWIGLU_LIMIT)
        hid = (up + 1.0) * g * jax.nn.sigmoid(SWIGLU_ALPHA * g)
        y = jnp.dot(hid.astype(BF16), wd_ref[0], preferred_element_type=F32) + bd_ref[0]
        o_ref[...] = y.astype(o_ref.dtype)

    @pl.when(i >= nb_ref[0])
    def _():
        o_ref[...] = jnp.zeros(o_ref.shape, o_ref.dtype)


def _moe_experts(xb, block_e, n_used, wg, bg, wu, bu, wd, bd):
    n_rows, d = xb.shape
    e, _, f = wg.shape
    nb = n_rows // MOE_BLOCK
    wmap = lambda i, be, nu: (be[i], 0, 0)
    grid_spec = pltpu.PrefetchScalarGridSpec(
        num_scalar_prefetch=2,
        grid=(nb,),
        in_specs=[pl.BlockSpec((MOE_BLOCK, d), lambda i, be, nu: (i, 0)),
                  pl.BlockSpec((1, d, f), wmap), pl.BlockSpec((1, 1, f), wmap),
                  pl.BlockSpec((1, d, f), wmap), pl.BlockSpec((1, 1, f), wmap),
                  pl.BlockSpec((1, f, d), wmap), pl.BlockSpec((1, 1, d), wmap)],
        out_specs=pl.BlockSpec((MOE_BLOCK, d), lambda i, be, nu: (i, 0)),
    )
    return pl.pallas_call(
        _moe_kernel,
        grid_spec=grid_spec,
        out_shape=jax.ShapeDtypeStruct((n_rows, d), BF16),
        compiler_params=_cparams("arbitrary"),
    )(block_e, n_used, xb, wg, bg.reshape(e, 1, f), wu, bu.reshape(e, 1, f), wd, bd.reshape(e, 1, d))


def _final_kernel(x1_ref, mod_ref, gate_ref, yg_ref, gf_ref, o_ref):
    gate = gate_ref[...]
    acc = jnp.zeros(x1_ref.shape, F32)
    for k in range(TOP_K):
        acc = acc + gate[:, k:k + 1] * yg_ref[k].astype(F32)
    x = x1_ref[...] + mod_ref[0, 5] * acc
    o_ref[...] = _rms(x) * gf_ref[...]


def _final(x1, mod, gates, yg, g_final, tm, rows_per_mod, row0):
    n, d = x1.shape
    r = mod.shape[2]
    tiles_per_mod = max(rows_per_mod // tm, 1)
    mod_map = (lambda i: (i // tiles_per_mod, 0, 0, 0)) if r == 1 else (lambda i: (0, 0, i, 0))
    off = row0 // tm
    return pl.pallas_call(
        _final_kernel,
        grid=(n // tm,),
        in_specs=[pl.BlockSpec((tm, d), lambda i: (i, 0)),
                  pl.BlockSpec((1, 6, r, d), mod_map),
                  pl.BlockSpec((tm, LANES), lambda i: (i + off, 0)),
                  pl.BlockSpec((TOP_K, tm, d), lambda i: (0, i + off, 0)),
                  pl.BlockSpec((1, d), lambda i: (0, 0))],
        out_specs=pl.BlockSpec((tm, d), lambda i: (i, 0)),
        out_shape=jax.ShapeDtypeStruct((n, d), F32),
        compiler_params=_cparams("arbitrary"),
    )(x1, mod, gates, yg, g_final.reshape(1, d))


def kernel(x_prompt, x_sample, c_prompt, c_sample, cache_k, cache_v, state_ssm_re, state_ssm_im, w_ada, b_ada, g_norm1, g_norm2, w_in, lambda_re, lambda_im, log_dt, b_ssm_re, b_ssm_im, c_ssm_re, c_ssm_im, d_ssm, w_glu, b_glu, g_attn_out, g_ssm_out, w_out, w_router, b_router, w_gate, b_gate, w_up, b_up, w_down, b_down, g_final):
    assert w_ada.shape[0] == 1, "one layer"
    b, t, d = x_prompt.shape
    bs = x_sample.shape[0]
    assert x_sample.shape[1] == 1
    wbuf, n_heads = cache_k.shape[2], cache_k.shape[3]
    aw = n_heads * HEAD_DIM
    n_groups, n_state = lambda_re.shape[1:]
    sw = n_groups * SSM_GROUP
    n_exp = w_router.shape[2]
    keep = min(max(w for w, _ in DILATED_PATTERNS), t)
    assert wbuf >= max(w for w, _ in DILATED_PATTERNS)
    tm = min(ROW_TILE, t)

    mod = _ada(jnp.concatenate([c_prompt, c_sample], axis=0), w_ada[0], b_ada[0])
    mod_p = mod[:b].reshape(b, 6, 1, d)
    mod_s = mod[b:].reshape(bs, 6, d).transpose(1, 0, 2)[None]

    w_in_bf = w_in[0].astype(BF16)
    g1 = g_norm1[0].reshape(1, d)
    cos_p, sin_p = _rope_tables(jnp.arange(t), n_heads)
    cos_s, sin_s = _rope_tables(jnp.full((1,), PAST_LEN), n_heads)
    q_p, kb_p, vb_p, kf_p, vf_p, u_p = _inproj(x_prompt, mod_p, g1, w_in_bf, cos_p, sin_p, tm, BF16)
    q_s, _, _, kf_s, vf_s, u_s = _inproj(x_sample.reshape(1, bs, d), mod_s, g1, w_in_bf, cos_s, sin_s, bs, F32)

    parts = [_band_attention(q_p, kb_p, vb_p, w, dl) for w, dl in DILATED_PATTERNS]
    ck = cache_k[0].reshape(bs, wbuf, aw)
    cv = cache_v[0].reshape(bs, wbuf, aw)
    kn = kf_s.reshape(bs, 1, aw)
    vn = vf_s.reshape(bs, 1, aw)
    attn_s = _decode_attention(q_s.reshape(bs, 1, aw), kn, vn, ck, cv).reshape(bs, aw)

    ssm_params = (lambda_re[0], lambda_im[0], log_dt[0], b_ssm_re[0], b_ssm_im[0], c_ssm_re[0], c_ssm_im[0])
    lc, gc = SSM_CHUNK, SSM_GROUP
    nc = t // lc
    u_perm = (u_p.astype(BF16).reshape(b, nc, lc, n_groups, gc).transpose(0, 1, 3, 2, 4)
              .reshape(b, nc, n_groups * lc * gc))
    y_perm, hr_p, hi_p = _s5_prompt(u_perm, _s5_chunk_matrices(*ssm_params))
    ys_p = (y_perm.reshape(b, nc, n_groups, lc, gc).transpose(0, 1, 3, 2, 4).reshape(b * t, sw))
    ys_s, hr_s, hi_s = _s5_step(u_s.reshape(bs, sw), state_ssm_re[0].reshape(bs, n_groups * n_state),
                                state_ssm_im[0].reshape(bs, n_groups * n_state),
                                _s5_step_matrices(*ssm_params))

    head_of_lane = jnp.arange(aw) // HEAD_DIM
    eh = (jnp.arange(LANES)[:, None] == head_of_lane[None, :]).astype(F32)
    wr_pad = jnp.zeros((d, LANES), F32).at[:, :n_exp].set(w_router[0])
    br_pad = jnp.full((1, LANES), NEG_BIG, F32).at[0, :n_exp].set(b_router[0])
    mix_w = (eh, d_ssm[0].reshape(1, sw), w_glu[0].astype(BF16), b_glu[0].reshape(1, sw),
             g_attn_out[0].reshape(1, aw), g_ssm_out[0].reshape(1, sw), w_out[0].astype(BF16),
             g_norm2[0].reshape(1, d), wr_pad, br_pad)
    flat = lambda a: a.reshape(b * t, a.shape[-1])
    attn_in = [flat(o) for o, _ in parts] + [flat(l) for _, l in parts]
    x1_p, h2_p, gate_p, idx_p = _mix(flat(x_prompt), mod_p, attn_in, ys_p, flat(u_p), mix_w, tm, t)
    x1_s, h2_s, gate_s, idx_s = _mix(x_sample.reshape(bs, d), mod_s, [attn_s], ys_s, u_s.reshape(bs, sw),
                                     mix_w, bs, 1)

    m_tok = b * t + bs
    h2 = jnp.concatenate([h2_p, h2_s], axis=0)
    gates = jnp.concatenate([gate_p, gate_s], axis=0)
    top_e = jnp.concatenate([idx_p, idx_s], axis=0)[:, :TOP_K]
    n_assign = m_tok * TOP_K
    flat_e = top_e.reshape(-1)
    onehot = (flat_e[:, None] == jnp.arange(n_exp)[None, :]).astype(jnp.int32)
    csum = jnp.cumsum(onehot, axis=0)
    rank = jnp.take_along_axis(csum, flat_e[:, None], axis=1)[:, 0] - 1
    counts = csum[-1]
    padded = (counts + MOE_BLOCK - 1) // MOE_BLOCK * MOE_BLOCK
    pend = jnp.cumsum(padded)
    pstart = pend - padded
    dest = pstart[flat_e] + rank
    nb = -(-n_assign // MOE_BLOCK) + n_exp
    n_rows = nb * MOE_BLOCK
    row_tok = jnp.zeros((n_rows,), jnp.int32).at[dest].set(jnp.arange(n_assign, dtype=jnp.int32) // TOP_K)
    block_e = jnp.minimum(jnp.searchsorted(pend, jnp.arange(nb, dtype=jnp.int32) * MOE_BLOCK, side='right'),
                          n_exp - 1).astype(jnp.int32)
    n_used = (pend[-1] // MOE_BLOCK).astype(jnp.int32).reshape(1)
    xb = jnp.take(h2, row_tok, axis=0)
    yb = _moe_experts(xb, block_e, n_used, w_gate[0].astype(BF16), b_gate[0], w_up[0].astype(BF16), b_up[0],
                      w_down[0].astype(BF16), b_down[0])
    yg = jnp.take(yb, dest.reshape(m_tok, TOP_K).T, axis=0)

    y_prompt = _final(x1_p, mod_p, gates, yg, g_final, tm, t, 0).reshape(b, t, d)
    y_sample = _final(x1_s, mod_s, gates, yg, g_final, bs, 1, b * t).reshape(bs, 1, d)

    k_win_p = kf_p[:, t - keep:].reshape(1, b, keep, n_heads, HEAD_DIM)
    v_win_p = vf_p[:, t - keep:].reshape(1, b, keep, n_heads, HEAD_DIM)
    k_win_s = jnp.concatenate([ck[:, 1:], kn], axis=1).reshape(1, bs, wbuf, n_heads, HEAD_DIM)
    v_win_s = jnp.concatenate([cv[:, 1:], vn], axis=1).reshape(1, bs, wbuf, n_heads, HEAD_DIM)
    st = lambda a, n: a.reshape(1, n, n_groups, n_state)
    return (y_prompt, y_sample, k_win_p, v_win_p, st(hr_p, b), st(hi_p, b),
            k_win_s, v_win_s, st(hr_s, bs), st(hi_s, bs))
```

```python
import functools

import jax
import jax.numpy as jnp
import numpy as np
from jax import lax
from jax.experimental import pallas as pl
from jax.experimental.pallas import tpu as pltpu

F32 = jnp.float32
BF16 = jnp.bfloat16
HIGHEST = lax.Precision.HIGHEST

HEAD_DIM = 64
DILATED_PATTERNS = ((128, 1), (512, 4), (2048, 16))
ROPE_THETA = 10000.0
PAST_LEN = 8192
SSM_GROUP = 16
SSM_STATE = 64
TOP_K = 4
SWIGLU_LIMIT = 7.0
SWIGLU_ALPHA = 1.702
RMS_EPS = 1e-6

LANES = 128
HEADS_PER_LANE_TILE = LANES // HEAD_DIM
ATTN_BLOCK = 128
DECODE_HEADS = 4
SSM_CHUNK = 16
ROW_TILE = 512
MOE_BLOCK = 512
VMEM_LIMIT = 52 * 1024 * 1024
NEG_BIG = -1e30


def _cparams(*sem):
    return pltpu.CompilerParams(dimension_semantics=sem, vmem_limit_bytes=VMEM_LIMIT)


def _rms(x):
    return x * lax.rsqrt(jnp.mean(x * x, axis=-1, keepdims=True) + RMS_EPS)


def _ada_kernel(c_ref, w_ref, b_ref, o_ref):
    c = c_ref[...]
    s = c * jax.nn.sigmoid(c)
    o_ref[...] = jnp.dot(s, w_ref[...], precision=HIGHEST, preferred_element_type=F32) + b_ref[...]


def _ada(c, w, b):
    n, d = c.shape
    nout = w.shape[1]
    return pl.pallas_call(
        _ada_kernel,
        grid=(nout // d,),
        in_specs=[pl.BlockSpec((n, d), lambda j: (0, 0)),
                  pl.BlockSpec((d, d), lambda j: (0, j)),
                  pl.BlockSpec((1, d), lambda j: (0, j))],
        out_specs=pl.BlockSpec((n, d), lambda j: (0, j)),
        out_shape=jax.ShapeDtypeStruct((n, nout), F32),
        compiler_params=_cparams("arbitrary"),
    )(c, w, b.reshape(1, nout))


def _project(x, mod_ref, g_ref, w_ref, cos_ref, sin_ref, aw):
    h = _rms(x) * g_ref[...]
    h = h * (1.0 + mod_ref[0, 1]) + mod_ref[0, 0]
    proj = jnp.dot(h.astype(BF16), w_ref[...], preferred_element_type=F32)
    cos = cos_ref[...]
    sin = sin_ref[...]
    lane = lax.broadcasted_iota(jnp.int32, (1, aw), 1)
    first_half = (lane & (HEAD_DIM - 1)) < (HEAD_DIM // 2)

    def rope(t):
        rot = jnp.where(first_half, pltpu.roll(t, aw - HEAD_DIM // 2, 1), pltpu.roll(t, HEAD_DIM // 2, 1))
        return t * cos + rot * sin

    q = rope(proj[:, :aw]) * (HEAD_DIM ** -0.5)
    k = rope(proj[:, aw:2 * aw])
    return q, k, proj[:, 2 * aw:3 * aw], proj[:, 3 * aw:]


def _inproj_prompt_kernel(x_ref, mod_ref, g_ref, w_ref, cos_ref, sin_ref,
                          q_ref, k_ref, v_ref, kt_ref, vt_ref, u_ref, *, aw):
    q, k, v, u = _project(x_ref[0], mod_ref, g_ref, w_ref, cos_ref, sin_ref, aw)
    for hp in range(aw // LANES):
        cols = slice(hp * LANES, (hp + 1) * LANES)
        q_ref[0, hp] = q[:, cols]
        k_ref[0, hp] = k[:, cols]
        v_ref[0, hp] = v[:, cols]
    kt_ref[0] = k.T
    vt_ref[0] = v.T
    u_ref[0] = u.astype(u_ref.dtype)


def _inproj_prompt(x, mod, g, w_bf, cos, sin, tm):
    b, t, d = x.shape
    nproj = w_bf.shape[1]
    aw = cos.shape[1]
    sw = nproj - 3 * aw
    n_lt = aw // LANES
    tok = lambda i, j: (j, i, 0)
    hp_major = lambda i, j: (j, 0, i, 0)
    transposed = lambda i, j: (j, 0, i)
    out_shapes = [jax.ShapeDtypeStruct((b, n_lt, t, LANES), F32)] * 3 + [
        jax.ShapeDtypeStruct((b, aw, t), F32), jax.ShapeDtypeStruct((b, aw, t), F32),
        jax.ShapeDtypeStruct((b, t, sw), BF16)]
    return pl.pallas_call(
        functools.partial(_inproj_prompt_kernel, aw=aw),
        grid=(t // tm, b),
        in_specs=[pl.BlockSpec((1, tm, d), tok),
                  pl.BlockSpec((1, 6, 1, d), lambda i, j: (j, 0, 0, 0)),
                  pl.BlockSpec((1, d), lambda i, j: (0, 0)),
                  pl.BlockSpec((d, nproj), lambda i, j: (0, 0)),
                  pl.BlockSpec((tm, aw), lambda i, j: (i, 0)),
                  pl.BlockSpec((tm, aw), lambda i, j: (i, 0))],
        out_specs=[pl.BlockSpec((1, n_lt, tm, LANES), hp_major)] * 3 + [
            pl.BlockSpec((1, aw, tm), transposed), pl.BlockSpec((1, aw, tm), transposed),
            pl.BlockSpec((1, tm, sw), tok)],
        out_shape=out_shapes,
        compiler_params=_cparams("arbitrary", "arbitrary"),
    )(x, mod, g, w_bf, cos, sin)


def _inproj_decode_kernel(x_ref, mod_ref, g_ref, w_ref, cos_ref, sin_ref, qt_ref, kt_ref, vt_ref, u_ref, *, aw):
    q, k, v, u = _project(x_ref[...], mod_ref, g_ref, w_ref, cos_ref, sin_ref, aw)
    qt_ref[...] = q.T
    kt_ref[...] = k.T
    vt_ref[...] = v.T
    u_ref[...] = u


def _inproj_decode(x, mod, g, w_bf, cos, sin):
    n, d = x.shape
    nproj = w_bf.shape[1]
    aw = cos.shape[1]
    return pl.pallas_call(
        functools.partial(_inproj_decode_kernel, aw=aw),
        out_shape=[jax.ShapeDtypeStruct((aw, n), F32)] * 3 + [jax.ShapeDtypeStruct((n, nproj - 3 * aw), F32)],
        compiler_params=pltpu.CompilerParams(vmem_limit_bytes=VMEM_LIMIT),
    )(x, mod, g, w_bf, cos, sin)


def _rope_tables(pos, n_heads):
    half = HEAD_DIM // 2
    inv_freq = ROPE_THETA ** (-jnp.arange(half, dtype=F32) / half)
    ang = pos.astype(F32)[:, None] * inv_freq[None, :]
    cos = jnp.cos(ang)
    sin = jnp.sin(ang)
    cos_h = jnp.concatenate([cos, cos], axis=-1)
    sin_h = jnp.concatenate([-sin, sin], axis=-1)
    return jnp.tile(cos_h, (1, n_heads)), jnp.tile(sin_h, (1, n_heads))


def _dilated_attn_kernel(q_ref, k_ref, v_ref, o_ref, acc, mrun, lrun):
    t = q_ref.shape[2]
    blk = ATTN_BLOCK
    lane = lax.broadcasted_iota(jnp.int32, (1, LANES), 1)
    head_mask = [(lane < HEAD_DIM).astype(F32), (lane >= HEAD_DIM).astype(F32)]
    lane_o = lax.broadcasted_iota(jnp.int32, (blk, LANES), 1)
    row = lax.broadcasted_iota(jnp.int32, (blk, 2 * blk), 0)
    col = lax.broadcasted_iota(jnp.int32, (blk, 2 * blk), 1)
    rel = row - col
    order = sorted(DILATED_PATTERNS, key=lambda p: -p[1])
    for pi, (window, dil) in enumerate(order):
        n_keys = window // dil
        nblk = t // (dil * blk)

        def block(idx, carry, pi=pi, dil=dil, n_keys=n_keys, nblk=nblk):
            r = idx // nblk
            gq = (idx % nblk) * blk
            ks = jnp.maximum(gq - blk, 0)
            q_rows = pl.ds(gq * dil + r, blk, stride=dil) if dil > 1 else pl.ds(pl.multiple_of(gq, blk), blk)
            k_rows = pl.ds(ks * dil + r, 2 * blk, stride=dil) if dil > 1 else pl.ds(pl.multiple_of(ks, blk), 2 * blk)
            dist = rel + (gq - ks)
            bias = jnp.where(dist >= 0, jnp.where(dist <= n_keys, 0.0, NEG_BIG), NEG_BIG)
            q2 = q_ref[0, 0, q_rows, :]
            k2 = k_ref[0, 0, k_rows, :].astype(BF16)
            v2 = v_ref[0, 0, k_rows, :].astype(BF16)
            os, ms, ls = [], [], []
            for half in range(HEADS_PER_LANE_TILE):
                qh = (q2 * head_mask[half]).astype(BF16)
                s = lax.dot_general(qh, k2, (((1,), (1,)), ((), ())), preferred_element_type=F32) + bias
                m = jnp.max(s, axis=1, keepdims=True)
                p = jnp.exp(s - m)
                ls.append(jnp.sum(p, axis=1, keepdims=True))
                ms.append(m)
                os.append(jnp.dot(p.astype(BF16), v2, preferred_element_type=F32))
            first = lane_o < HEAD_DIM
            o_t = jnp.where(first, os[0], os[1])
            m_t = jnp.where(first, ms[0], ms[1])
            l_t = jnp.where(first, ls[0], ls[1])
            if pi == 0:
                acc[q_rows, :] = o_t
                mrun[q_rows, :] = m_t
                lrun[q_rows, :] = l_t
            else:
                m_o = mrun[q_rows, :]
                m_n = jnp.maximum(m_o, m_t)
                a_o = jnp.exp(m_o - m_n)
                a_t = jnp.exp(m_t - m_n)
                acc_n = a_o * acc[q_rows, :] + a_t * o_t
                l_n = a_o * lrun[q_rows, :] + a_t * l_t
                if pi == len(order) - 1:
                    o_ref[0, 0, q_rows, :] = (acc_n / l_n).astype(o_ref.dtype)
                else:
                    acc[q_rows, :] = acc_n
                    mrun[q_rows, :] = m_n
                    lrun[q_rows, :] = l_n
            return carry

        lax.fori_loop(0, dil * nblk, block, 0)


def _dilated_attention(q, k, v):
    b, n_lt, t, _ = q.shape
    for window, dil in DILATED_PATTERNS:
        assert window // dil == ATTN_BLOCK and t % (dil * 2 * ATTN_BLOCK) == 0
    assert DILATED_PATTERNS[0][1] == 1
    spec = pl.BlockSpec((1, 1, t, LANES), lambda i, j: (i, j, 0, 0))
    return pl.pallas_call(
        _dilated_attn_kernel,
        grid=(b, n_lt),
        in_specs=[spec, spec, spec],
        out_specs=spec,
        out_shape=jax.ShapeDtypeStruct((b, n_lt, t, LANES), BF16),
        scratch_shapes=[pltpu.VMEM((t, LANES), F32) for _ in range(3)],
        compiler_params=_cparams("arbitrary", "arbitrary"),
    )(q, k, v)


def _decode_kernel(qt_ref, kt_ref, vt_ref, ck_ref, cv_ref, ok_ref, ov_ref, at_ref, *, hb):
    i = pl.program_id(0)
    hg = pl.program_id(1)
    w = ck_ref.shape[3]
    bs = qt_ref.shape[1]
    sel = lax.broadcasted_iota(jnp.int32, (HEAD_DIM, bs), 1) == i

    def column(ref, rs):
        return jnp.sum(jnp.where(sel, ref[rs, :], 0.0), axis=1, keepdims=True)

    pos = lax.broadcasted_iota(jnp.int32, (1, w), 1)
    dist = w - pos
    cnt = jnp.zeros((1, w), F32)
    for window, dil in DILATED_PATTERNS:
        cnt = cnt + jnp.where((dist & (dil - 1)) == 0, jnp.where(dist <= window, 1.0, 0.0), 0.0)
    n_pat = float(len(DILATED_PATTERNS))
    last = lax.broadcasted_iota(jnp.int32, (HEAD_DIM, w), 1) == w - 1

    @pl.when((i == 0) & (hg == 0))
    def _():
        at_ref[...] = jnp.zeros(at_ref.shape, F32)

    for h in range(hb):
        rs = pl.ds(pl.multiple_of((hg * hb + h) * HEAD_DIM, HEAD_DIM), HEAD_DIM)
        qc, kc, vc = column(qt_ref, rs), column(kt_ref, rs), column(vt_ref, rs)
        kk = ck_ref[0, h]
        vv = cv_ref[0, h]
        s = jnp.sum(kk * qc, axis=0, keepdims=True)
        s_new = jnp.sum(kc * qc, axis=0, keepdims=True)
        s = jnp.where(cnt > 0.0, s, NEG_BIG)
        m = jnp.maximum(jnp.max(s, axis=1, keepdims=True), s_new)
        p = cnt * jnp.exp(s - m)
        p_new = n_pat * jnp.exp(s_new - m)
        l = jnp.sum(p, axis=1, keepdims=True) + p_new
        o = (jnp.sum(vv * p, axis=1, keepdims=True) + p_new * vc) / l
        at_ref[rs, :] = jnp.where(sel, o, at_ref[rs, :])
        ok_ref[0, h] = jnp.where(last, kc, pltpu.roll(kk, w - 1, 1))
        ov_ref[0, h] = jnp.where(last, vc, pltpu.roll(vv, w - 1, 1))


def _decode_attention(qt, kt, vt, ck, cv):
    bs, n_heads, hd, w = ck.shape
    aw = n_heads * hd
    for window, dil in DILATED_PATTERNS:
        assert window <= w and dil & (dil - 1) == 0
    hb = DECODE_HEADS
    full = pl.BlockSpec((aw, bs), lambda i, j: (0, 0))
    buf = pl.BlockSpec((1, hb, hd, w), lambda i, j: (i, j, 0, 0))
    return pl.pallas_call(
        functools.partial(_decode_kernel, hb=hb),
        grid=(bs, n_heads // hb),
        in_specs=[full, full, full, buf, buf],
        out_specs=[buf, buf, full],
        out_shape=[jax.ShapeDtypeStruct(ck.shape, F32), jax.ShapeDtypeStruct(cv.shape, F32),
                   jax.ShapeDtypeStruct((aw, bs), F32)],
        compiler_params=_cparams("arbitrary", "arbitrary"),
    )(qt, kt, vt, ck, cv)


def _s5_discretise(lam_re, lam_im, log_dt, b_re, b_im):
    dt = jnp.exp(log_dt)[:, None]
    mag = jnp.exp(lam_re * dt)
    ar = mag * jnp.cos(lam_im * dt)
    ai = mag * jnp.sin(lam_im * dt)
    den = lam_re * lam_re + lam_im * lam_im
    fr = ((ar - 1.0) * lam_re + ai * lam_im) / den
    fi = (ai * lam_re - (ar - 1.0) * lam_im) / den
    bbr = fr[..., None] * b_re - fi[..., None] * b_im
    bbi = fr[..., None] * b_im + fi[..., None] * b_re
    return dt, ar, ai, bbr, bbi


def _s5_chunk_matrices(lam_re, lam_im, log_dt, b_re, b_im, c_re, c_im):
    lc = SSM_CHUNK
    g, p, c = b_re.shape
    dt, _, _, bbr, bbi = _s5_discretise(lam_re, lam_im, log_dt, b_re, b_im)
    kk = jnp.arange(lc + 1, dtype=F32)[:, None, None]
    mag = jnp.exp(kk * lam_re * dt)
    apr = mag * jnp.cos(kk * lam_im * dt)
    api = mag * jnp.sin(kk * lam_im * dt)
    akb_r = apr[:lc, :, :, None] * bbr - api[:lc, :, :, None] * bbi
    akb_i = apr[:lc, :, :, None] * bbi + api[:lc, :, :, None] * bbr
    kern = (jnp.einsum('gop,kgpc->kgoc', c_re, akb_r, precision=HIGHEST)
            - jnp.einsum('gop,kgpc->kgoc', c_im, akb_i, precision=HIGHEST))
    ii = jnp.arange(lc)
    lag = ii[None, :] - ii[:, None]
    toe = jnp.where((lag >= 0)[:, :, None, None, None], kern[jnp.clip(lag, 0, lc - 1)], 0.0)
    toe = toe.transpose(2, 0, 4, 1, 3).reshape(g, lc * c, lc * c)
    rev = lc - 1 - ii
    ws_r = akb_r[rev].transpose(1, 0, 3, 2).reshape(g, lc * c, p)
    ws_i = akb_i[rev].transpose(1, 0, 3, 2).reshape(g, lc * c, p)
    a1r, a1i = apr[1:], api[1:]
    ca_r = c_re[None] * a1r[:, :, None, :] - c_im[None] * a1i[:, :, None, :]
    ca_i = c_re[None] * a1i[:, :, None, :] + c_im[None] * a1r[:, :, None, :]
    wy_r = ca_r.transpose(1, 3, 0, 2).reshape(g, p, lc * c)
    wy_i = (-ca_i).transpose(1, 3, 0, 2).reshape(g, p, lc * c)

    def pair_diag(m):
        r, s = m.shape[1:]
        m2 = m.reshape(g // 2, 2, r, s)
        z = jnp.zeros((g // 2, r, s), m.dtype)
        top = jnp.concatenate([m2[:, 0], z], axis=2)
        bot = jnp.concatenate([z, m2[:, 1]], axis=2)
        return jnp.concatenate([top, bot], axis=1)

    alc_r = apr[lc].reshape(1, g * p)
    alc_i = api[lc].reshape(1, g * p)
    return (toe.astype(BF16), pair_diag(ws_r).astype(BF16), pair_diag(ws_i).astype(BF16),
            pair_diag(wy_r).astype(BF16), pair_diag(wy_i).astype(BF16), alc_r, alc_i)


def _s5_step_matrices(lam_re, lam_im, log_dt, b_re, b_im, c_re, c_im):
    g, p, c = b_re.shape
    _, ar, ai, bbr, bbi = _s5_discretise(lam_re, lam_im, log_dt, b_re, b_im)
    eye = jnp.eye(g, dtype=F32)
    bd_br = jnp.einsum('gpc,gh->gchp', bbr, eye).reshape(g * c, g * p)
    bd_bi = jnp.einsum('gpc,gh->gchp', bbi, eye).reshape(g * c, g * p)
    bd_cr = jnp.einsum('gcp,gh->gphc', c_re, eye).reshape(g * p, g * c)
    bd_ci = jnp.einsum('gcp,gh->gphc', c_im, eye).reshape(g * p, g * c)
    return bd_br, bd_bi, bd_cr, bd_ci, ar.reshape(1, g * p), ai.reshape(1, g * p)


def _s5_prompt_kernel(u_ref, toe_ref, wsr_ref, wsi_ref, wyr_ref, wyi_ref, ar_ref, ai_ref,
                      y_ref, hr_ref, hi_ref, sr, si, hr_hist, hi_hist):
    nc = u_ref.shape[1]
    n_pairs = wsr_ref.shape[0]
    pw = wsr_ref.shape[1]
    sw = wsr_ref.shape[2]
    gw = toe_ref.shape[1]
    for pr in range(n_pairs):
        up = u_ref[0, :, pr * pw:(pr + 1) * pw]
        sr[:, pr * sw:(pr + 1) * sw] = jnp.dot(up, wsr_ref[pr], preferred_element_type=F32)
        si[:, pr * sw:(pr + 1) * sw] = jnp.dot(up, wsi_ref[pr], preferred_element_type=F32)

    ar = ar_ref[...]
    ai = ai_ref[...]
    nstate = ar.shape[1]
    rowid = lax.broadcasted_iota(jnp.int32, (8, nstate), 0)

    def eight_chunks(c8, carry):
        h_r, h_i = carry
        base = pl.multiple_of(c8 * 8, 8)
        s_r8 = sr[pl.ds(base, 8), :]
        s_i8 = si[pl.ds(base, 8), :]
        t_r = jnp.zeros((8, nstate), F32)
        t_i = jnp.zeros((8, nstate), F32)
        for r in range(8):
            t_r = jnp.where(rowid == r, h_r, t_r)
            t_i = jnp.where(rowid == r, h_i, t_i)
            n_r = ar * h_r - ai * h_i + s_r8[r:r + 1, :]
            n_i = ar * h_i + ai * h_r + s_i8[r:r + 1, :]
            h_r, h_i = n_r, n_i
        hr_hist[pl.ds(base, 8), :] = t_r
        hi_hist[pl.ds(base, 8), :] = t_i
        return h_r, h_i

    zero = jnp.zeros((1, nstate), F32)
    h_r, h_i = lax.fori_loop(0, nc // 8, eight_chunks, (zero, zero))
    hr_ref[0] = h_r
    hi_ref[0] = h_i

    for pr in range(n_pairs):
        st = slice(pr * sw, (pr + 1) * sw)
        y2 = (jnp.dot(hr_hist[:, st].astype(BF16), wyr_ref[pr], preferred_element_type=F32)
              + jnp.dot(hi_hist[:, st].astype(BF16), wyi_ref[pr], preferred_element_type=F32))
        for gg in range(2):
            g = pr * 2 + gg
            cols = slice(g * gw, (g + 1) * gw)
            yi = jnp.dot(u_ref[0, :, cols], toe_ref[g], preferred_element_type=F32)
            y_ref[0, :, cols] = (yi + y2[:, gg * gw:(gg + 1) * gw]).astype(y_ref.dtype)


def _s5_prompt(u_perm, mats):
    b, nc, wid = u_perm.shape
    nstate = mats[-1].shape[1]
    full = lambda a: pl.BlockSpec(a.shape, lambda i: (0,) * a.ndim)
    return pl.pallas_call(
        _s5_prompt_kernel,
        grid=(b,),
        in_specs=[pl.BlockSpec((1, nc, wid), lambda i: (i, 0, 0))] + [full(a) for a in mats],
        out_specs=[pl.BlockSpec((1, nc, wid), lambda i: (i, 0, 0)),
                   pl.BlockSpec((1, 1, nstate), lambda i: (i, 0, 0)),
                   pl.BlockSpec((1, 1, nstate), lambda i: (i, 0, 0))],
        out_shape=[jax.ShapeDtypeStruct((b, nc, wid), BF16),
                   jax.ShapeDtypeStruct((b, 1, nstate), F32),
                   jax.ShapeDtypeStruct((b, 1, nstate), F32)],
        scratch_shapes=[pltpu.VMEM((nc, nstate), F32) for _ in range(4)],
        compiler_params=_cparams("arbitrary"),
    )(u_perm, *mats)


def _s5_step_kernel(u_ref, h0r_ref, h0i_ref, bbr_ref, bbi_ref, cr_ref, ci_ref, ar_ref, ai_ref,
                    y_ref, xr_ref, xi_ref):
    u = u_ref[...]
    ar = ar_ref[...]
    ai = ai_ref[...]
    h0r = h0r_ref[...]
    h0i = h0i_ref[...]
    xr = ar * h0r - ai * h0i + jnp.dot(u, bbr_ref[...], precision=HIGHEST, preferred_element_type=F32)
    xi = ar * h0i + ai * h0r + jnp.dot(u, bbi_ref[...], precision=HIGHEST, preferred_element_type=F32)
    xr_ref[...] = xr
    xi_ref[...] = xi
    y_ref[...] = (jnp.dot(xr, cr_ref[...], precision=HIGHEST, preferred_element_type=F32)
                  - jnp.dot(xi, ci_ref[...], precision=HIGHEST, preferred_element_type=F32))


def _s5_step(u, h0r, h0i, mats):
    n, wid = u.shape
    nstate = h0r.shape[1]
    return pl.pallas_call(
        _s5_step_kernel,
        out_shape=[jax.ShapeDtypeStruct((n, wid), F32),
                   jax.ShapeDtypeStruct((n, nstate), F32),
                   jax.ShapeDtypeStruct((n, nstate), F32)],
        compiler_params=pltpu.CompilerParams(vmem_limit_bytes=VMEM_LIMIT),
    )(u, h0r, h0i, *mats)


def _split_bf16(a):
    hi = a.astype(BF16)
    return hi, (a - hi.astype(F32)).astype(BF16)


def _mix_kernel(x_ref, mod_ref, attn_ref, ys_ref, u_ref, d_ref, wglu_ref, bglu_ref, ga_ref, gs_ref,
                wout_ref, g2_ref, wrh_ref, wrl_ref, br_ref, x1_ref, h2_ref, gate_ref, idx_ref, *, lane_tiled_attn):
    aw = ys_ref.shape[1]
    if lane_tiled_attn:
        attn = jnp.concatenate([attn_ref[0, i] for i in range(attn_ref.shape[1])], axis=1).astype(F32)
    else:
        attn = attn_ref[...]
    a_n = _rms(attn) * ga_ref[...]

    y = ys_ref[...].astype(F32) + d_ref[...] * u_ref[...].astype(F32)
    y = 0.5 * y * (1.0 + jnp.tanh(np.float32(np.sqrt(2.0 / np.pi)) * (y + 0.044715 * (y * y * y))))
    z = jnp.dot(y.astype(BF16), wglu_ref[...], preferred_element_type=F32) + bglu_ref[...]
    ssm = y * jax.nn.sigmoid(z)
    s_n = _rms(ssm) * gs_ref[...]

    mixed = (jnp.dot(a_n.astype(BF16), wout_ref[0:aw, :], preferred_element_type=F32)
             + jnp.dot(s_n.astype(BF16), wout_ref[aw:, :], preferred_element_type=F32))
    x1 = x_ref[...] + mod_ref[0, 2] * mixed
    x1_ref[...] = x1
    h2 = _rms(x1) * g2_ref[...] * (1.0 + mod_ref[0, 4]) + mod_ref[0, 3]
    h2_ref[...] = h2.astype(BF16)

    h_hi, h_lo = _split_bf16(h2)
    logits = (jnp.dot(h_hi, wrh_ref[...], preferred_element_type=F32)
              + jnp.dot(h_lo, wrh_ref[...], preferred_element_type=F32)
              + jnp.dot(h_hi, wrl_ref[...], preferred_element_type=F32)) + br_ref[...]
    lane = lax.broadcasted_iota(jnp.int32, logits.shape, 1)
    lane_f = lane.astype(F32)
    cur = logits
    vals, idxs = [], []
    for _ in range(TOP_K):
        mx = jnp.max(cur, axis=1, keepdims=True)
        ix = jnp.min(jnp.where(cur == mx, lane_f, float(LANES)), axis=1, keepdims=True)
        vals.append(mx)
        idxs.append(ix)
        cur = jnp.where(lane_f == ix, -jnp.inf, cur)
    exps = [jnp.exp(v - vals[0]) for v in vals]
    den = exps[0]
    for e in exps[1:]:
        den = den + e
    gate = jnp.zeros(logits.shape, F32)
    idx = jnp.zeros(logits.shape, F32)
    for k in range(TOP_K):
        gate = jnp.where(lane == k, exps[k] / den, gate)
        idx = jnp.where(lane == k, idxs[k], idx)
    gate_ref[...] = gate
    idx_ref[...] = idx.astype(jnp.int32)


def _mix(x2d, mod, attn, ys, u, weights, tm, rows_per_mod):
    n, d = x2d.shape
    aw = ys.shape[1]
    r = mod.shape[2]
    tiles_per_mod = max(rows_per_mod // tm, 1)
    mod_map = (lambda i: (i // tiles_per_mod, 0, 0, 0)) if r == 1 else (lambda i: (0, 0, i, 0))
    rowspec = lambda w: pl.BlockSpec((tm, w), lambda i: (i, 0))
    full = lambda a: pl.BlockSpec(a.shape, lambda i: (0,) * a.ndim)
    lane_tiled = attn.ndim == 4
    if lane_tiled:
        attn_spec = pl.BlockSpec((1, attn.shape[1], tm, LANES),
                                 lambda i: (i // tiles_per_mod, 0, i % tiles_per_mod, 0))
    else:
        attn_spec = rowspec(aw)
    return pl.pallas_call(
        functools.partial(_mix_kernel, lane_tiled_attn=lane_tiled),
        grid=(n // tm,),
        in_specs=[rowspec(d), pl.BlockSpec((1, 6, r, d), mod_map), attn_spec, rowspec(aw), rowspec(aw)]
        + [full(w) for w in weights],
        out_specs=[rowspec(d), rowspec(d), rowspec(LANES), rowspec(LANES)],
        out_shape=[jax.ShapeDtypeStruct((n, d), F32), jax.ShapeDtypeStruct((n, d), BF16),
                   jax.ShapeDtypeStruct((n, LANES), F32), jax.ShapeDtypeStruct((n, LANES), jnp.int32)],
        compiler_params=_cparams("arbitrary"),
    )(x2d, mod, attn, ys, u, *weights)


def _moe_kernel(be_ref, first_ref, nb_ref, x_ref, wg_ref, bg_ref, wu_ref, bu_ref, wd_ref, bd_ref, o_ref,
                wg_bf, wu_bf, wd_bf):
    i = pl.program_id(0)

    @pl.when(first_ref[i] == 1)
    def _():
        wg_bf[...] = wg_ref[0].astype(BF16)
        wu_bf[...] = wu_ref[0].astype(BF16)
        wd_bf[...] = wd_ref[0].astype(BF16)

    @pl.when(i < nb_ref[0])
    def _():
        x = x_ref[...]
        g = jnp.dot(x, wg_bf[...], preferred_element_type=F32) + bg_ref[0]
        up = jnp.dot(x, wu_bf[...], preferred_element_type=F32) + bu_ref[0]
        g = jnp.minimum(g, SWIGLU_LIMIT)
        up = jnp.clip(up, -SWIGLU_LIMIT, SWIGLU_LIMIT)
        hid = (up + 1.0) * g * jax.nn.sigmoid(SWIGLU_ALPHA * g)
        y = jnp.dot(hid.astype(BF16), wd_bf[...], preferred_element_type=F32) + bd_ref[0]
        o_ref[...] = y.astype(o_ref.dtype)

    @pl.when(i >= nb_ref[0])
    def _():
        o_ref[...] = jnp.zeros(o_ref.shape, o_ref.dtype)


def _moe_experts(xb, block_e, first, n_used, wg, bg, wu, bu, wd, bd):
    n_rows, d = xb.shape
    e, _, f = wg.shape
    nb = n_rows // MOE_BLOCK
    wmap = lambda i, be, fi, nu: (be[i], 0, 0)
    grid_spec = pltpu.PrefetchScalarGridSpec(
        num_scalar_prefetch=3,
        grid=(nb,),
        in_specs=[pl.BlockSpec((MOE_BLOCK, d), lambda i, be, fi, nu: (i, 0)),
                  pl.BlockSpec((1, d, f), wmap), pl.BlockSpec((1, 1, f), wmap),
                  pl.BlockSpec((1, d, f), wmap), pl.BlockSpec((1, 1, f), wmap),
                  pl.BlockSpec((1, f, d), wmap), pl.BlockSpec((1, 1, d), wmap)],
        out_specs=pl.BlockSpec((MOE_BLOCK, d), lambda i, be, fi, nu: (i, 0)),
        scratch_shapes=[pltpu.VMEM((d, f), BF16), pltpu.VMEM((d, f), BF16), pltpu.VMEM((f, d), BF16)],
    )
    return pl.pallas_call(
        _moe_kernel,
        grid_spec=grid_spec,
        out_shape=jax.ShapeDtypeStruct((n_rows, d), BF16),
        compiler_params=_cparams("arbitrary"),
    )(block_e, first, n_used, xb, wg, bg.reshape(e, 1, f), wu, bu.reshape(e, 1, f), wd, bd.reshape(e, 1, d))


def _final_kernel(x1_ref, mod_ref, gate_ref, yg_ref, gf_ref, o_ref):
    gate = gate_ref[...]
    acc = jnp.zeros(x1_ref.shape, F32)
    for k in range(TOP_K):
        acc = acc + gate[:, k:k + 1] * yg_ref[k].astype(F32)
    x = x1_ref[...] + mod_ref[0, 5] * acc
    o_ref[...] = _rms(x) * gf_ref[...]


def _final(x1, mod, gates, yg, g_final, tm, rows_per_mod, row0):
    n, d = x1.shape
    r = mod.shape[2]
    tiles_per_mod = max(rows_per_mod // tm, 1)
    mod_map = (lambda i: (i // tiles_per_mod, 0, 0, 0)) if r == 1 else (lambda i: (0, 0, i, 0))
    assert row0 % tm == 0
    off = row0 // tm
    return pl.pallas_call(
        _final_kernel,
        grid=(n // tm,),
        in_specs=[pl.BlockSpec((tm, d), lambda i: (i, 0)),
                  pl.BlockSpec((1, 6, r, d), mod_map),
                  pl.BlockSpec((tm, LANES), lambda i: (i + off, 0)),
                  pl.BlockSpec((TOP_K, tm, d), lambda i: (0, i + off, 0)),
                  pl.BlockSpec((1, d), lambda i: (0, 0))],
        out_specs=pl.BlockSpec((tm, d), lambda i: (i, 0)),
        out_shape=jax.ShapeDtypeStruct((n, d), F32),
        compiler_params=_cparams("arbitrary"),
    )(x1, mod, gates, yg, g_final.reshape(1, d))


def kernel(x_prompt, x_sample, c_prompt, c_sample, cache_k, cache_v, state_ssm_re, state_ssm_im, w_ada, b_ada, g_norm1, g_norm2, w_in, lambda_re, lambda_im, log_dt, b_ssm_re, b_ssm_im, c_ssm_re, c_ssm_im, d_ssm, w_glu, b_glu, g_attn_out, g_ssm_out, w_out, w_router, b_router, w_gate, b_gate, w_up, b_up, w_down, b_down, g_final):
    assert w_ada.shape[0] == 1, "one layer"
    b, t, d = x_prompt.shape
    bs = x_sample.shape[0]
    assert x_sample.shape[1] == 1
    wbuf, n_heads = cache_k.shape[2], cache_k.shape[3]
    aw = n_heads * HEAD_DIM
    n_groups, n_state = lambda_re.shape[1:]
    sw = n_groups * SSM_GROUP
    n_exp = w_router.shape[2]
    keep = min(max(w for w, _ in DILATED_PATTERNS), t)
    tm = min(ROW_TILE, t)

    mod = _ada(jnp.concatenate([c_prompt, c_sample], axis=0), w_ada[0], b_ada[0])
    mod_p = mod[:b].reshape(b, 6, 1, d)
    mod_s = mod[b:].reshape(bs, 6, d).transpose(1, 0, 2)[None]

    w_in_bf = w_in[0].astype(BF16)
    g1 = g_norm1[0].reshape(1, d)
    cos_p, sin_p = _rope_tables(jnp.arange(t), n_heads)
    cos_s, sin_s = _rope_tables(jnp.full((1,), PAST_LEN), n_heads)
    q_p, k_p, v_p, kt_p, vt_p, u_p = _inproj_prompt(x_prompt, mod_p, g1, w_in_bf, cos_p, sin_p, tm)
    qt_s, kt_s, vt_s, u_s = _inproj_decode(x_sample.reshape(bs, d), mod_s, g1, w_in_bf, cos_s, sin_s)

    attn_p = _dilated_attention(q_p, k_p, v_p)
    to_hdp = lambda c: jnp.transpose(c[0], (0, 2, 3, 1))
    from_hdp = lambda c: jnp.transpose(c, (0, 3, 1, 2))[None]
    ck_new, cv_new, attn_t = _decode_attention(qt_s, kt_s, vt_s, to_hdp(cache_k), to_hdp(cache_v))
    attn_s = attn_t.T

    ssm_params = (lambda_re[0], lambda_im[0], log_dt[0], b_ssm_re[0], b_ssm_im[0], c_ssm_re[0], c_ssm_im[0])
    lc, gc = SSM_CHUNK, SSM_GROUP
    nc = t // lc
    u_perm = (u_p.reshape(b, nc, lc, n_groups, gc).transpose(0, 1, 3, 2, 4).reshape(b, nc, n_groups * lc * gc))
    y_perm, hr_p, hi_p = _s5_prompt(u_perm, _s5_chunk_matrices(*ssm_params))
    ys_p = (y_perm.reshape(b, nc, n_groups, lc, gc).transpose(0, 1, 3, 2, 4).reshape(b * t, sw))
    ys_s, hr_s, hi_s = _s5_step(u_s, state_ssm_re[0].reshape(bs, n_groups * n_state),
                                state_ssm_im[0].reshape(bs, n_groups * n_state),
                                _s5_step_matrices(*ssm_params))

    wr_pad = jnp.zeros((d, LANES), F32).at[:, :n_exp].set(w_router[0])
    wr_hi = wr_pad.astype(BF16)
    wr_lo = (wr_pad - wr_hi.astype(F32)).astype(BF16)
    br_pad = jnp.full((1, LANES), NEG_BIG, F32).at[0, :n_exp].set(b_router[0])
    mix_w = (d_ssm[0].reshape(1, sw), w_glu[0].astype(BF16), b_glu[0].reshape(1, sw),
             g_attn_out[0].reshape(1, aw), g_ssm_out[0].reshape(1, sw), w_out[0].astype(BF16),
             g_norm2[0].reshape(1, d), wr_hi, wr_lo, br_pad)
    x1_p, h2_p, gate_p, idx_p = _mix(x_prompt.reshape(b * t, d), mod_p, attn_p, ys_p, u_p.reshape(b * t, sw),
                                     mix_w, tm, t)
    x1_s, h2_s, gate_s, idx_s = _mix(x_sample.reshape(bs, d), mod_s, attn_s, ys_s, u_s, mix_w, bs, 1)

    m_tok = b * t + bs
    h2 = jnp.concatenate([h2_p, h2_s], axis=0)
    gates = jnp.concatenate([gate_p, gate_s], axis=0)
    top_e = jnp.concatenate([idx_p, idx_s], axis=0)[:, :TOP_K]
    n_assign = m_tok * TOP_K
    flat_e = top_e.reshape(-1)
    onehot = (flat_e[:, None] == jnp.arange(n_exp)[None, :]).astype(jnp.int32)
    csum = jnp.cumsum(onehot, axis=0)
    rank = jnp.sum(csum * onehot, axis=1) - 1
    counts = csum[-1]
    padded = (counts + MOE_BLOCK - 1) // MOE_BLOCK * MOE_BLOCK
    pend = jnp.cumsum(padded)
    pstart = pend - padded
    dest = jnp.sum(pstart[None, :] * onehot, axis=1) + rank
    nb = -(-n_assign // MOE_BLOCK) + n_exp
    n_rows = nb * MOE_BLOCK
    row_tok = jnp.zeros((n_rows,), jnp.int32).at[dest].set(
        jnp.arange(n_assign, dtype=jnp.int32) // TOP_K, unique_indices=True, mode="promise_in_bounds")
    block_start = jnp.arange(nb, dtype=jnp.int32) * MOE_BLOCK
    block_e = jnp.minimum(jnp.sum((pend[None, :] <= block_start[:, None]).astype(jnp.int32), axis=1), n_exp - 1)
    first = jnp.concatenate([jnp.ones((1,), jnp.int32), (block_e[1:] != block_e[:-1]).astype(jnp.int32)])
    n_used = (pend[-1] // MOE_BLOCK).astype(jnp.int32).reshape(1)
    xb = h2.at[row_tok].get(mode="promise_in_bounds")
    yb = _moe_experts(xb, block_e, first, n_used, w_gate[0], b_gate[0], w_up[0], b_up[0], w_down[0], b_down[0])
    yg = yb.at[dest.reshape(m_tok, TOP_K).T].get(mode="promise_in_bounds")

    y_prompt = _final(x1_p, mod_p, gates, yg, g_final, tm, t, 0).reshape(b, t, d)
    y_sample = _final(x1_s, mod_s, gates, yg, g_final, bs, 1, b * t).reshape(bs, 1, d)

    k_win_p = from_hdp(kt_p[:, :, t - keep:].reshape(b, n_heads, HEAD_DIM, keep))
    v_win_p = from_hdp(vt_p[:, :, t - keep:].reshape(b, n_heads, HEAD_DIM, keep))
    st = lambda a, n: a.reshape(1, n, n_groups, n_state)
    return (y_prompt, y_sample, k_win_p, v_win_p, st(hr_p, b), st(hi_p, b),
            from_hdp(ck_new), from_hdp(cv_new), st(hr_s, bs), st(hi_s, bs))
```

```python
import functools

import jax
import jax.numpy as jnp
import numpy as np
from jax import lax
from jax.experimental import pallas as pl
from jax.experimental.pallas import tpu as pltpu

F32 = jnp.float32
BF16 = jnp.bfloat16
HIGHEST = lax.Precision.HIGHEST

HEAD_DIM = 64
DILATED_PATTERNS = ((128, 1), (512, 4), (2048, 16))
ROPE_THETA = 10000.0
PAST_LEN = 8192
SSM_GROUP = 16
SSM_STATE = 64
TOP_K = 4
SWIGLU_LIMIT = 7.0
SWIGLU_ALPHA = 1.702
RMS_EPS = 1e-6

LANES = 128
HEADS_PER_LANE_TILE = LANES // HEAD_DIM
ATTN_BLOCK = 128
ATTN_UNROLL = 8
DECODE_HEADS = 4
SSM_CHUNK = 16
S5_SLAB = 2048
S5_PERMUTE_ROWS = 32
ROW_TILE = 512
MOE_BLOCK = 512
VMEM_LIMIT = 52 * 1024 * 1024
NEG_BIG = -1e30


def _cparams(*sem):
    return pltpu.CompilerParams(dimension_semantics=sem, vmem_limit_bytes=VMEM_LIMIT)


def _rms(x):
    return x * lax.rsqrt(jnp.mean(x * x, axis=-1, keepdims=True) + RMS_EPS)


def _ada_kernel(c_ref, w_ref, b_ref, o_ref):
    c = c_ref[...]
    s = c * jax.nn.sigmoid(c)
    o_ref[...] = jnp.dot(s, w_ref[...], precision=HIGHEST, preferred_element_type=F32) + b_ref[...]


def _ada(c, w, b):
    n, d = c.shape
    nout = w.shape[1]
    return pl.pallas_call(
        _ada_kernel,
        grid=(nout // d,),
        in_specs=[pl.BlockSpec((n, d), lambda j: (0, 0)),
                  pl.BlockSpec((d, d), lambda j: (0, j)),
                  pl.BlockSpec((1, d), lambda j: (0, j))],
        out_specs=pl.BlockSpec((n, d), lambda j: (0, j)),
        out_shape=jax.ShapeDtypeStruct((n, nout), F32),
        compiler_params=_cparams("arbitrary"),
    )(c, w, b.reshape(1, nout))


def _project(x, mod_ref, g_ref, w_ref, cos_ref, sin_ref, aw):
    h = _rms(x) * g_ref[...]
    h = h * (1.0 + mod_ref[0, 1]) + mod_ref[0, 0]
    proj = jnp.dot(h.astype(BF16), w_ref[...], preferred_element_type=F32)
    cos = cos_ref[...]
    sin = sin_ref[...]
    lane = lax.broadcasted_iota(jnp.int32, (1, aw), 1)
    first_half = (lane & (HEAD_DIM - 1)) < (HEAD_DIM // 2)

    def rope(t):
        rot = jnp.where(first_half, pltpu.roll(t, aw - HEAD_DIM // 2, 1), pltpu.roll(t, HEAD_DIM // 2, 1))
        return t * cos + rot * sin

    q = rope(proj[:, :aw]) * (HEAD_DIM ** -0.5)
    k = rope(proj[:, aw:2 * aw])
    return q, k, proj[:, 2 * aw:3 * aw], proj[:, 3 * aw:]


def _inproj_prompt_kernel(x_ref, mod_ref, g_ref, w_ref, cos_ref, sin_ref,
                          q_ref, k_ref, v_ref, kt_ref, vt_ref, u_ref, *, aw):
    q, k, v, u = _project(x_ref[0], mod_ref, g_ref, w_ref, cos_ref, sin_ref, aw)
    for hp in range(aw // LANES):
        cols = slice(hp * LANES, (hp + 1) * LANES)
        q_ref[0, hp] = q[:, cols]
        k_ref[0, hp] = k[:, cols]
        v_ref[0, hp] = v[:, cols]
    kt_ref[0] = k.T
    vt_ref[0] = v.T
    for lt in range(u_ref.shape[1]):
        u_ref[0, lt] = u[:, lt * LANES:(lt + 1) * LANES]


def _inproj_prompt(x, mod, g, w_bf, cos, sin, tm):
    b, t, d = x.shape
    nproj = w_bf.shape[1]
    aw = cos.shape[1]
    sw = nproj - 3 * aw
    n_lt = aw // LANES
    tok = lambda i, j: (j, i, 0)
    hp_major = lambda i, j: (j, 0, i, 0)
    transposed = lambda i, j: (j, 0, i)
    out_shapes = [jax.ShapeDtypeStruct((b, n_lt, t, LANES), F32)] * 3 + [
        jax.ShapeDtypeStruct((b, aw, t), F32), jax.ShapeDtypeStruct((b, aw, t), F32),
        jax.ShapeDtypeStruct((b, sw // LANES, t, LANES), F32)]
    return pl.pallas_call(
        functools.partial(_inproj_prompt_kernel, aw=aw),
        grid=(t // tm, b),
        in_specs=[pl.BlockSpec((1, tm, d), tok),
                  pl.BlockSpec((1, 6, 1, d), lambda i, j: (j, 0, 0, 0)),
                  pl.BlockSpec((1, d), lambda i, j: (0, 0)),
                  pl.BlockSpec((d, nproj), lambda i, j: (0, 0)),
                  pl.BlockSpec((tm, aw), lambda i, j: (i, 0)),
                  pl.BlockSpec((tm, aw), lambda i, j: (i, 0))],
        out_specs=[pl.BlockSpec((1, n_lt, tm, LANES), hp_major)] * 3 + [
            pl.BlockSpec((1, aw, tm), transposed), pl.BlockSpec((1, aw, tm), transposed),
            pl.BlockSpec((1, sw // LANES, tm, LANES), hp_major)],
        out_shape=out_shapes,
        compiler_params=_cparams("arbitrary", "arbitrary"),
    )(x, mod, g, w_bf, cos, sin)


def _inproj_decode_kernel(x_ref, mod_ref, g_ref, w_ref, cos_ref, sin_ref, qt_ref, kt_ref, vt_ref, u_ref, *, aw):
    q, k, v, u = _project(x_ref[...], mod_ref, g_ref, w_ref, cos_ref, sin_ref, aw)
    qt_ref[...] = q.T
    kt_ref[...] = k.T
    vt_ref[...] = v.T
    u_ref[...] = u


def _inproj_decode(x, mod, g, w_bf, cos, sin):
    n, d = x.shape
    nproj = w_bf.shape[1]
    aw = cos.shape[1]
    return pl.pallas_call(
        functools.partial(_inproj_decode_kernel, aw=aw),
        out_shape=[jax.ShapeDtypeStruct((aw, n), F32)] * 3 + [jax.ShapeDtypeStruct((n, nproj - 3 * aw), F32)],
        compiler_params=pltpu.CompilerParams(vmem_limit_bytes=VMEM_LIMIT),
    )(x, mod, g, w_bf, cos, sin)


def _rope_tables(pos, n_heads):
    half = HEAD_DIM // 2
    inv_freq = ROPE_THETA ** (-jnp.arange(half, dtype=F32) / half)
    ang = pos.astype(F32)[:, None] * inv_freq[None, :]
    cos = jnp.cos(ang)
    sin = jnp.sin(ang)
    cos_h = jnp.concatenate([cos, cos], axis=-1)
    sin_h = jnp.concatenate([-sin, sin], axis=-1)
    return jnp.tile(cos_h, (1, n_heads)), jnp.tile(sin_h, (1, n_heads))


def _dilated_attn_kernel(q_ref, k_ref, v_ref, o_ref, acc, mrun, lrun):
    t = q_ref.shape[2]
    blk = ATTN_BLOCK
    lane = lax.broadcasted_iota(jnp.int32, (1, LANES), 1)
    head_mask = [(lane < HEAD_DIM).astype(F32), (lane >= HEAD_DIM).astype(F32)]
    lane_o = lax.broadcasted_iota(jnp.int32, (blk, LANES), 1)
    row = lax.broadcasted_iota(jnp.int32, (blk, 2 * blk), 0)
    col = lax.broadcasted_iota(jnp.int32, (blk, 2 * blk), 1)
    rel = row - col
    order = sorted(DILATED_PATTERNS, key=lambda p: -p[1])
    for pi, (window, dil) in enumerate(order):
        n_keys = window // dil
        nblk = t // (dil * blk)

        def block(idx, carry, pi=pi, dil=dil, n_keys=n_keys, nblk=nblk):
            r = idx // nblk
            gq = (idx % nblk) * blk
            ks = jnp.maximum(gq - blk, 0)
            q_rows = pl.ds(gq * dil + r, blk, stride=dil) if dil > 1 else pl.ds(pl.multiple_of(gq, blk), blk)
            k_rows = pl.ds(ks * dil + r, 2 * blk, stride=dil) if dil > 1 else pl.ds(pl.multiple_of(ks, blk), 2 * blk)
            dist = rel + (gq - ks)
            bias = jnp.where(dist >= 0, jnp.where(dist <= n_keys, 0.0, NEG_BIG), NEG_BIG)
            q2 = q_ref[0, 0, q_rows, :]
            k2 = k_ref[0, 0, k_rows, :].astype(BF16)
            v2 = v_ref[0, 0, k_rows, :].astype(BF16)
            os, ms, ls = [], [], []
            for half in range(HEADS_PER_LANE_TILE):
                qh = (q2 * head_mask[half]).astype(BF16)
                s = lax.dot_general(qh, k2, (((1,), (1,)), ((), ())), preferred_element_type=F32) + bias
                m = jnp.max(s, axis=1, keepdims=True)
                p = jnp.exp(s - m)
                ls.append(jnp.sum(p, axis=1, keepdims=True))
                ms.append(m)
                os.append(jnp.dot(p.astype(BF16), v2, preferred_element_type=F32))
            first = lane_o < HEAD_DIM
            o_t = jnp.where(first, os[0], os[1])
            m_t = jnp.where(first, ms[0], ms[1])
            l_t = jnp.where(first, ls[0], ls[1])
            if pi == 0:
                acc[q_rows, :] = o_t
                mrun[q_rows, :] = m_t
                lrun[q_rows, :] = l_t
            else:
                m_o = mrun[q_rows, :]
                m_n = jnp.maximum(m_o, m_t)
                a_o = jnp.exp(m_o - m_n)
                a_t = jnp.exp(m_t - m_n)
                acc_n = a_o * acc[q_rows, :] + a_t * o_t
                l_n = a_o * lrun[q_rows, :] + a_t * l_t
                if pi == len(order) - 1:
                    o_ref[0, 0, q_rows, :] = (acc_n / l_n).astype(o_ref.dtype)
                else:
                    acc[q_rows, :] = acc_n
                    mrun[q_rows, :] = m_n
                    lrun[q_rows, :] = l_n
            return carry

        lax.fori_loop(0, dil * nblk, block, 0, unroll=ATTN_UNROLL)


def _dilated_attention(q, k, v):
    b, n_lt, t, _ = q.shape
    for window, dil in DILATED_PATTERNS:
        assert window // dil == ATTN_BLOCK and t % (dil * 2 * ATTN_BLOCK) == 0
    assert DILATED_PATTERNS[0][1] == 1
    spec = pl.BlockSpec((1, 1, t, LANES), lambda i, j: (i, j, 0, 0))
    return pl.pallas_call(
        _dilated_attn_kernel,
        grid=(b, n_lt),
        in_specs=[spec, spec, spec],
        out_specs=spec,
        out_shape=jax.ShapeDtypeStruct((b, n_lt, t, LANES), BF16),
        scratch_shapes=[pltpu.VMEM((t, LANES), F32) for _ in range(3)],
        compiler_params=_cparams("arbitrary", "arbitrary"),
    )(q, k, v)


def _decode_kernel(qt_ref, kt_ref, vt_ref, ck_ref, cv_ref, ok_ref, ov_ref, at_ref, *, hb):
    i = pl.program_id(0)
    hg = pl.program_id(1)
    w = ck_ref.shape[3]
    bs = qt_ref.shape[1]
    sel = lax.broadcasted_iota(jnp.int32, (HEAD_DIM, bs), 1) == i

    def column(ref, rs):
        return jnp.sum(jnp.where(sel, ref[rs, :], 0.0), axis=1, keepdims=True)

    pos = lax.broadcasted_iota(jnp.int32, (1, w), 1)
    dist = w - pos
    cnt = jnp.zeros((1, w), F32)
    for window, dil in DILATED_PATTERNS:
        cnt = cnt + jnp.where((dist & (dil - 1)) == 0, jnp.where(dist <= window, 1.0, 0.0), 0.0)
    n_pat = float(len(DILATED_PATTERNS))
    last = lax.broadcasted_iota(jnp.int32, (HEAD_DIM, w), 1) == w - 1

    @pl.when((i == 0) & (hg == 0))
    def _():
        at_ref[...] = jnp.zeros(at_ref.shape, F32)

    for h in range(hb):
        rs = pl.ds(pl.multiple_of((hg * hb + h) * HEAD_DIM, HEAD_DIM), HEAD_DIM)
        qc, kc, vc = column(qt_ref, rs), column(kt_ref, rs), column(vt_ref, rs)
        kk = ck_ref[0, h]
        vv = cv_ref[0, h]
        s = jnp.sum(kk * qc, axis=0, keepdims=True)
        s_new = jnp.sum(kc * qc, axis=0, keepdims=True)
        s = jnp.where(cnt > 0.0, s, NEG_BIG)
        m = jnp.maximum(jnp.max(s, axis=1, keepdims=True), s_new)
        p = cnt * jnp.exp(s - m)
        p_new = n_pat * jnp.exp(s_new - m)
        l = jnp.sum(p, axis=1, keepdims=True) + p_new
        o = (jnp.sum(vv * p, axis=1, keepdims=True) + p_new * vc) / l
        at_ref[rs, :] = jnp.where(sel, o, at_ref[rs, :])
        ok_ref[0, h] = jnp.where(last, kc, pltpu.roll(kk, w - 1, 1))
        ov_ref[0, h] = jnp.where(last, vc, pltpu.roll(vv, w - 1, 1))


def _decode_attention(qt, kt, vt, ck, cv):
    bs, n_heads, hd, w = ck.shape
    aw = n_heads * hd
    for window, dil in DILATED_PATTERNS:
        assert window <= w and dil & (dil - 1) == 0
    hb = DECODE_HEADS
    full = pl.BlockSpec((aw, bs), lambda i, j: (0, 0))
    buf = pl.BlockSpec((1, hb, hd, w), lambda i, j: (i, j, 0, 0))
    return pl.pallas_call(
        functools.partial(_decode_kernel, hb=hb),
        grid=(bs, n_heads // hb),
        in_specs=[full, full, full, buf, buf],
        out_specs=[buf, buf, full],
        out_shape=[jax.ShapeDtypeStruct(ck.shape, F32), jax.ShapeDtypeStruct(cv.shape, F32),
                   jax.ShapeDtypeStruct((aw, bs), F32)],
        compiler_params=_cparams("arbitrary", "arbitrary"),
    )(qt, kt, vt, ck, cv)


def _s5_discretise(lam_re, lam_im, log_dt, b_re, b_im):
    dt = jnp.exp(log_dt)[:, None]
    mag = jnp.exp(lam_re * dt)
    ar = mag * jnp.cos(lam_im * dt)
    ai = mag * jnp.sin(lam_im * dt)
    den = lam_re * lam_re + lam_im * lam_im
    fr = ((ar - 1.0) * lam_re + ai * lam_im) / den
    fi = (ai * lam_re - (ar - 1.0) * lam_im) / den
    bbr = fr[..., None] * b_re - fi[..., None] * b_im
    bbi = fr[..., None] * b_im + fi[..., None] * b_re
    return dt, ar, ai, bbr, bbi


def _s5_chunk_matrices(lam_re, lam_im, log_dt, b_re, b_im, c_re, c_im):
    lc = SSM_CHUNK
    g, p, c = b_re.shape
    dt, _, _, bbr, bbi = _s5_discretise(lam_re, lam_im, log_dt, b_re, b_im)
    kk = jnp.arange(lc + 1, dtype=F32)[:, None, None]
    mag = jnp.exp(kk * lam_re * dt)
    apr = mag * jnp.cos(kk * lam_im * dt)
    api = mag * jnp.sin(kk * lam_im * dt)
    akb_r = apr[:lc, :, :, None] * bbr - api[:lc, :, :, None] * bbi
    akb_i = apr[:lc, :, :, None] * bbi + api[:lc, :, :, None] * bbr
    kern = (jnp.einsum('gop,kgpc->kgoc', c_re, akb_r, precision=HIGHEST)
            - jnp.einsum('gop,kgpc->kgoc', c_im, akb_i, precision=HIGHEST))
    ii = jnp.arange(lc)
    lag = ii[None, :] - ii[:, None]
    toe = jnp.where((lag >= 0)[:, :, None, None, None], kern[jnp.clip(lag, 0, lc - 1)], 0.0)
    toe = toe.transpose(2, 0, 4, 1, 3).reshape(g, lc * c, lc * c)
    rev = lc - 1 - ii
    ws_r = akb_r[rev].transpose(1, 0, 3, 2).reshape(g, lc * c, p)
    ws_i = akb_i[rev].transpose(1, 0, 3, 2).reshape(g, lc * c, p)
    a1r, a1i = apr[1:], api[1:]
    ca_r = c_re[None] * a1r[:, :, None, :] - c_im[None] * a1i[:, :, None, :]
    ca_i = c_re[None] * a1i[:, :, None, :] + c_im[None] * a1r[:, :, None, :]
    wy_r = ca_r.transpose(1, 3, 0, 2).reshape(g, p, lc * c)
    wy_i = (-ca_i).transpose(1, 3, 0, 2).reshape(g, p, lc * c)

    def pair_diag(m):
        r, s = m.shape[1:]
        m2 = m.reshape(g // 2, 2, r, s)
        z = jnp.zeros((g // 2, r, s), m.dtype)
        top = jnp.concatenate([m2[:, 0], z], axis=2)
        bot = jnp.concatenate([z, m2[:, 1]], axis=2)
        return jnp.concatenate([top, bot], axis=1)

    alc_r = apr[lc].reshape(1, g * p)
    alc_i = api[lc].reshape(1, g * p)
    return (toe.astype(BF16), pair_diag(ws_r).astype(BF16), pair_diag(ws_i).astype(BF16),
            pair_diag(wy_r).astype(BF16), pair_diag(wy_i).astype(BF16), alc_r, alc_i)


def _s5_step_matrices(lam_re, lam_im, log_dt, b_re, b_im, c_re, c_im):
    g, p, c = b_re.shape
    _, ar, ai, bbr, bbi = _s5_discretise(lam_re, lam_im, log_dt, b_re, b_im)
    eye = jnp.eye(g, dtype=F32)
    bd_br = jnp.einsum('gpc,gh->gchp', bbr, eye).reshape(g * c, g * p)
    bd_bi = jnp.einsum('gpc,gh->gchp', bbi, eye).reshape(g * c, g * p)
    bd_cr = jnp.einsum('gcp,gh->gphc', c_re, eye).reshape(g * p, g * c)
    bd_ci = jnp.einsum('gcp,gh->gphc', c_im, eye).reshape(g * p, g * c)
    return bd_br, bd_bi, bd_cr, bd_ci, ar.reshape(1, g * p), ai.reshape(1, g * p)


def _s5_prompt_kernel(u_ref, toe_ref, wsr_ref, wsi_ref, wyr_ref, wyi_ref, ar_ref, ai_ref,
                      y_ref, hr_ref, hi_ref, uperm, yperm, sr, si, hr_hist, hi_hist, hcar_r, hcar_i):
    j = pl.program_id(1)
    nc = uperm.shape[0]
    lc, gc = SSM_CHUNK, SSM_GROUP
    gpt = LANES // gc
    n_lt = u_ref.shape[1]
    rb = S5_PERMUTE_ROWS
    gw = lc * gc
    lane_grp = lax.broadcasted_iota(jnp.int32, (1, LANES), 1) // gc

    def shifted(x, groups):
        sh = (groups * gc) % LANES
        return x if sh == 0 else pltpu.roll(x, sh, 1)

    def permute_in(rc, carry):
        r0 = pl.multiple_of(rc * rb, rb)
        for lt in range(n_lt):
            for h in range(lc // gpt):
                xs = [u_ref[0, lt, pl.ds(r0 * lc + h * gpt + il, rb, stride=lc), :] for il in range(gpt)]
                for gl in range(gpt):
                    d = jnp.zeros((rb, LANES), F32)
                    for il in range(gpt):
                        d = jnp.where(lane_grp == il, shifted(xs[il], il - gl), d)
                    c0 = (lt * gpt + gl) * gw + h * LANES
                    uperm[pl.ds(r0, rb), c0:c0 + LANES] = d.astype(BF16)
        return carry

    lax.fori_loop(0, nc // rb, permute_in, 0)

    n_pairs = wsr_ref.shape[0]
    pw = wsr_ref.shape[1]
    sw = wsr_ref.shape[2]
    for pr in range(n_pairs):
        up = uperm[:, pr * pw:(pr + 1) * pw]
        sr[:, pr * sw:(pr + 1) * sw] = jnp.dot(up, wsr_ref[pr], preferred_element_type=F32)
        si[:, pr * sw:(pr + 1) * sw] = jnp.dot(up, wsi_ref[pr], preferred_element_type=F32)

    ar = ar_ref[...]
    ai = ai_ref[...]
    nstate = ar.shape[1]
    rowid = lax.broadcasted_iota(jnp.int32, (8, nstate), 0)

    @pl.when(j == 0)
    def _():
        hcar_r[...] = jnp.zeros(hcar_r.shape, F32)
        hcar_i[...] = jnp.zeros(hcar_i.shape, F32)

    def eight_chunks(c8, carry):
        h_r, h_i = carry
        base = pl.multiple_of(c8 * 8, 8)
        s_r8 = sr[pl.ds(base, 8), :]
        s_i8 = si[pl.ds(base, 8), :]
        t_r = jnp.zeros((8, nstate), F32)
        t_i = jnp.zeros((8, nstate), F32)
        for r in range(8):
            t_r = jnp.where(rowid == r, h_r, t_r)
            t_i = jnp.where(rowid == r, h_i, t_i)
            n_r = ar * h_r - ai * h_i + s_r8[r:r + 1, :]
            n_i = ar * h_i + ai * h_r + s_i8[r:r + 1, :]
            h_r, h_i = n_r, n_i
        hr_hist[pl.ds(base, 8), :] = t_r
        hi_hist[pl.ds(base, 8), :] = t_i
        return h_r, h_i

    h_r, h_i = lax.fori_loop(0, nc // 8, eight_chunks, (hcar_r[...], hcar_i[...]))
    hcar_r[...] = h_r
    hcar_i[...] = h_i
    hr_ref[0] = h_r
    hi_ref[0] = h_i

    for pr in range(n_pairs):
        st = slice(pr * sw, (pr + 1) * sw)
        y2 = (jnp.dot(hr_hist[:, st].astype(BF16), wyr_ref[pr], preferred_element_type=F32)
              + jnp.dot(hi_hist[:, st].astype(BF16), wyi_ref[pr], preferred_element_type=F32))
        for gg in range(2):
            g = pr * 2 + gg
            cols = slice(g * gw, (g + 1) * gw)
            yi = jnp.dot(uperm[:, cols], toe_ref[g], preferred_element_type=F32)
            yperm[:, cols] = yi + y2[:, gg * gw:(gg + 1) * gw]

    def permute_out(rc, carry):
        r0 = pl.multiple_of(rc * rb, rb)
        for lt in range(n_lt):
            for h in range(lc // gpt):
                ds_ = [yperm[pl.ds(r0, rb), (lt * gpt + gl) * gw + h * LANES:(lt * gpt + gl) * gw + (h + 1) * LANES]
                       for gl in range(gpt)]
                for il in range(gpt):
                    yv = jnp.zeros((rb, LANES), F32)
                    for gl in range(gpt):
                        yv = jnp.where(lane_grp == gl, shifted(ds_[gl], gl - il), yv)
                    y_ref[0, lt, pl.ds(r0 * lc + h * gpt + il, rb, stride=lc), :] = yv
        return carry

    lax.fori_loop(0, nc // rb, permute_out, 0)


def _s5_prompt(u, mats):
    b, n_lt, t, _ = u.shape
    wid = n_lt * LANES
    nstate = mats[-1].shape[1]
    ts = min(t, S5_SLAB)
    nc = ts // SSM_CHUNK
    assert t % ts == 0 and nc % S5_PERMUTE_ROWS == 0
    full = lambda a: pl.BlockSpec(a.shape, lambda i, j: (0,) * a.ndim)
    return pl.pallas_call(
        _s5_prompt_kernel,
        grid=(b, t // ts),
        in_specs=[pl.BlockSpec((1, n_lt, ts, LANES), lambda i, j: (i, 0, j, 0))] + [full(a) for a in mats],
        out_specs=[pl.BlockSpec((1, n_lt, ts, LANES), lambda i, j: (i, 0, j, 0)),
                   pl.BlockSpec((1, 1, nstate), lambda i, j: (i, 0, 0)),
                   pl.BlockSpec((1, 1, nstate), lambda i, j: (i, 0, 0))],
        out_shape=[jax.ShapeDtypeStruct(u.shape, F32),
                   jax.ShapeDtypeStruct((b, 1, nstate), F32),
                   jax.ShapeDtypeStruct((b, 1, nstate), F32)],
        scratch_shapes=[pltpu.VMEM((nc, wid * SSM_CHUNK), BF16), pltpu.VMEM((nc, wid * SSM_CHUNK), F32)]
        + [pltpu.VMEM((nc, nstate), F32) for _ in range(4)]
        + [pltpu.VMEM((1, nstate), F32) for _ in range(2)],
        compiler_params=_cparams("arbitrary", "arbitrary"),
    )(u, *mats)


def _s5_step_kernel(u_ref, h0r_ref, h0i_ref, bbr_ref, bbi_ref, cr_ref, ci_ref, ar_ref, ai_ref,
                    y_ref, xr_ref, xi_ref):
    u = u_ref[...]
    ar = ar_ref[...]
    ai = ai_ref[...]
    h0r = h0r_ref[...]
    h0i = h0i_ref[...]
    xr = ar * h0r - ai * h0i + jnp.dot(u, bbr_ref[...], precision=HIGHEST, preferred_element_type=F32)
    xi = ar * h0i + ai * h0r + jnp.dot(u, bbi_ref[...], precision=HIGHEST, preferred_element_type=F32)
    xr_ref[...] = xr
    xi_ref[...] = xi
    y_ref[...] = (jnp.dot(xr, cr_ref[...], precision=HIGHEST, preferred_element_type=F32)
                  - jnp.dot(xi, ci_ref[...], precision=HIGHEST, preferred_element_type=F32))


def _s5_step(u, h0r, h0i, mats):
    n, wid = u.shape
    nstate = h0r.shape[1]
    return pl.pallas_call(
        _s5_step_kernel,
        out_shape=[jax.ShapeDtypeStruct((n, wid), F32),
                   jax.ShapeDtypeStruct((n, nstate), F32),
                   jax.ShapeDtypeStruct((n, nstate), F32)],
        compiler_params=pltpu.CompilerParams(vmem_limit_bytes=VMEM_LIMIT),
    )(u, h0r, h0i, *mats)


def _split_bf16(a):
    hi = a.astype(BF16)
    return hi, (a - hi.astype(F32)).astype(BF16)


def _mix_kernel(x_ref, mod_ref, attn_ref, ys_ref, u_ref, d_ref, wglu_ref, bglu_ref, ga_ref, gs_ref,
                wout_ref, g2_ref, wrh_ref, wrl_ref, br_ref, x1_ref, h2_ref, gate_ref, idx_ref, *, lane_tiled):
    def rows(ref):
        if lane_tiled:
            return jnp.concatenate([ref[0, i] for i in range(ref.shape[1])], axis=1).astype(F32)
        return ref[...]

    aw = ga_ref.shape[1]
    a_n = _rms(rows(attn_ref)) * ga_ref[...]

    y = rows(ys_ref) + d_ref[...] * rows(u_ref)
    y = 0.5 * y * (1.0 + jnp.tanh(np.float32(np.sqrt(2.0 / np.pi)) * (y + 0.044715 * (y * y * y))))
    z = jnp.dot(y.astype(BF16), wglu_ref[...], preferred_element_type=F32) + bglu_ref[...]
    ssm = y * jax.nn.sigmoid(z)
    s_n = _rms(ssm) * gs_ref[...]

    mixed = (jnp.dot(a_n.astype(BF16), wout_ref[0:aw, :], preferred_element_type=F32)
             + jnp.dot(s_n.astype(BF16), wout_ref[aw:, :], preferred_element_type=F32))
    x1 = x_ref[...] + mod_ref[0, 2] * mixed
    x1_ref[...] = x1
    h2 = _rms(x1) * g2_ref[...] * (1.0 + mod_ref[0, 4]) + mod_ref[0, 3]
    h2_ref[...] = h2.astype(BF16)

    h_hi, h_lo = _split_bf16(h2)
    logits = (jnp.dot(h_hi, wrh_ref[...], preferred_element_type=F32)
              + jnp.dot(h_lo, wrh_ref[...], preferred_element_type=F32)
              + jnp.dot(h_hi, wrl_ref[...], preferred_element_type=F32)) + br_ref[...]
    lane = lax.broadcasted_iota(jnp.int32, logits.shape, 1)
    lane_f = lane.astype(F32)
    cur = logits
    vals, idxs = [], []
    for _ in range(TOP_K):
        mx = jnp.max(cur, axis=1, keepdims=True)
        ix = jnp.min(jnp.where(cur == mx, lane_f, float(LANES)), axis=1, keepdims=True)
        vals.append(mx)
        idxs.append(ix)
        cur = jnp.where(lane_f == ix, -jnp.inf, cur)
    exps = [jnp.exp(v - vals[0]) for v in vals]
    den = exps[0]
    for e in exps[1:]:
        den = den + e
    gate = jnp.zeros(logits.shape, F32)
    idx = jnp.zeros(logits.shape, F32)
    for k in range(TOP_K):
        gate = jnp.where(lane == k, exps[k] / den, gate)
        idx = jnp.where(lane == k, idxs[k], idx)
    gate_ref[...] = gate
    idx_ref[...] = idx.astype(jnp.int32)


def _mix(x2d, mod, attn, ys, u, weights, tm, rows_per_mod):
    n, d = x2d.shape
    r = mod.shape[2]
    tiles_per_mod = max(rows_per_mod // tm, 1)
    mod_map = (lambda i: (i // tiles_per_mod, 0, 0, 0)) if r == 1 else (lambda i: (0, 0, i, 0))
    rowspec = lambda w: pl.BlockSpec((tm, w), lambda i: (i, 0))
    full = lambda a: pl.BlockSpec(a.shape, lambda i: (0,) * a.ndim)
    lane_tiled = attn.ndim == 4
    assert ys.ndim == attn.ndim and u.ndim == attn.ndim
    if lane_tiled:
        act_spec = lambda a: pl.BlockSpec((1, a.shape[1], tm, LANES),
                                          lambda i: (i // tiles_per_mod, 0, i % tiles_per_mod, 0))
    else:
        act_spec = lambda a: rowspec(a.shape[1])
    return pl.pallas_call(
        functools.partial(_mix_kernel, lane_tiled=lane_tiled),
        grid=(n // tm,),
        in_specs=[rowspec(d), pl.BlockSpec((1, 6, r, d), mod_map), act_spec(attn), act_spec(ys), act_spec(u)]
        + [full(w) for w in weights],
        out_specs=[rowspec(d), rowspec(d), rowspec(LANES), rowspec(LANES)],
        out_shape=[jax.ShapeDtypeStruct((n, d), F32), jax.ShapeDtypeStruct((n, d), BF16),
                   jax.ShapeDtypeStruct((n, LANES), F32), jax.ShapeDtypeStruct((n, LANES), jnp.int32)],
        compiler_params=_cparams("arbitrary"),
    )(x2d, mod, attn, ys, u, *weights)


def _moe_kernel(be_ref, first_ref, nb_ref, x_ref, wg_ref, bg_ref, wu_ref, bu_ref, wd_ref, bd_ref, o_ref,
                wg_bf, wu_bf, wd_bf):
    i = pl.program_id(0)

    @pl.when(first_ref[i] == 1)
    def _():
        wg_bf[...] = wg_ref[0].astype(BF16)
        wu_bf[...] = wu_ref[0].astype(BF16)
        wd_bf[...] = wd_ref[0].astype(BF16)

    @pl.when(i < nb_ref[0])
    def _():
        x = x_ref[...]
        g = jnp.dot(x, wg_bf[...], preferred_element_type=F32) + bg_ref[0]
        up = jnp.dot(x, wu_bf[...], preferred_element_type=F32) + bu_ref[0]
        g = jnp.minimum(g, SWIGLU_LIMIT)
        up = jnp.clip(up, -SWIGLU_LIMIT, SWIGLU_LIMIT)
        hid = (up + 1.0) * g * jax.nn.sigmoid(SWIGLU_ALPHA * g)
        y = jnp.dot(hid.astype(BF16), wd_bf[...], preferred_element_type=F32) + bd_ref[0]
        o_ref[...] = y.astype(o_ref.dtype)

    @pl.when(i >= nb_ref[0])
    def _():
        o_ref[...] = jnp.zeros(o_ref.shape, o_ref.dtype)


def _moe_experts(xb, block_e, first, n_used, wg, bg, wu, bu, wd, bd):
    n_rows, d = xb.shape
    e, _, f = wg.shape
    nb = n_rows // MOE_BLOCK
    wmap = lambda i, be, fi, nu: (be[i], 0, 0)
    grid_spec = pltpu.PrefetchScalarGridSpec(
        num_scalar_prefetch=3,
        grid=(nb,),
        in_specs=[pl.BlockSpec((MOE_BLOCK, d), lambda i, be, fi, nu: (i, 0)),
                  pl.BlockSpec((1, d, f), wmap), pl.BlockSpec((1, 1, f), wmap),
                  pl.BlockSpec((1, d, f), wmap), pl.BlockSpec((1, 1, f), wmap),
                  pl.BlockSpec((1, f, d), wmap), pl.BlockSpec((1, 1, d), wmap)],
        out_specs=pl.BlockSpec((MOE_BLOCK, d), lambda i, be, fi, nu: (i, 0)),
        scratch_shapes=[pltpu.VMEM((d, f), BF16), pltpu.VMEM((d, f), BF16), pltpu.VMEM((f, d), BF16)],
    )
    return pl.pallas_call(
        _moe_kernel,
        grid_spec=grid_spec,
        out_shape=jax.ShapeDtypeStruct((n_rows, d), BF16),
        compiler_params=_cparams("arbitrary"),
    )(block_e, first, n_used, xb, wg, bg.reshape(e, 1, f), wu, bu.reshape(e, 1, f), wd, bd.reshape(e, 1, d))


def _final_kernel(x1_ref, mod_ref, gate_ref, yg_ref, gf_ref, o_ref):
    gate = gate_ref[...]
    acc = jnp.zeros(x1_ref.shape, F32)
    for k in range(TOP_K):
        acc = acc + gate[:, k:k + 1] * yg_ref[k].astype(F32)
    x = x1_ref[...] + mod_ref[0, 5] * acc
    o_ref[...] = _rms(x) * gf_ref[...]


def _final(x1, mod, gates, yg, g_final, tm, rows_per_mod, row0):
    n, d = x1.shape
    r = mod.shape[2]
    tiles_per_mod = max(rows_per_mod // tm, 1)
    mod_map = (lambda i: (i // tiles_per_mod, 0, 0, 0)) if r == 1 else (lambda i: (0, 0, i, 0))
    assert row0 % tm == 0
    off = row0 // tm
    return pl.pallas_call(
        _final_kernel,
        grid=(n // tm,),
        in_specs=[pl.BlockSpec((tm, d), lambda i: (i, 0)),
                  pl.BlockSpec((1, 6, r, d), mod_map),
                  pl.BlockSpec((tm, LANES), lambda i: (i + off, 0)),
                  pl.BlockSpec((TOP_K, tm, d), lambda i: (0, i + off, 0)),
                  pl.BlockSpec((1, d), lambda i: (0, 0))],
        out_specs=pl.BlockSpec((tm, d), lambda i: (i, 0)),
        out_shape=jax.ShapeDtypeStruct((n, d), F32),
        compiler_params=_cparams("arbitrary"),
    )(x1, mod, gates, yg, g_final.reshape(1, d))


def kernel(x_prompt, x_sample, c_prompt, c_sample, cache_k, cache_v, state_ssm_re, state_ssm_im, w_ada, b_ada, g_norm1, g_norm2, w_in, lambda_re, lambda_im, log_dt, b_ssm_re, b_ssm_im, c_ssm_re, c_ssm_im, d_ssm, w_glu, b_glu, g_attn_out, g_ssm_out, w_out, w_router, b_router, w_gate, b_gate, w_up, b_up, w_down, b_down, g_final):
    assert w_ada.shape[0] == 1, "one layer"
    b, t, d = x_prompt.shape
    bs = x_sample.shape[0]
    assert x_sample.shape[1] == 1
    wbuf, n_heads = cache_k.shape[2], cache_k.shape[3]
    aw = n_heads * HEAD_DIM
    n_groups, n_state = lambda_re.shape[1:]
    sw = n_groups * SSM_GROUP
    n_exp = w_router.shape[2]
    keep = min(max(w for w, _ in DILATED_PATTERNS), t)
    tm = min(ROW_TILE, t)

    mod = _ada(jnp.concatenate([c_prompt, c_sample], axis=0), w_ada[0], b_ada[0])
    mod_p = mod[:b].reshape(b, 6, 1, d)
    mod_s = mod[b:].reshape(bs, 6, d).transpose(1, 0, 2)[None]

    w_in_bf = w_in[0].astype(BF16)
    g1 = g_norm1[0].reshape(1, d)
    cos_p, sin_p = _rope_tables(jnp.arange(t), n_heads)
    cos_s, sin_s = _rope_tables(jnp.full((1,), PAST_LEN), n_heads)
    q_p, k_p, v_p, kt_p, vt_p, u_p = _inproj_prompt(x_prompt, mod_p, g1, w_in_bf, cos_p, sin_p, tm)
    qt_s, kt_s, vt_s, u_s = _inproj_decode(x_sample.reshape(bs, d), mod_s, g1, w_in_bf, cos_s, sin_s)

    attn_p = _dilated_attention(q_p, k_p, v_p)
    to_hdp = lambda c: jnp.transpose(c[0], (0, 2, 3, 1))
    from_hdp = lambda c: jnp.transpose(c, (0, 3, 1, 2))[None]
    ck_new, cv_new, attn_t = _decode_attention(qt_s, kt_s, vt_s, to_hdp(cache_k), to_hdp(cache_v))
    attn_s = attn_t.T

    ssm_params = (lambda_re[0], lambda_im[0], log_dt[0], b_ssm_re[0], b_ssm_im[0], c_ssm_re[0], c_ssm_im[0])
    ys_p, hr_p, hi_p = _s5_prompt(u_p, _s5_chunk_matrices(*ssm_params))
    ys_s, hr_s, hi_s = _s5_step(u_s, state_ssm_re[0].reshape(bs, n_groups * n_state),
                                state_ssm_im[0].reshape(bs, n_groups * n_state),
                                _s5_step_matrices(*ssm_params))

    wr_pad = jnp.zeros((d, LANES), F32).at[:, :n_exp].set(w_router[0])
    wr_hi = wr_pad.astype(BF16)
    wr_lo = (wr_pad - wr_hi.astype(F32)).astype(BF16)
    br_pad = jnp.full((1, LANES), NEG_BIG, F32).at[0, :n_exp].set(b_router[0])
    mix_w = (d_ssm[0].reshape(1, sw), w_glu[0].astype(BF16), b_glu[0].reshape(1, sw),
             g_attn_out[0].reshape(1, aw), g_ssm_out[0].reshape(1, sw), w_out[0].astype(BF16),
             g_norm2[0].reshape(1, d), wr_hi, wr_lo, br_pad)
    x1_p, h2_p, gate_p, idx_p = _mix(x_prompt.reshape(b * t, d), mod_p, attn_p, ys_p, u_p,
                                     mix_w, tm, t)
    x1_s, h2_s, gate_s, idx_s = _mix(x_sample.reshape(bs, d), mod_s, attn_s, ys_s, u_s, mix_w, bs, 1)

    m_tok = b * t + bs
    h2 = jnp.concatenate([h2_p, h2_s], axis=0)
    gates = jnp.concatenate([gate_p, gate_s], axis=0)
    top_e = jnp.concatenate([idx_p, idx_s], axis=0)[:, :TOP_K]
    n_assign = m_tok * TOP_K
    flat_e = top_e.reshape(-1)
    onehot = (flat_e[:, None] == jnp.arange(n_exp)[None, :]).astype(jnp.int32)
    csum = jnp.cumsum(onehot, axis=0)
    rank = jnp.sum(csum * onehot, axis=1) - 1
    counts = csum[-1]
    padded = (counts + MOE_BLOCK - 1) // MOE_BLOCK * MOE_BLOCK
    pend = jnp.cumsum(padded)
    pstart = pend - padded
    dest = jnp.sum(pstart[None, :] * onehot, axis=1) + rank
    nb = -(-n_assign // MOE_BLOCK) + n_exp
    n_rows = nb * MOE_BLOCK
    block_start = jnp.arange(nb, dtype=jnp.int32) * MOE_BLOCK
    block_e = jnp.minimum(jnp.sum((pend[None, :] <= block_start[:, None]).astype(jnp.int32), axis=1), n_exp - 1)
    order = jnp.argsort(flat_e, stable=True).astype(jnp.int32)
    ustart = jnp.cumsum(counts) - counts
    first_src = ustart[block_e] + block_start - pstart[block_e]
    last_src = ustart[block_e] + counts[block_e] - 1
    src = jnp.minimum(first_src[:, None] + jnp.arange(MOE_BLOCK, dtype=jnp.int32)[None, :], last_src[:, None])
    row_tok = order[jnp.clip(src.reshape(-1), 0, n_assign - 1)] // TOP_K
    first = jnp.concatenate([jnp.ones((1,), jnp.int32), (block_e[1:] != block_e[:-1]).astype(jnp.int32)])
    n_used = (pend[-1] // MOE_BLOCK).astype(jnp.int32).reshape(1)
    xb = h2.at[row_tok].get(mode="promise_in_bounds")
    yb = _moe_experts(xb, block_e, first, n_used, w_gate[0], b_gate[0], w_up[0], b_up[0], w_down[0], b_down[0])
    yg = yb.at[dest.reshape(m_tok, TOP_K).T].get(mode="promise_in_bounds")

    y_prompt = _final(x1_p, mod_p, gates, yg, g_final, tm, t, 0).reshape(b, t, d)
    y_sample = _final(x1_s, mod_s, gates, yg, g_final, bs, 1, b * t).reshape(bs, 1, d)

    k_win_p = from_hdp(kt_p[:, :, t - keep:].reshape(b, n_heads, HEAD_DIM, keep))
    v_win_p = from_hdp(vt_p[:, :, t - keep:].reshape(b, n_heads, HEAD_DIM, keep))
    st = lambda a, n: a.reshape(1, n, n_groups, n_state)
    return (y_prompt, y_sample, k_win_p, v_win_p, st(hr_p, b), st(hi_p, b),
            from_hdp(ck_new), from_hdp(cv_new), st(hr_s, bs), st(hi_s, bs))
```

```python
import functools

import jax
import jax.numpy as jnp
import numpy as np
from jax import lax
from jax.experimental import pallas as pl
from jax.experimental.pallas import tpu as pltpu

F32 = jnp.float32
BF16 = jnp.bfloat16
HIGHEST = lax.Precision.HIGHEST

HEAD_DIM = 64
DILATED_PATTERNS = ((128, 1), (512, 4), (2048, 16))
ROPE_THETA = 10000.0
PAST_LEN = 8192
SSM_GROUP = 16
SSM_STATE = 64
TOP_K = 4
SWIGLU_LIMIT = 7.0
SWIGLU_ALPHA = 1.702
RMS_EPS = 1e-6

LANES = 128
HEADS_PER_LANE_TILE = LANES // HEAD_DIM
ATTN_BLOCK = 128
ATTN_UNROLL = 8
DECODE_HEADS = 4
SSM_CHUNK = 16
S5_SLAB = 2048
S5_PERMUTE_ROWS = 32
ROW_TILE = 512
MOE_BLOCK = 512
VMEM_LIMIT = 52 * 1024 * 1024
NEG_BIG = -1e30


def _cparams(*sem):
    return pltpu.CompilerParams(dimension_semantics=sem, vmem_limit_bytes=VMEM_LIMIT)


def _rms(x):
    return x * lax.rsqrt(jnp.mean(x * x, axis=-1, keepdims=True) + RMS_EPS)


def _ada_kernel(c_ref, w_ref, b_ref, o_ref):
    c = c_ref[...]
    s = c * jax.nn.sigmoid(c)
    o_ref[...] = jnp.dot(s, w_ref[...], precision=HIGHEST, preferred_element_type=F32) + b_ref[...]


def _ada(c, w, b):
    n, d = c.shape
    nout = w.shape[1]
    return pl.pallas_call(
        _ada_kernel,
        grid=(nout // d,),
        in_specs=[pl.BlockSpec((n, d), lambda j: (0, 0)),
                  pl.BlockSpec((d, d), lambda j: (0, j)),
                  pl.BlockSpec((1, d), lambda j: (0, j))],
        out_specs=pl.BlockSpec((n, d), lambda j: (0, j)),
        out_shape=jax.ShapeDtypeStruct((n, nout), F32),
        compiler_params=_cparams("arbitrary"),
    )(c, w, b.reshape(1, nout))


def _project(x, mod_ref, g_ref, w_ref, cos_ref, sin_ref, aw):
    h = _rms(x) * g_ref[...]
    h = h * (1.0 + mod_ref[0, 1]) + mod_ref[0, 0]
    proj = jnp.dot(h.astype(BF16), w_ref[...], preferred_element_type=F32)
    cos = cos_ref[...]
    sin = sin_ref[...]
    lane = lax.broadcasted_iota(jnp.int32, (1, aw), 1)
    first_half = (lane & (HEAD_DIM - 1)) < (HEAD_DIM // 2)

    def rope(t):
        rot = jnp.where(first_half, pltpu.roll(t, aw - HEAD_DIM // 2, 1), pltpu.roll(t, HEAD_DIM // 2, 1))
        return t * cos + rot * sin

    q = rope(proj[:, :aw]) * (HEAD_DIM ** -0.5)
    k = rope(proj[:, aw:2 * aw])
    return q, k, proj[:, 2 * aw:3 * aw], proj[:, 3 * aw:]


def _inproj_prompt_kernel(x_ref, mod_ref, g_ref, w_ref, cos_ref, sin_ref,
                          q_ref, k_ref, v_ref, kt_ref, vt_ref, u_ref, *, aw):
    q, k, v, u = _project(x_ref[0], mod_ref, g_ref, w_ref, cos_ref, sin_ref, aw)
    for hp in range(aw // LANES):
        cols = slice(hp * LANES, (hp + 1) * LANES)
        q_ref[0, hp] = q[:, cols]
        k_ref[0, hp] = k[:, cols]
        v_ref[0, hp] = v[:, cols]
    kt_ref[0] = k.T
    vt_ref[0] = v.T
    for lt in range(u_ref.shape[1]):
        u_ref[0, lt] = u[:, lt * LANES:(lt + 1) * LANES]


def _inproj_prompt(x, mod, g, w_bf, cos, sin, tm):
    b, t, d = x.shape
    nproj = w_bf.shape[1]
    aw = cos.shape[1]
    sw = nproj - 3 * aw
    n_lt = aw // LANES
    tok = lambda i, j: (j, i, 0)
    hp_major = lambda i, j: (j, 0, i, 0)
    transposed = lambda i, j: (j, 0, i)
    out_shapes = [jax.ShapeDtypeStruct((b, n_lt, t, LANES), F32)] * 3 + [
        jax.ShapeDtypeStruct((b, aw, t), F32), jax.ShapeDtypeStruct((b, aw, t), F32),
        jax.ShapeDtypeStruct((b, sw // LANES, t, LANES), F32)]
    return pl.pallas_call(
        functools.partial(_inproj_prompt_kernel, aw=aw),
        grid=(t // tm, b),
        in_specs=[pl.BlockSpec((1, tm, d), tok),
                  pl.BlockSpec((1, 6, 1, d), lambda i, j: (j, 0, 0, 0)),
                  pl.BlockSpec((1, d), lambda i, j: (0, 0)),
                  pl.BlockSpec((d, nproj), lambda i, j: (0, 0)),
                  pl.BlockSpec((tm, aw), lambda i, j: (i, 0)),
                  pl.BlockSpec((tm, aw), lambda i, j: (i, 0))],
        out_specs=[pl.BlockSpec((1, n_lt, tm, LANES), hp_major)] * 3 + [
            pl.BlockSpec((1, aw, tm), transposed), pl.BlockSpec((1, aw, tm), transposed),
            pl.BlockSpec((1, sw // LANES, tm, LANES), hp_major)],
        out_shape=out_shapes,
        compiler_params=_cparams("arbitrary", "arbitrary"),
    )(x, mod, g, w_bf, cos, sin)


def _inproj_decode_kernel(x_ref, mod_ref, g_ref, w_ref, cos_ref, sin_ref, qt_ref, kt_ref, vt_ref, u_ref, *, aw):
    q, k, v, u = _project(x_ref[...], mod_ref, g_ref, w_ref, cos_ref, sin_ref, aw)
    qt_ref[...] = q.T
    kt_ref[...] = k.T
    vt_ref[...] = v.T
    u_ref[...] = u


def _inproj_decode(x, mod, g, w_bf, cos, sin):
    n, d = x.shape
    nproj = w_bf.shape[1]
    aw = cos.shape[1]
    return pl.pallas_call(
        functools.partial(_inproj_decode_kernel, aw=aw),
        out_shape=[jax.ShapeDtypeStruct((aw, n), F32)] * 3 + [jax.ShapeDtypeStruct((n, nproj - 3 * aw), F32)],
        compiler_params=pltpu.CompilerParams(vmem_limit_bytes=VMEM_LIMIT),
    )(x, mod, g, w_bf, cos, sin)


def _rope_tables(pos, n_heads):
    half = HEAD_DIM // 2
    inv_freq = ROPE_THETA ** (-jnp.arange(half, dtype=F32) / half)
    ang = pos.astype(F32)[:, None] * inv_freq[None, :]
    cos = jnp.cos(ang)
    sin = jnp.sin(ang)
    cos_h = jnp.concatenate([cos, cos], axis=-1)
    sin_h = jnp.concatenate([-sin, sin], axis=-1)
    return jnp.tile(cos_h, (1, n_heads)), jnp.tile(sin_h, (1, n_heads))


def _dilated_attn_kernel(q_ref, k_ref, v_ref, o_ref, acc, mrun, lrun):
    t = q_ref.shape[2]
    blk = ATTN_BLOCK
    lane = lax.broadcasted_iota(jnp.int32, (1, LANES), 1)
    head_mask = [(lane < HEAD_DIM).astype(F32), (lane >= HEAD_DIM).astype(F32)]
    lane_o = lax.broadcasted_iota(jnp.int32, (blk, LANES), 1)
    row = lax.broadcasted_iota(jnp.int32, (blk, 2 * blk), 0)
    col = lax.broadcasted_iota(jnp.int32, (blk, 2 * blk), 1)
    rel = row - col
    order = sorted(DILATED_PATTERNS, key=lambda p: -p[1])
    for pi, (window, dil) in enumerate(order):
        n_keys = window // dil
        nblk = t // (dil * blk)

        def block(idx, carry, pi=pi, dil=dil, n_keys=n_keys, nblk=nblk):
            r = idx // nblk
            gq = (idx % nblk) * blk
            ks = jnp.maximum(gq - blk, 0)
            q_rows = pl.ds(gq * dil + r, blk, stride=dil) if dil > 1 else pl.ds(pl.multiple_of(gq, blk), blk)
            k_rows = pl.ds(ks * dil + r, 2 * blk, stride=dil) if dil > 1 else pl.ds(pl.multiple_of(ks, blk), 2 * blk)
            dist = rel + (gq - ks)
            bias = jnp.where(dist >= 0, jnp.where(dist <= n_keys, 0.0, NEG_BIG), NEG_BIG)
            q2 = q_ref[0, 0, q_rows, :]
            k2 = k_ref[0, 0, k_rows, :].astype(BF16)
            v2 = v_ref[0, 0, k_rows, :].astype(BF16)
            os, ms, ls = [], [], []
            for half in range(HEADS_PER_LANE_TILE):
                qh = (q2 * head_mask[half]).astype(BF16)
                s = lax.dot_general(qh, k2, (((1,), (1,)), ((), ())), preferred_element_type=F32) + bias
                m = jnp.max(s, axis=1, keepdims=True)
                p = jnp.exp(s - m)
                ls.append(jnp.sum(p, axis=1, keepdims=True))
                ms.append(m)
                os.append(jnp.dot(p.astype(BF16), v2, preferred_element_type=F32))
            first = lane_o < HEAD_DIM
            o_t = jnp.where(first, os[0], os[1])
            m_t = jnp.where(first, ms[0], ms[1])
            l_t = jnp.where(first, ls[0], ls[1])
            if pi == 0:
                acc[q_rows, :] = o_t
                mrun[q_rows, :] = m_t
                lrun[q_rows, :] = l_t
            else:
                m_o = mrun[q_rows, :]
                m_n = jnp.maximum(m_o, m_t)
                a_o = jnp.exp(m_o - m_n)
                a_t = jnp.exp(m_t - m_n)
                acc_n = a_o * acc[q_rows, :] + a_t * o_t
                l_n = a_o * lrun[q_rows, :] + a_t * l_t
                if pi == len(order) - 1:
                    o_ref[0, 0, q_rows, :] = (acc_n / l_n).astype(o_ref.dtype)
                else:
                    acc[q_rows, :] = acc_n
                    mrun[q_rows, :] = m_n
                    lrun[q_rows, :] = l_n
            return carry

        lax.fori_loop(0, dil * nblk, block, 0, unroll=ATTN_UNROLL)


def _dilated_attention(q, k, v):
    b, n_lt, t, _ = q.shape
    for window, dil in DILATED_PATTERNS:
        assert window // dil == ATTN_BLOCK and t % (dil * 2 * ATTN_BLOCK) == 0
    assert DILATED_PATTERNS[0][1] == 1
    spec = pl.BlockSpec((1, 1, t, LANES), lambda i, j: (i, j, 0, 0))
    return pl.pallas_call(
        _dilated_attn_kernel,
        grid=(b, n_lt),
        in_specs=[spec, spec, spec],
        out_specs=spec,
        out_shape=jax.ShapeDtypeStruct((b, n_lt, t, LANES), BF16),
        scratch_shapes=[pltpu.VMEM((t, LANES), F32) for _ in range(3)],
        compiler_params=_cparams("arbitrary", "arbitrary"),
    )(q, k, v)


def _decode_kernel(qt_ref, kt_ref, vt_ref, ck_ref, cv_ref, ok_ref, ov_ref, at_ref, *, hb):
    i = pl.program_id(0)
    hg = pl.program_id(1)
    w = ck_ref.shape[3]
    bs = qt_ref.shape[1]
    sel = lax.broadcasted_iota(jnp.int32, (HEAD_DIM, bs), 1) == i

    def column(ref, rs):
        return jnp.sum(jnp.where(sel, ref[rs, :], 0.0), axis=1, keepdims=True)

    pos = lax.broadcasted_iota(jnp.int32, (1, w), 1)
    dist = w - pos
    cnt = jnp.zeros((1, w), F32)
    for window, dil in DILATED_PATTERNS:
        cnt = cnt + jnp.where((dist & (dil - 1)) == 0, jnp.where(dist <= window, 1.0, 0.0), 0.0)
    n_pat = float(len(DILATED_PATTERNS))
    last = lax.broadcasted_iota(jnp.int32, (HEAD_DIM, w), 1) == w - 1

    @pl.when((i == 0) & (hg == 0))
    def _():
        at_ref[...] = jnp.zeros(at_ref.shape, F32)

    for h in range(hb):
        rs = pl.ds(pl.multiple_of((hg * hb + h) * HEAD_DIM, HEAD_DIM), HEAD_DIM)
        qc, kc, vc = column(qt_ref, rs), column(kt_ref, rs), column(vt_ref, rs)
        kk = ck_ref[0, h]
        vv = cv_ref[0, h]
        s = jnp.sum(kk * qc, axis=0, keepdims=True)
        s_new = jnp.sum(kc * qc, axis=0, keepdims=True)
        s = jnp.where(cnt > 0.0, s, NEG_BIG)
        m = jnp.maximum(jnp.max(s, axis=1, keepdims=True), s_new)
        p = cnt * jnp.exp(s - m)
        p_new = n_pat * jnp.exp(s_new - m)
        l = jnp.sum(p, axis=1, keepdims=True) + p_new
        o = (jnp.sum(vv * p, axis=1, keepdims=True) + p_new * vc) / l
        at_ref[rs, :] = jnp.where(sel, o, at_ref[rs, :])
        ok_ref[0, h] = jnp.where(last, kc, pltpu.roll(kk, w - 1, 1))
        ov_ref[0, h] = jnp.where(last, vc, pltpu.roll(vv, w - 1, 1))


def _decode_attention(qt, kt, vt, ck, cv):
    bs, n_heads, hd, w = ck.shape
    aw = n_heads * hd
    for window, dil in DILATED_PATTERNS:
        assert window <= w and dil & (dil - 1) == 0
    hb = DECODE_HEADS
    full = pl.BlockSpec((aw, bs), lambda i, j: (0, 0))
    buf = pl.BlockSpec((1, hb, hd, w), lambda i, j: (i, j, 0, 0))
    return pl.pallas_call(
        functools.partial(_decode_kernel, hb=hb),
        grid=(bs, n_heads // hb),
        in_specs=[full, full, full, buf, buf],
        out_specs=[buf, buf, full],
        out_shape=[jax.ShapeDtypeStruct(ck.shape, F32), jax.ShapeDtypeStruct(cv.shape, F32),
                   jax.ShapeDtypeStruct((aw, bs), F32)],
        compiler_params=_cparams("arbitrary", "arbitrary"),
    )(qt, kt, vt, ck, cv)


def _s5_discretise(lam_re, lam_im, log_dt, b_re, b_im):
    dt = jnp.exp(log_dt)[:, None]
    mag = jnp.exp(lam_re * dt)
    ar = mag * jnp.cos(lam_im * dt)
    ai = mag * jnp.sin(lam_im * dt)
    den = lam_re * lam_re + lam_im * lam_im
    fr = ((ar - 1.0) * lam_re + ai * lam_im) / den
    fi = (ai * lam_re - (ar - 1.0) * lam_im) / den
    bbr = fr[..., None] * b_re - fi[..., None] * b_im
    bbi = fr[..., None] * b_im + fi[..., None] * b_re
    return dt, ar, ai, bbr, bbi


def _s5_chunk_matrices(lam_re, lam_im, log_dt, b_re, b_im, c_re, c_im):
    lc = SSM_CHUNK
    g, p, c = b_re.shape
    dt, _, _, bbr, bbi = _s5_discretise(lam_re, lam_im, log_dt, b_re, b_im)
    kk = jnp.arange(lc + 1, dtype=F32)[:, None, None]
    mag = jnp.exp(kk * lam_re * dt)
    apr = mag * jnp.cos(kk * lam_im * dt)
    api = mag * jnp.sin(kk * lam_im * dt)
    akb_r = apr[:lc, :, :, None] * bbr - api[:lc, :, :, None] * bbi
    akb_i = apr[:lc, :, :, None] * bbi + api[:lc, :, :, None] * bbr
    kern = (jnp.einsum('gop,kgpc->kgoc', c_re, akb_r, precision=HIGHEST)
            - jnp.einsum('gop,kgpc->kgoc', c_im, akb_i, precision=HIGHEST))
    ii = jnp.arange(lc)
    lag = ii[None, :] - ii[:, None]
    toe = jnp.where((lag >= 0)[:, :, None, None, None], kern[jnp.clip(lag, 0, lc - 1)], 0.0)
    toe = toe.transpose(2, 0, 4, 1, 3).reshape(g, lc * c, lc * c)
    rev = lc - 1 - ii
    ws_r = akb_r[rev].transpose(1, 0, 3, 2).reshape(g, lc * c, p)
    ws_i = akb_i[rev].transpose(1, 0, 3, 2).reshape(g, lc * c, p)
    a1r, a1i = apr[1:], api[1:]
    ca_r = c_re[None] * a1r[:, :, None, :] - c_im[None] * a1i[:, :, None, :]
    ca_i = c_re[None] * a1i[:, :, None, :] + c_im[None] * a1r[:, :, None, :]
    wy_r = ca_r.transpose(1, 3, 0, 2).reshape(g, p, lc * c)
    wy_i = (-ca_i).transpose(1, 3, 0, 2).reshape(g, p, lc * c)

    def pair_diag(m):
        r, s = m.shape[1:]
        m2 = m.reshape(g // 2, 2, r, s)
        z = jnp.zeros((g // 2, r, s), m.dtype)
        top = jnp.concatenate([m2[:, 0], z], axis=2)
        bot = jnp.concatenate([z, m2[:, 1]], axis=2)
        return jnp.concatenate([top, bot], axis=1)

    alc_r = apr[lc].reshape(1, g * p)
    alc_i = api[lc].reshape(1, g * p)
    return (toe.astype(BF16), pair_diag(ws_r).astype(BF16), pair_diag(ws_i).astype(BF16),
            pair_diag(wy_r).astype(BF16), pair_diag(wy_i).astype(BF16), alc_r, alc_i)


def _s5_step_matrices(lam_re, lam_im, log_dt, b_re, b_im, c_re, c_im):
    g, p, c = b_re.shape
    _, ar, ai, bbr, bbi = _s5_discretise(lam_re, lam_im, log_dt, b_re, b_im)
    eye = jnp.eye(g, dtype=F32)
    bd_br = jnp.einsum('gpc,gh->gchp', bbr, eye).reshape(g * c, g * p)
    bd_bi = jnp.einsum('gpc,gh->gchp', bbi, eye).reshape(g * c, g * p)
    bd_cr = jnp.einsum('gcp,gh->gphc', c_re, eye).reshape(g * p, g * c)
    bd_ci = jnp.einsum('gcp,gh->gphc', c_im, eye).reshape(g * p, g * c)
    return bd_br, bd_bi, bd_cr, bd_ci, ar.reshape(1, g * p), ai.reshape(1, g * p)


def _s5_prompt_kernel(u_ref, toe_ref, wsr_ref, wsi_ref, wyr_ref, wyi_ref, ar_ref, ai_ref,
                      y_ref, hr_ref, hi_ref, uperm, yperm, sr, si, hr_hist, hi_hist, hcar_r, hcar_i):
    j = pl.program_id(1)
    nc = uperm.shape[0]
    lc, gc = SSM_CHUNK, SSM_GROUP
    gpt = LANES // gc
    n_lt = u_ref.shape[1]
    rb = S5_PERMUTE_ROWS
    gw = lc * gc
    lane_grp = lax.broadcasted_iota(jnp.int32, (1, LANES), 1) // gc

    def shifted(x, groups):
        sh = (groups * gc) % LANES
        return x if sh == 0 else pltpu.roll(x, sh, 1)

    def permute_in(rc, carry):
        r0 = pl.multiple_of(rc * rb, rb)
        for lt in range(n_lt):
            for h in range(lc // gpt):
                xs = [u_ref[0, lt, pl.ds(r0 * lc + h * gpt + il, rb, stride=lc), :] for il in range(gpt)]
                for gl in range(gpt):
                    d = jnp.zeros((rb, LANES), F32)
                    for il in range(gpt):
                        d = jnp.where(lane_grp == il, shifted(xs[il], il - gl), d)
                    c0 = (lt * gpt + gl) * gw + h * LANES
                    uperm[pl.ds(r0, rb), c0:c0 + LANES] = d.astype(BF16)
        return carry

    lax.fori_loop(0, nc // rb, permute_in, 0)

    n_pairs = wsr_ref.shape[0]
    pw = wsr_ref.shape[1]
    sw = wsr_ref.shape[2]
    for pr in range(n_pairs):
        up = uperm[:, pr * pw:(pr + 1) * pw]
        sr[:, pr * sw:(pr + 1) * sw] = jnp.dot(up, wsr_ref[pr], preferred_element_type=F32)
        si[:, pr * sw:(pr + 1) * sw] = jnp.dot(up, wsi_ref[pr], preferred_element_type=F32)

    ar = ar_ref[...]
    ai = ai_ref[...]
    nstate = ar.shape[1]
    rowid = lax.broadcasted_iota(jnp.int32, (8, nstate), 0)

    @pl.when(j == 0)
    def _():
        hcar_r[...] = jnp.zeros(hcar_r.shape, F32)
        hcar_i[...] = jnp.zeros(hcar_i.shape, F32)

    def eight_chunks(c8, carry):
        h_r, h_i = carry
        base = pl.multiple_of(c8 * 8, 8)
        s_r8 = sr[pl.ds(base, 8), :]
        s_i8 = si[pl.ds(base, 8), :]
        t_r = jnp.zeros((8, nstate), F32)
        t_i = jnp.zeros((8, nstate), F32)
        for r in range(8):
            t_r = jnp.where(rowid == r, h_r, t_r)
            t_i = jnp.where(rowid == r, h_i, t_i)
            n_r = ar * h_r - ai * h_i + s_r8[r:r + 1, :]
            n_i = ar * h_i + ai * h_r + s_i8[r:r + 1, :]
            h_r, h_i = n_r, n_i
        hr_hist[pl.ds(base, 8), :] = t_r
        hi_hist[pl.ds(base, 8), :] = t_i
        return h_r, h_i

    h_r, h_i = lax.fori_loop(0, nc // 8, eight_chunks, (hcar_r[...], hcar_i[...]))
    hcar_r[...] = h_r
    hcar_i[...] = h_i
    hr_ref[0] = h_r
    hi_ref[0] = h_i

    for pr in range(n_pairs):
        st = slice(pr * sw, (pr + 1) * sw)
        y2 = (jnp.dot(hr_hist[:, st].astype(BF16), wyr_ref[pr], preferred_element_type=F32)
              + jnp.dot(hi_hist[:, st].astype(BF16), wyi_ref[pr], preferred_element_type=F32))
        for gg in range(2):
            g = pr * 2 + gg
            cols = slice(g * gw, (g + 1) * gw)
            yi = jnp.dot(uperm[:, cols], toe_ref[g], preferred_element_type=F32)
            yperm[:, cols] = yi + y2[:, gg * gw:(gg + 1) * gw]

    def permute_out(rc, carry):
        r0 = pl.multiple_of(rc * rb, rb)
        for lt in range(n_lt):
            for h in range(lc // gpt):
                ds_ = [yperm[pl.ds(r0, rb), (lt * gpt + gl) * gw + h * LANES:(lt * gpt + gl) * gw + (h + 1) * LANES]
                       for gl in range(gpt)]
                for il in range(gpt):
                    yv = jnp.zeros((rb, LANES), F32)
                    for gl in range(gpt):
                        yv = jnp.where(lane_grp == gl, shifted(ds_[gl], gl - il), yv)
                    y_ref[0, lt, pl.ds(r0 * lc + h * gpt + il, rb, stride=lc), :] = yv
        return carry

    lax.fori_loop(0, nc // rb, permute_out, 0)


def _s5_prompt(u, mats):
    b, n_lt, t, _ = u.shape
    wid = n_lt * LANES
    nstate = mats[-1].shape[1]
    ts = min(t, S5_SLAB)
    nc = ts // SSM_CHUNK
    assert t % ts == 0 and nc % S5_PERMUTE_ROWS == 0
    full = lambda a: pl.BlockSpec(a.shape, lambda i, j: (0,) * a.ndim)
    return pl.pallas_call(
        _s5_prompt_kernel,
        grid=(b, t // ts),
        in_specs=[pl.BlockSpec((1, n_lt, ts, LANES), lambda i, j: (i, 0, j, 0))] + [full(a) for a in mats],
        out_specs=[pl.BlockSpec((1, n_lt, ts, LANES), lambda i, j: (i, 0, j, 0)),
                   pl.BlockSpec((1, 1, nstate), lambda i, j: (i, 0, 0)),
                   pl.BlockSpec((1, 1, nstate), lambda i, j: (i, 0, 0))],
        out_shape=[jax.ShapeDtypeStruct(u.shape, F32),
                   jax.ShapeDtypeStruct((b, 1, nstate), F32),
                   jax.ShapeDtypeStruct((b, 1, nstate), F32)],
        scratch_shapes=[pltpu.VMEM((nc, wid * SSM_CHUNK), BF16), pltpu.VMEM((nc, wid * SSM_CHUNK), F32)]
        + [pltpu.VMEM((nc, nstate), F32) for _ in range(4)]
        + [pltpu.VMEM((1, nstate), F32) for _ in range(2)],
        compiler_params=_cparams("arbitrary", "arbitrary"),
    )(u, *mats)


def _s5_step_kernel(u_ref, h0r_ref, h0i_ref, bbr_ref, bbi_ref, cr_ref, ci_ref, ar_ref, ai_ref,
                    y_ref, xr_ref, xi_ref):
    u = u_ref[...]
    ar = ar_ref[...]
    ai = ai_ref[...]
    h0r = h0r_ref[...]
    h0i = h0i_ref[...]
    xr = ar * h0r - ai * h0i + jnp.dot(u, bbr_ref[...], precision=HIGHEST, preferred_element_type=F32)
    xi = ar * h0i + ai * h0r + jnp.dot(u, bbi_ref[...], precision=HIGHEST, preferred_element_type=F32)
    xr_ref[...] = xr
    xi_ref[...] = xi
    y_ref[...] = (jnp.dot(xr, cr_ref[...], precision=HIGHEST, preferred_element_type=F32)
                  - jnp.dot(xi, ci_ref[...], precision=HIGHEST, preferred_element_type=F32))


def _s5_step(u, h0r, h0i, mats):
    n, wid = u.shape
    nstate = h0r.shape[1]
    return pl.pallas_call(
        _s5_step_kernel,
        out_shape=[jax.ShapeDtypeStruct((n, wid), F32),
                   jax.ShapeDtypeStruct((n, nstate), F32),
                   jax.ShapeDtypeStruct((n, nstate), F32)],
        compiler_params=pltpu.CompilerParams(vmem_limit_bytes=VMEM_LIMIT),
    )(u, h0r, h0i, *mats)


def _split_bf16(a):
    hi = a.astype(BF16)
    return hi, (a - hi.astype(F32)).astype(BF16)


def _mix_kernel(x_ref, mod_ref, attn_ref, ys_ref, u_ref, d_ref, wglu_ref, bglu_ref, ga_ref, gs_ref,
                wout_ref, g2_ref, wrh_ref, wrl_ref, br_ref, x1_ref, h2_ref, gate_ref, idx_ref, *, lane_tiled):
    def rows(ref):
        if lane_tiled:
            return jnp.concatenate([ref[0, i] for i in range(ref.shape[1])], axis=1).astype(F32)
        return ref[...]

    aw = ga_ref.shape[1]
    a_n = _rms(rows(attn_ref)) * ga_ref[...]

    y = rows(ys_ref) + d_ref[...] * rows(u_ref)
    y = 0.5 * y * (1.0 + jnp.tanh(np.float32(np.sqrt(2.0 / np.pi)) * (y + 0.044715 * (y * y * y))))
    z = jnp.dot(y.astype(BF16), wglu_ref[...], preferred_element_type=F32) + bglu_ref[...]
    ssm = y * jax.nn.sigmoid(z)
    s_n = _rms(ssm) * gs_ref[...]

    mixed = (jnp.dot(a_n.astype(BF16), wout_ref[0:aw, :], preferred_element_type=F32)
             + jnp.dot(s_n.astype(BF16), wout_ref[aw:, :], preferred_element_type=F32))
    x1 = x_ref[...] + mod_ref[0, 2] * mixed
    x1_ref[...] = x1
    h2 = _rms(x1) * g2_ref[...] * (1.0 + mod_ref[0, 4]) + mod_ref[0, 3]
    h2_ref[...] = h2.astype(BF16)

    h_hi, h_lo = _split_bf16(h2)
    logits = (jnp.dot(h_hi, wrh_ref[...], preferred_element_type=F32)
              + jnp.dot(h_lo, wrh_ref[...], preferred_element_type=F32)
              + jnp.dot(h_hi, wrl_ref[...], preferred_element_type=F32)) + br_ref[...]
    lane = lax.broadcasted_iota(jnp.int32, logits.shape, 1)
    lane_f = lane.astype(F32)
    cur = logits
    vals, idxs = [], []
    for _ in range(TOP_K):
        mx = jnp.max(cur, axis=1, keepdims=True)
        ix = jnp.min(jnp.where(cur == mx, lane_f, float(LANES)), axis=1, keepdims=True)
        vals.append(mx)
        idxs.append(ix)
        cur = jnp.where(lane_f == ix, -jnp.inf, cur)
    exps = [jnp.exp(v - vals[0]) for v in vals]
    den = exps[0]
    for e in exps[1:]:
        den = den + e
    gate = jnp.zeros(logits.shape, F32)
    idx = jnp.zeros(logits.shape, F32)
    for k in range(TOP_K):
        gate = jnp.where(lane == k, exps[k] / den, gate)
        idx = jnp.where(lane == k, idxs[k], idx)
    gate_ref[...] = gate
    idx_ref[...] = idx.astype(jnp.int32)


def _mix(x2d, mod, attn, ys, u, weights, tm, rows_per_mod):
    n, d = x2d.shape
    r = mod.shape[2]
    tiles_per_mod = max(rows_per_mod // tm, 1)
    mod_map = (lambda i: (i // tiles_per_mod, 0, 0, 0)) if r == 1 else (lambda i: (0, 0, i, 0))
    rowspec = lambda w: pl.BlockSpec((tm, w), lambda i: (i, 0))
    full = lambda a: pl.BlockSpec(a.shape, lambda i: (0,) * a.ndim)
    lane_tiled = attn.ndim == 4
    assert ys.ndim == attn.ndim and u.ndim == attn.ndim
    if lane_tiled:
        act_spec = lambda a: pl.BlockSpec((1, a.shape[1], tm, LANES),
                                          lambda i: (i // tiles_per_mod, 0, i % tiles_per_mod, 0))
    else:
        act_spec = lambda a: rowspec(a.shape[1])
    return pl.pallas_call(
        functools.partial(_mix_kernel, lane_tiled=lane_tiled),
        grid=(n // tm,),
        in_specs=[rowspec(d), pl.BlockSpec((1, 6, r, d), mod_map), act_spec(attn), act_spec(ys), act_spec(u)]
        + [full(w) for w in weights],
        out_specs=[rowspec(d), rowspec(d), rowspec(LANES), rowspec(LANES)],
        out_shape=[jax.ShapeDtypeStruct((n, d), F32), jax.ShapeDtypeStruct((n, d), BF16),
                   jax.ShapeDtypeStruct((n, LANES), F32), jax.ShapeDtypeStruct((n, LANES), jnp.int32)],
        compiler_params=_cparams("arbitrary"),
    )(x2d, mod, attn, ys, u, *weights)


def _moe_kernel(be_ref, first_ref, nb_ref, x_ref, wg_ref, bg_ref, wu_ref, bu_ref, wd_ref, bd_ref, o_ref,
                wg_bf, wu_bf, wd_bf):
    i = pl.program_id(0)

    @pl.when(first_ref[i] == 1)
    def _():
        wg_bf[...] = wg_ref[0].astype(BF16)
        wu_bf[...] = wu_ref[0].astype(BF16)
        wd_bf[...] = wd_ref[0].astype(BF16)

    @pl.when(i < nb_ref[0])
    def _():
        x = x_ref[...]
        g = jnp.dot(x, wg_bf[...], preferred_element_type=F32) + bg_ref[0]
        up = jnp.dot(x, wu_bf[...], preferred_element_type=F32) + bu_ref[0]
        g = jnp.minimum(g, SWIGLU_LIMIT)
        up = jnp.clip(up, -SWIGLU_LIMIT, SWIGLU_LIMIT)
        hid = (up + 1.0) * g * jax.nn.sigmoid(SWIGLU_ALPHA * g)
        y = jnp.dot(hid.astype(BF16), wd_bf[...], preferred_element_type=F32) + bd_ref[0]
        o_ref[...] = y.astype(o_ref.dtype)

    @pl.when(i >= nb_ref[0])
    def _():
        o_ref[...] = jnp.zeros(o_ref.shape, o_ref.dtype)


def _moe_experts(xb, block_e, first, n_used, wg, bg, wu, bu, wd, bd):
    n_rows, d = xb.shape
    e, _, f = wg.shape
    nb = n_rows // MOE_BLOCK
    wmap = lambda i, be, fi, nu: (be[i], 0, 0)
    grid_spec = pltpu.PrefetchScalarGridSpec(
        num_scalar_prefetch=3,
        grid=(nb,),
        in_specs=[pl.BlockSpec((MOE_BLOCK, d), lambda i, be, fi, nu: (i, 0)),
                  pl.BlockSpec((1, d, f), wmap), pl.BlockSpec((1, 1, f), wmap),
                  pl.BlockSpec((1, d, f), wmap), pl.BlockSpec((1, 1, f), wmap),
                  pl.BlockSpec((1, f, d), wmap), pl.BlockSpec((1, 1, d), wmap)],
        out_specs=pl.BlockSpec((MOE_BLOCK, d), lambda i, be, fi, nu: (i, 0)),
        scratch_shapes=[pltpu.VMEM((d, f), BF16), pltpu.VMEM((d, f), BF16), pltpu.VMEM((f, d), BF16)],
    )
    return pl.pallas_call(
        _moe_kernel,
        grid_spec=grid_spec,
        out_shape=jax.ShapeDtypeStruct((n_rows, d), BF16),
        compiler_params=_cparams("arbitrary"),
    )(block_e, first, n_used, xb, wg, bg.reshape(e, 1, f), wu, bu.reshape(e, 1, f), wd, bd.reshape(e, 1, d))


def _final_kernel(x1_ref, mod_ref, gate_ref, yg_ref, gf_ref, o_ref):
    gate = gate_ref[...]
    acc = jnp.zeros(x1_ref.shape, F32)
    for k in range(TOP_K):
        acc = acc + gate[:, k:k + 1] * yg_ref[k].astype(F32)
    x = x1_ref[...] + mod_ref[0, 5] * acc
    o_ref[...] = _rms(x) * gf_ref[...]


def _final(x1, mod, gates, yg, g_final, tm, rows_per_mod, row0):
    n, d = x1.shape
    r = mod.shape[2]
    tiles_per_mod = max(rows_per_mod // tm, 1)
    mod_map = (lambda i: (i // tiles_per_mod, 0, 0, 0)) if r == 1 else (lambda i: (0, 0, i, 0))
    assert row0 % tm == 0
    off = row0 // tm
    return pl.pallas_call(
        _final_kernel,
        grid=(n // tm,),
        in_specs=[pl.BlockSpec((tm, d), lambda i: (i, 0)),
                  pl.BlockSpec((1, 6, r, d), mod_map),
                  pl.BlockSpec((tm, LANES), lambda i: (i + off, 0)),
                  pl.BlockSpec((TOP_K, tm, d), lambda i: (0, i + off, 0)),
                  pl.BlockSpec((1, d), lambda i: (0, 0))],
        out_specs=pl.BlockSpec((tm, d), lambda i: (i, 0)),
        out_shape=jax.ShapeDtypeStruct((n, d), F32),
        compiler_params=_cparams("arbitrary"),
    )(x1, mod, gates, yg, g_final.reshape(1, d))


def _moe_decode_kernel(h2_ref, gate_ref, idx_ref, x1_ref, mod_ref, gf_ref,
                       wg_ref, bg_ref, wu_ref, bu_ref, wd_ref, bd_ref, o_ref, acc):
    e = pl.program_id(0)

    @pl.when(e == 0)
    def _():
        acc[...] = jnp.zeros(acc.shape, F32)

    gate_e = jnp.sum(jnp.where(idx_ref[...] == e, gate_ref[...], 0.0), axis=1, keepdims=True)
    x = h2_ref[...]
    g = jnp.dot(x, wg_ref[0].astype(BF16), preferred_element_type=F32) + bg_ref[0]
    up = jnp.dot(x, wu_ref[0].astype(BF16), preferred_element_type=F32) + bu_ref[0]
    g = jnp.minimum(g, SWIGLU_LIMIT)
    up = jnp.clip(up, -SWIGLU_LIMIT, SWIGLU_LIMIT)
    hid = (up + 1.0) * g * jax.nn.sigmoid(SWIGLU_ALPHA * g)
    y = jnp.dot(hid.astype(BF16), wd_ref[0].astype(BF16), preferred_element_type=F32) + bd_ref[0]
    acc[...] += gate_e * y

    @pl.when(e == pl.num_programs(0) - 1)
    def _():
        x1 = x1_ref[...] + mod_ref[0, 5] * acc[...]
        o_ref[...] = _rms(x1) * gf_ref[...]


def _moe_decode(h2, gates, idx, x1, mod, g_final, wg, bg, wu, bu, wd, bd):
    n, d = x1.shape
    e, _, f = wg.shape
    full = lambda a: pl.BlockSpec(a.shape, lambda i: (0,) * a.ndim)
    wmap = lambda i: (i, 0, 0)
    return pl.pallas_call(
        _moe_decode_kernel,
        grid=(e,),
        in_specs=[full(h2), full(gates), full(idx), full(x1), full(mod), pl.BlockSpec((1, d), lambda i: (0, 0)),
                  pl.BlockSpec((1, d, f), wmap), pl.BlockSpec((1, 1, f), wmap),
                  pl.BlockSpec((1, d, f), wmap), pl.BlockSpec((1, 1, f), wmap),
                  pl.BlockSpec((1, f, d), wmap), pl.BlockSpec((1, 1, d), wmap)],
        out_specs=pl.BlockSpec((n, d), lambda i: (0, 0)),
        out_shape=jax.ShapeDtypeStruct((n, d), F32),
        scratch_shapes=[pltpu.VMEM((n, d), F32)],
        compiler_params=_cparams("arbitrary"),
    )(h2, gates, idx, x1, mod, g_final.reshape(1, d), wg, bg.reshape(e, 1, f), wu, bu.reshape(e, 1, f),
      wd, bd.reshape(e, 1, d))


def kernel(x_prompt, x_sample, c_prompt, c_sample, cache_k, cache_v, state_ssm_re, state_ssm_im, w_ada, b_ada, g_norm1, g_norm2, w_in, lambda_re, lambda_im, log_dt, b_ssm_re, b_ssm_im, c_ssm_re, c_ssm_im, d_ssm, w_glu, b_glu, g_attn_out, g_ssm_out, w_out, w_router, b_router, w_gate, b_gate, w_up, b_up, w_down, b_down, g_final):
    assert w_ada.shape[0] == 1, "one layer"
    b, t, d = x_prompt.shape
    bs = x_sample.shape[0]
    assert x_sample.shape[1] == 1
    wbuf, n_heads = cache_k.shape[2], cache_k.shape[3]
    aw = n_heads * HEAD_DIM
    n_groups, n_state = lambda_re.shape[1:]
    sw = n_groups * SSM_GROUP
    n_exp = w_router.shape[2]
    keep = min(max(w for w, _ in DILATED_PATTERNS), t)
    tm = min(ROW_TILE, t)

    mod = _ada(jnp.concatenate([c_prompt, c_sample], axis=0), w_ada[0], b_ada[0])
    mod_p = mod[:b].reshape(b, 6, 1, d)
    mod_s = mod[b:].reshape(bs, 6, d).transpose(1, 0, 2)[None]

    w_in_bf = w_in[0].astype(BF16)
    g1 = g_norm1[0].reshape(1, d)
    ssm_params = (lambda_re[0], lambda_im[0], log_dt[0], b_ssm_re[0], b_ssm_im[0], c_ssm_re[0], c_ssm_im[0])
    wr_pad = jnp.zeros((d, LANES), F32).at[:, :n_exp].set(w_router[0])
    wr_hi = wr_pad.astype(BF16)
    wr_lo = (wr_pad - wr_hi.astype(F32)).astype(BF16)
    br_pad = jnp.full((1, LANES), NEG_BIG, F32).at[0, :n_exp].set(b_router[0])
    mix_w = (d_ssm[0].reshape(1, sw), w_glu[0].astype(BF16), b_glu[0].reshape(1, sw),
             g_attn_out[0].reshape(1, aw), g_ssm_out[0].reshape(1, sw), w_out[0].astype(BF16),
             g_norm2[0].reshape(1, d), wr_hi, wr_lo, br_pad)
    experts = (w_gate[0], b_gate[0], w_up[0], b_up[0], w_down[0], b_down[0])

    cos_p, sin_p = _rope_tables(jnp.arange(t), n_heads)
    q_p, k_p, v_p, kt_p, vt_p, u_p = _inproj_prompt(x_prompt, mod_p, g1, w_in_bf, cos_p, sin_p, tm)
    attn_p = _dilated_attention(q_p, k_p, v_p)
    ys_p, hr_p, hi_p = _s5_prompt(u_p, _s5_chunk_matrices(*ssm_params))
    x1_p, h2_p, gate_p, idx_p = _mix(x_prompt.reshape(b * t, d), mod_p, attn_p, ys_p, u_p, mix_w, tm, t)

    m_tok = b * t
    n_assign = m_tok * TOP_K
    flat_e = idx_p[:, :TOP_K].reshape(-1)
    onehot = (flat_e[:, None] == jnp.arange(n_exp)[None, :]).astype(jnp.int32)
    csum = jnp.cumsum(onehot, axis=0)
    rank = jnp.sum(csum * onehot, axis=1) - 1
    counts = csum[-1]
    padded = (counts + MOE_BLOCK - 1) // MOE_BLOCK * MOE_BLOCK
    pend = jnp.cumsum(padded)
    pstart = pend - padded
    dest = jnp.sum(pstart[None, :] * onehot, axis=1) + rank
    nb = -(-n_assign // MOE_BLOCK) + n_exp
    block_start = jnp.arange(nb, dtype=jnp.int32) * MOE_BLOCK
    block_e = jnp.minimum(jnp.sum((pend[None, :] <= block_start[:, None]).astype(jnp.int32), axis=1), n_exp - 1)
    order = jnp.argsort(flat_e, stable=True).astype(jnp.int32)
    ustart = jnp.cumsum(counts) - counts
    first_src = ustart[block_e] + block_start - pstart[block_e]
    last_src = ustart[block_e] + counts[block_e] - 1
    src = jnp.minimum(first_src[:, None] + jnp.arange(MOE_BLOCK, dtype=jnp.int32)[None, :], last_src[:, None])
    row_tok = order[jnp.clip(src.reshape(-1), 0, n_assign - 1)] // TOP_K
    first = jnp.concatenate([jnp.ones((1,), jnp.int32), (block_e[1:] != block_e[:-1]).astype(jnp.int32)])
    n_used = (pend[-1] // MOE_BLOCK).astype(jnp.int32).reshape(1)
    xb = h2_p.at[row_tok].get(mode="promise_in_bounds")

    cos_s, sin_s = _rope_tables(jnp.full((1,), PAST_LEN), n_heads)
    qt_s, kt_s, vt_s, u_s = _inproj_decode(x_sample.reshape(bs, d), mod_s, g1, w_in_bf, cos_s, sin_s)
    to_hdp = lambda c: jnp.transpose(c[0], (0, 2, 3, 1))
    from_hdp = lambda c: jnp.transpose(c, (0, 3, 1, 2))[None]
    ck_new, cv_new, attn_t = _decode_attention(qt_s, kt_s, vt_s, to_hdp(cache_k), to_hdp(cache_v))
    ys_s, hr_s, hi_s = _s5_step(u_s, state_ssm_re[0].reshape(bs, n_groups * n_state),
                                state_ssm_im[0].reshape(bs, n_groups * n_state),
                                _s5_step_matrices(*ssm_params))
    x1_s, h2_s, gate_s, idx_s = _mix(x_sample.reshape(bs, d), mod_s, attn_t.T, ys_s, u_s, mix_w, bs, 1)

    yb = _moe_experts(xb, block_e, first, n_used, *experts)
    yg = yb.at[dest.reshape(m_tok, TOP_K).T].get(mode="promise_in_bounds")
    y_sample = _moe_decode(h2_s, gate_s, idx_s, x1_s, mod_s, g_final, *experts).reshape(bs, 1, d)
    y_prompt = _final(x1_p, mod_p, gate_p, yg, g_final, tm, t, 0).reshape(b, t, d)

    k_win_p = from_hdp(kt_p[:, :, t - keep:].reshape(b, n_heads, HEAD_DIM, keep))
    v_win_p = from_hdp(vt_p[:, :, t - keep:].reshape(b, n_heads, HEAD_DIM, keep))
    st = lambda a, n: a.reshape(1, n, n_groups, n_state)
    return (y_prompt, y_sample, k_win_p, v_win_p, st(hr_p, b), st(hi_p, b),
            from_hdp(ck_new), from_hdp(cv_new), st(hr_s, bs), st(hi_s, bs))
```

```python
import functools

import jax
import jax.numpy as jnp
import numpy as np
from jax import lax
from jax.experimental import pallas as pl
from jax.experimental.pallas import tpu as pltpu

F32 = jnp.float32
BF16 = jnp.bfloat16
HIGHEST = lax.Precision.HIGHEST

HEAD_DIM = 64
DILATED_PATTERNS = ((128, 1), (512, 4), (2048, 16))
ROPE_THETA = 10000.0
PAST_LEN = 8192
SSM_GROUP = 16
SSM_STATE = 64
TOP_K = 4
SWIGLU_LIMIT = 7.0
SWIGLU_ALPHA = 1.702
RMS_EPS = 1e-6

LANES = 128
HEADS_PER_LANE_TILE = LANES // HEAD_DIM
ATTN_BLOCK = 128
ATTN_UNROLL = 8
DECODE_HEADS = 4
SSM_CHUNK = 16
S5_SLAB = 2048
ROW_TILE = 512
MOE_BLOCK = 512
MOE_SPLITS = 2
VMEM_LIMIT = 52 * 1024 * 1024
NEG_BIG = -1e30


def _cparams(*sem):
    return pltpu.CompilerParams(dimension_semantics=sem, vmem_limit_bytes=VMEM_LIMIT)


def _rms(x):
    return x * lax.rsqrt(jnp.mean(x * x, axis=-1, keepdims=True) + RMS_EPS)


def _ada_kernel(c_ref, w_ref, b_ref, o_ref):
    c = c_ref[...]
    s = c * jax.nn.sigmoid(c)
    o_ref[...] = jnp.dot(s, w_ref[...], precision=HIGHEST, preferred_element_type=F32) + b_ref[...]


def _ada(c, w, b):
    n, d = c.shape
    nout = w.shape[1]
    return pl.pallas_call(
        _ada_kernel,
        grid=(nout // d,),
        in_specs=[pl.BlockSpec((n, d), lambda j: (0, 0)),
                  pl.BlockSpec((d, d), lambda j: (0, j)),
                  pl.BlockSpec((1, d), lambda j: (0, j))],
        out_specs=pl.BlockSpec((n, d), lambda j: (0, j)),
        out_shape=jax.ShapeDtypeStruct((n, nout), F32),
        compiler_params=_cparams("arbitrary"),
    )(c, w, b.reshape(1, nout))


def _project(x, mod_ref, g_ref, w_ref, cos_ref, sin_ref, aw):
    h = _rms(x) * g_ref[...]
    h = h * (1.0 + mod_ref[0, 1]) + mod_ref[0, 0]
    proj = jnp.dot(h.astype(BF16), w_ref[...], preferred_element_type=F32)
    cos = cos_ref[...]
    sin = sin_ref[...]
    lane = lax.broadcasted_iota(jnp.int32, (1, aw), 1)
    first_half = (lane & (HEAD_DIM - 1)) < (HEAD_DIM // 2)

    def rope(t):
        rot = jnp.where(first_half, pltpu.roll(t, aw - HEAD_DIM // 2, 1), pltpu.roll(t, HEAD_DIM // 2, 1))
        return t * cos + rot * sin

    q = rope(proj[:, :aw]) * (HEAD_DIM ** -0.5)
    k = rope(proj[:, aw:2 * aw])
    return q, k, proj[:, 2 * aw:3 * aw], proj[:, 3 * aw:]


def _inproj_prompt_kernel(x_ref, mod_ref, g_ref, w_ref, cos_ref, sin_ref, regroup_ref,
                          q_ref, k_ref, v_ref, kt_ref, vt_ref, u_ref, *, aw):
    q, k, v, u = _project(x_ref[0], mod_ref, g_ref, w_ref, cos_ref, sin_ref, aw)
    for hp in range(aw // LANES):
        cols = slice(hp * LANES, (hp + 1) * LANES)
        q_ref[0, hp] = q[:, cols]
        k_ref[0, hp] = k[:, cols]
        v_ref[0, hp] = v[:, cols]
    kt_ref[0] = k.T
    vt_ref[0] = v.T
    u = jnp.dot(regroup_ref[...], u.astype(BF16), preferred_element_type=F32).astype(BF16)
    for lt in range(u_ref.shape[1]):
        u_ref[0, lt] = u[:, lt * LANES:(lt + 1) * LANES]


def _regroup_matrix(tm):
    lc = SSM_CHUNK
    dst = jnp.arange(tm)
    src = (dst % (tm // lc)) * lc + dst // (tm // lc)
    return (src[:, None] == jnp.arange(tm)[None, :]).astype(BF16)


def _inproj_prompt(x, mod, g, w_bf, cos, sin, tm):
    b, t, d = x.shape
    nproj = w_bf.shape[1]
    aw = cos.shape[1]
    sw = nproj - 3 * aw
    n_lt = aw // LANES
    tok = lambda i, j: (j, i, 0)
    hp_major = lambda i, j: (j, 0, i, 0)
    transposed = lambda i, j: (j, 0, i)
    out_shapes = [jax.ShapeDtypeStruct((b, n_lt, t, LANES), F32)] * 3 + [
        jax.ShapeDtypeStruct((b, aw, t), F32), jax.ShapeDtypeStruct((b, aw, t), F32),
        jax.ShapeDtypeStruct((b, sw // LANES, t, LANES), BF16)]
    return pl.pallas_call(
        functools.partial(_inproj_prompt_kernel, aw=aw),
        grid=(t // tm, b),
        in_specs=[pl.BlockSpec((1, tm, d), tok),
                  pl.BlockSpec((1, 6, 1, d), lambda i, j: (j, 0, 0, 0)),
                  pl.BlockSpec((1, d), lambda i, j: (0, 0)),
                  pl.BlockSpec((d, nproj), lambda i, j: (0, 0)),
                  pl.BlockSpec((tm, aw), lambda i, j: (i, 0)),
                  pl.BlockSpec((tm, aw), lambda i, j: (i, 0)),
                  pl.BlockSpec((tm, tm), lambda i, j: (0, 0))],
        out_specs=[pl.BlockSpec((1, n_lt, tm, LANES), hp_major)] * 3 + [
            pl.BlockSpec((1, aw, tm), transposed), pl.BlockSpec((1, aw, tm), transposed),
            pl.BlockSpec((1, sw // LANES, tm, LANES), hp_major)],
        out_shape=out_shapes,
        compiler_params=_cparams("arbitrary", "arbitrary"),
    )(x, mod, g, w_bf, cos, sin, _regroup_matrix(tm))


def _inproj_decode_kernel(x_ref, mod_ref, g_ref, w_ref, cos_ref, sin_ref, qt_ref, kt_ref, vt_ref, u_ref, *, aw):
    q, k, v, u = _project(x_ref[...], mod_ref, g_ref, w_ref, cos_ref, sin_ref, aw)
    qt_ref[...] = q.T
    kt_ref[...] = k.T
    vt_ref[...] = v.T
    u_ref[...] = u


def _inproj_decode(x, mod, g, w_bf, cos, sin):
    n, d = x.shape
    nproj = w_bf.shape[1]
    aw = cos.shape[1]
    return pl.pallas_call(
        functools.partial(_inproj_decode_kernel, aw=aw),
        out_shape=[jax.ShapeDtypeStruct((aw, n), F32)] * 3 + [jax.ShapeDtypeStruct((n, nproj - 3 * aw), F32)],
        compiler_params=pltpu.CompilerParams(vmem_limit_bytes=VMEM_LIMIT),
    )(x, mod, g, w_bf, cos, sin)


def _rope_tables(pos, n_heads):
    half = HEAD_DIM // 2
    inv_freq = ROPE_THETA ** (-jnp.arange(half, dtype=F32) / half)
    ang = pos.astype(F32)[:, None] * inv_freq[None, :]
    cos = jnp.cos(ang)
    sin = jnp.sin(ang)
    cos_h = jnp.concatenate([cos, cos], axis=-1)
    sin_h = jnp.concatenate([-sin, sin], axis=-1)
    return jnp.tile(cos_h, (1, n_heads)), jnp.tile(sin_h, (1, n_heads))


def _dilated_attn_kernel(q_ref, k_ref, v_ref, o_ref, acc, mrun, lrun):
    t = q_ref.shape[2]
    blk = ATTN_BLOCK
    lane = lax.broadcasted_iota(jnp.int32, (1, LANES), 1)
    head_mask = [(lane < HEAD_DIM).astype(F32), (lane >= HEAD_DIM).astype(F32)]
    lane_o = lax.broadcasted_iota(jnp.int32, (blk, LANES), 1)
    row = lax.broadcasted_iota(jnp.int32, (blk, 2 * blk), 0)
    col = lax.broadcasted_iota(jnp.int32, (blk, 2 * blk), 1)
    rel = row - col
    order = sorted(DILATED_PATTERNS, key=lambda p: -p[1])
    for pi, (window, dil) in enumerate(order):
        n_keys = window // dil
        nblk = t // (dil * blk)

        def block(idx, carry, pi=pi, dil=dil, n_keys=n_keys, nblk=nblk):
            r = idx // nblk
            gq = (idx % nblk) * blk
            ks = jnp.maximum(gq - blk, 0)
            q_rows = pl.ds(gq * dil + r, blk, stride=dil) if dil > 1 else pl.ds(pl.multiple_of(gq, blk), blk)
            k_rows = pl.ds(ks * dil + r, 2 * blk, stride=dil) if dil > 1 else pl.ds(pl.multiple_of(ks, blk), 2 * blk)
            dist = rel + (gq - ks)
            bias = jnp.where(dist >= 0, jnp.where(dist <= n_keys, 0.0, NEG_BIG), NEG_BIG)
            q2 = q_ref[0, 0, q_rows, :]
            k2 = k_ref[0, 0, k_rows, :].astype(BF16)
            v2 = v_ref[0, 0, k_rows, :].astype(BF16)
            os, ms, ls = [], [], []
            for half in range(HEADS_PER_LANE_TILE):
                qh = (q2 * head_mask[half]).astype(BF16)
                s = lax.dot_general(qh, k2, (((1,), (1,)), ((), ())), preferred_element_type=F32) + bias
                m = jnp.max(s, axis=1, keepdims=True)
                p = jnp.exp(s - m)
                ls.append(jnp.sum(p, axis=1, keepdims=True))
                ms.append(m)
                os.append(jnp.dot(p.astype(BF16), v2, preferred_element_type=F32))
            first = lane_o < HEAD_DIM
            o_t = jnp.where(first, os[0], os[1])
            m_t = jnp.where(first, ms[0], ms[1])
            l_t = jnp.where(first, ls[0], ls[1])
            if pi == 0:
                acc[q_rows, :] = o_t
                mrun[q_rows, :] = m_t
                lrun[q_rows, :] = l_t
            else:
                m_o = mrun[q_rows, :]
                m_n = jnp.maximum(m_o, m_t)
                a_o = jnp.exp(m_o - m_n)
                a_t = jnp.exp(m_t - m_n)
                acc_n = a_o * acc[q_rows, :] + a_t * o_t
                l_n = a_o * lrun[q_rows, :] + a_t * l_t
                if pi == len(order) - 1:
                    o_ref[0, 0, q_rows, :] = (acc_n / l_n).astype(o_ref.dtype)
                else:
                    acc[q_rows, :] = acc_n
                    mrun[q_rows, :] = m_n
                    lrun[q_rows, :] = l_n
            return carry

        lax.fori_loop(0, dil * nblk, block, 0, unroll=ATTN_UNROLL)


def _dilated_attention(q, k, v):
    b, n_lt, t, _ = q.shape
    for window, dil in DILATED_PATTERNS:
        assert window // dil == ATTN_BLOCK and t % (dil * 2 * ATTN_BLOCK) == 0
    assert DILATED_PATTERNS[0][1] == 1
    spec = pl.BlockSpec((1, 1, t, LANES), lambda i, j: (i, j, 0, 0))
    return pl.pallas_call(
        _dilated_attn_kernel,
        grid=(b, n_lt),
        in_specs=[spec, spec, spec],
        out_specs=spec,
        out_shape=jax.ShapeDtypeStruct((b, n_lt, t, LANES), BF16),
        scratch_shapes=[pltpu.VMEM((t, LANES), F32) for _ in range(3)],
        compiler_params=_cparams("arbitrary", "arbitrary"),
    )(q, k, v)


def _decode_kernel(qt_ref, kt_ref, vt_ref, ck_ref, cv_ref, ok_ref, ov_ref, at_ref, *, hb):
    i = pl.program_id(0)
    hg = pl.program_id(1)
    w = ck_ref.shape[3]
    bs = qt_ref.shape[1]
    sel = lax.broadcasted_iota(jnp.int32, (HEAD_DIM, bs), 1) == i

    def column(ref, rs):
        return jnp.sum(jnp.where(sel, ref[rs, :], 0.0), axis=1, keepdims=True)

    pos = lax.broadcasted_iota(jnp.int32, (1, w), 1)
    dist = w - pos
    cnt = jnp.zeros((1, w), F32)
    for window, dil in DILATED_PATTERNS:
        cnt = cnt + jnp.where((dist & (dil - 1)) == 0, jnp.where(dist <= window, 1.0, 0.0), 0.0)
    n_pat = float(len(DILATED_PATTERNS))
    last = lax.broadcasted_iota(jnp.int32, (HEAD_DIM, w), 1) == w - 1

    @pl.when((i == 0) & (hg == 0))
    def _():
        at_ref[...] = jnp.zeros(at_ref.shape, F32)

    for h in range(hb):
        rs = pl.ds(pl.multiple_of((hg * hb + h) * HEAD_DIM, HEAD_DIM), HEAD_DIM)
        qc, kc, vc = column(qt_ref, rs), column(kt_ref, rs), column(vt_ref, rs)
        kk = ck_ref[0, h]
        vv = cv_ref[0, h]
        s = jnp.sum(kk * qc, axis=0, keepdims=True)
        s_new = jnp.sum(kc * qc, axis=0, keepdims=True)
        s = jnp.where(cnt > 0.0, s, NEG_BIG)
        m = jnp.maximum(jnp.max(s, axis=1, keepdims=True), s_new)
        p = cnt * jnp.exp(s - m)
        p_new = n_pat * jnp.exp(s_new - m)
        l = jnp.sum(p, axis=1, keepdims=True) + p_new
        o = (jnp.sum(vv * p, axis=1, keepdims=True) + p_new * vc) / l
        at_ref[rs, :] = jnp.where(sel, o, at_ref[rs, :])
        ok_ref[0, h] = jnp.where(last, kc, pltpu.roll(kk, w - 1, 1))
        ov_ref[0, h] = jnp.where(last, vc, pltpu.roll(vv, w - 1, 1))


def _decode_attention(qt, kt, vt, ck, cv):
    bs, n_heads, hd, w = ck.shape
    aw = n_heads * hd
    for window, dil in DILATED_PATTERNS:
        assert window <= w and dil & (dil - 1) == 0
    hb = DECODE_HEADS
    full = pl.BlockSpec((aw, bs), lambda i, j: (0, 0))
    buf = pl.BlockSpec((1, hb, hd, w), lambda i, j: (i, j, 0, 0))
    return pl.pallas_call(
        functools.partial(_decode_kernel, hb=hb),
        grid=(bs, n_heads // hb),
        in_specs=[full, full, full, buf, buf],
        out_specs=[buf, buf, full],
        out_shape=[jax.ShapeDtypeStruct(ck.shape, F32), jax.ShapeDtypeStruct(cv.shape, F32),
                   jax.ShapeDtypeStruct((aw, bs), F32)],
        compiler_params=_cparams("arbitrary", "arbitrary"),
    )(qt, kt, vt, ck, cv)


def _s5_discretise(lam_re, lam_im, log_dt, b_re, b_im):
    dt = jnp.exp(log_dt)[:, None]
    mag = jnp.exp(lam_re * dt)
    ar = mag * jnp.cos(lam_im * dt)
    ai = mag * jnp.sin(lam_im * dt)
    den = lam_re * lam_re + lam_im * lam_im
    fr = ((ar - 1.0) * lam_re + ai * lam_im) / den
    fi = (ai * lam_re - (ar - 1.0) * lam_im) / den
    bbr = fr[..., None] * b_re - fi[..., None] * b_im
    bbi = fr[..., None] * b_im + fi[..., None] * b_re
    return dt, ar, ai, bbr, bbi


def _s5_chunk_matrices(lam_re, lam_im, log_dt, b_re, b_im, c_re, c_im):
    lc = SSM_CHUNK
    g, p, c = b_re.shape
    dt, _, _, bbr, bbi = _s5_discretise(lam_re, lam_im, log_dt, b_re, b_im)
    kk = jnp.arange(lc + 1, dtype=F32)[:, None, None]
    mag = jnp.exp(kk * lam_re * dt)
    apr = mag * jnp.cos(kk * lam_im * dt)
    api = mag * jnp.sin(kk * lam_im * dt)
    akb_r = apr[:lc, :, :, None] * bbr - api[:lc, :, :, None] * bbi
    akb_i = apr[:lc, :, :, None] * bbi + api[:lc, :, :, None] * bbr
    kern = (jnp.einsum('gop,kgpc->kgoc', c_re, akb_r, precision=HIGHEST)
            - jnp.einsum('gop,kgpc->kgoc', c_im, akb_i, precision=HIGHEST))
    ii = jnp.arange(lc)
    lag = ii[None, :] - ii[:, None]
    toe = jnp.where((lag >= 0)[:, :, None, None, None], kern[jnp.clip(lag, 0, lc - 1)], 0.0)
    toe = toe.transpose(2, 0, 4, 1, 3).reshape(g, lc * c, lc * c)
    rev = lc - 1 - ii
    ws_r = akb_r[rev].transpose(1, 0, 3, 2).reshape(g, lc * c, p)
    ws_i = akb_i[rev].transpose(1, 0, 3, 2).reshape(g, lc * c, p)
    a1r, a1i = apr[1:], api[1:]
    ca_r = c_re[None] * a1r[:, :, None, :] - c_im[None] * a1i[:, :, None, :]
    ca_i = c_re[None] * a1i[:, :, None, :] + c_im[None] * a1r[:, :, None, :]
    wy_r = ca_r.transpose(1, 3, 0, 2).reshape(g, p, lc * c)
    wy_i = (-ca_i).transpose(1, 3, 0, 2).reshape(g, p, lc * c)

    def pair_diag(m):
        r, s = m.shape[1:]
        m2 = m.reshape(g // 2, 2, r, s)
        z = jnp.zeros((g // 2, r, s), m.dtype)
        top = jnp.concatenate([m2[:, 0], z], axis=2)
        bot = jnp.concatenate([z, m2[:, 1]], axis=2)
        return jnp.concatenate([top, bot], axis=1)

    alc_r = apr[lc].reshape(1, g * p)
    alc_i = api[lc].reshape(1, g * p)
    return (toe.astype(BF16), pair_diag(ws_r).astype(BF16), pair_diag(ws_i).astype(BF16),
            pair_diag(wy_r).astype(BF16), pair_diag(wy_i).astype(BF16), alc_r, alc_i)


def _s5_step_matrices(lam_re, lam_im, log_dt, b_re, b_im, c_re, c_im):
    g, p, c = b_re.shape
    _, ar, ai, bbr, bbi = _s5_discretise(lam_re, lam_im, log_dt, b_re, b_im)
    eye = jnp.eye(g, dtype=F32)
    bd_br = jnp.einsum('gpc,gh->gchp', bbr, eye).reshape(g * c, g * p)
    bd_bi = jnp.einsum('gpc,gh->gchp', bbi, eye).reshape(g * c, g * p)
    bd_cr = jnp.einsum('gcp,gh->gphc', c_re, eye).reshape(g * p, g * c)
    bd_ci = jnp.einsum('gcp,gh->gphc', c_im, eye).reshape(g * p, g * c)
    return bd_br, bd_bi, bd_cr, bd_ci, ar.reshape(1, g * p), ai.reshape(1, g * p)


def _s5_prompt_kernel(u_ref, toe_ref, wsr_ref, wsi_ref, wyr_ref, wyi_ref, ar_ref, ai_ref,
                      y_ref, hr_ref, hi_ref, uperm, yperm, sr, si, hr_hist, hi_hist, hcar_r, hcar_i):
    j = pl.program_id(1)
    nc = uperm.shape[0]
    lc, gc = SSM_CHUNK, SSM_GROUP
    gpt = LANES // gc
    n_lt = u_ref.shape[1]
    rb = ROW_TILE // lc
    gw = lc * gc
    lane_grp = lax.broadcasted_iota(jnp.int32, (1, LANES), 1) // gc

    def group_transpose(vs):
        d = 1
        while d < gpt:
            hi = (lane_grp & d) != 0
            nxt = list(vs)
            for a in range(gpt):
                if a & d == 0:
                    nxt[a] = jnp.where(hi, pltpu.roll(vs[a + d], d * gc, 1), vs[a])
                    nxt[a + d] = jnp.where(hi, vs[a + d], pltpu.roll(vs[a], LANES - d * gc, 1))
            vs = nxt
            d *= 2
        return vs

    def permute_in(rc, carry):
        r0 = pl.multiple_of(rc * rb, rb)
        for lt in range(n_lt):
            for h in range(lc // gpt):
                xs = [u_ref[0, lt, pl.ds(pl.multiple_of(r0 * lc + (h * gpt + il) * rb, rb), rb), :].astype(F32)
                      for il in range(gpt)]
                for gl, d in enumerate(group_transpose(xs)):
                    c0 = (lt * gpt + gl) * gw + h * LANES
                    uperm[pl.ds(r0, rb), c0:c0 + LANES] = d.astype(BF16)
        return carry

    lax.fori_loop(0, nc // rb, permute_in, 0)

    n_pairs = wsr_ref.shape[0]
    pw = wsr_ref.shape[1]
    sw = wsr_ref.shape[2]
    for pr in range(n_pairs):
        up = uperm[:, pr * pw:(pr + 1) * pw]
        sr[:, pr * sw:(pr + 1) * sw] = jnp.dot(up, wsr_ref[pr], preferred_element_type=F32)
        si[:, pr * sw:(pr + 1) * sw] = jnp.dot(up, wsi_ref[pr], preferred_element_type=F32)

    ar = ar_ref[...]
    ai = ai_ref[...]
    nstate = ar.shape[1]
    rowid = lax.broadcasted_iota(jnp.int32, (8, nstate), 0)

    @pl.when(j == 0)
    def _():
        hcar_r[...] = jnp.zeros(hcar_r.shape, F32)
        hcar_i[...] = jnp.zeros(hcar_i.shape, F32)

    def eight_chunks(c8, carry):
        h_r, h_i = carry
        base = pl.multiple_of(c8 * 8, 8)
        s_r8 = sr[pl.ds(base, 8), :]
        s_i8 = si[pl.ds(base, 8), :]
        t_r = jnp.zeros((8, nstate), F32)
        t_i = jnp.zeros((8, nstate), F32)
        for r in range(8):
            t_r = jnp.where(rowid == r, h_r, t_r)
            t_i = jnp.where(rowid == r, h_i, t_i)
            n_r = ar * h_r - ai * h_i + s_r8[r:r + 1, :]
            n_i = ar * h_i + ai * h_r + s_i8[r:r + 1, :]
            h_r, h_i = n_r, n_i
        hr_hist[pl.ds(base, 8), :] = t_r
        hi_hist[pl.ds(base, 8), :] = t_i
        return h_r, h_i

    h_r, h_i = lax.fori_loop(0, nc // 8, eight_chunks, (hcar_r[...], hcar_i[...]))
    hcar_r[...] = h_r
    hcar_i[...] = h_i
    hr_ref[0] = h_r
    hi_ref[0] = h_i

    for pr in range(n_pairs):
        st = slice(pr * sw, (pr + 1) * sw)
        y2 = (jnp.dot(hr_hist[:, st].astype(BF16), wyr_ref[pr], preferred_element_type=F32)
              + jnp.dot(hi_hist[:, st].astype(BF16), wyi_ref[pr], preferred_element_type=F32))
        for gg in range(2):
            g = pr * 2 + gg
            cols = slice(g * gw, (g + 1) * gw)
            yi = jnp.dot(uperm[:, cols], toe_ref[g], preferred_element_type=F32)
            yperm[:, cols] = yi + y2[:, gg * gw:(gg + 1) * gw]

    def permute_out(rc, carry):
        r0 = pl.multiple_of(rc * rb, rb)
        for lt in range(n_lt):
            for h in range(lc // gpt):
                ds_ = [yperm[pl.ds(r0, rb), (lt * gpt + gl) * gw + h * LANES:(lt * gpt + gl) * gw + (h + 1) * LANES]
                       for gl in range(gpt)]
                for il, yv in enumerate(group_transpose(ds_)):
                    y_ref[0, lt, pl.ds(pl.multiple_of(r0 * lc + (h * gpt + il) * rb, rb), rb), :] = yv.astype(y_ref.dtype)
        return carry

    lax.fori_loop(0, nc // rb, permute_out, 0)


def _s5_prompt(u, mats):
    b, n_lt, t, _ = u.shape
    wid = n_lt * LANES
    nstate = mats[-1].shape[1]
    ts = min(t, S5_SLAB)
    nc = ts // SSM_CHUNK
    assert t % ts == 0 and ts % ROW_TILE == 0
    full = lambda a: pl.BlockSpec(a.shape, lambda i, j: (0,) * a.ndim)
    return pl.pallas_call(
        _s5_prompt_kernel,
        grid=(b, t // ts),
        in_specs=[pl.BlockSpec((1, n_lt, ts, LANES), lambda i, j: (i, 0, j, 0))] + [full(a) for a in mats],
        out_specs=[pl.BlockSpec((1, n_lt, ts, LANES), lambda i, j: (i, 0, j, 0)),
                   pl.BlockSpec((1, 1, nstate), lambda i, j: (i, 0, 0)),
                   pl.BlockSpec((1, 1, nstate), lambda i, j: (i, 0, 0))],
        out_shape=[jax.ShapeDtypeStruct(u.shape, BF16),
                   jax.ShapeDtypeStruct((b, 1, nstate), F32),
                   jax.ShapeDtypeStruct((b, 1, nstate), F32)],
        scratch_shapes=[pltpu.VMEM((nc, wid * SSM_CHUNK), BF16), pltpu.VMEM((nc, wid * SSM_CHUNK), F32)]
        + [pltpu.VMEM((nc, nstate), F32) for _ in range(4)]
        + [pltpu.VMEM((1, nstate), F32) for _ in range(2)],
        compiler_params=_cparams("arbitrary", "arbitrary"),
    )(u, *mats)


def _s5_step_kernel(u_ref, h0r_ref, h0i_ref, bbr_ref, bbi_ref, cr_ref, ci_ref, ar_ref, ai_ref,
                    y_ref, xr_ref, xi_ref):
    u = u_ref[...]
    ar = ar_ref[...]
    ai = ai_ref[...]
    h0r = h0r_ref[...]
    h0i = h0i_ref[...]
    xr = ar * h0r - ai * h0i + jnp.dot(u, bbr_ref[...], precision=HIGHEST, preferred_element_type=F32)
    xi = ar * h0i + ai * h0r + jnp.dot(u, bbi_ref[...], precision=HIGHEST, preferred_element_type=F32)
    xr_ref[...] = xr
    xi_ref[...] = xi
    y_ref[...] = (jnp.dot(xr, cr_ref[...], precision=HIGHEST, preferred_element_type=F32)
                  - jnp.dot(xi, ci_ref[...], precision=HIGHEST, preferred_element_type=F32))


def _s5_step(u, h0r, h0i, mats):
    n, wid = u.shape
    nstate = h0r.shape[1]
    return pl.pallas_call(
        _s5_step_kernel,
        out_shape=[jax.ShapeDtypeStruct((n, wid), F32),
                   jax.ShapeDtypeStruct((n, nstate), F32),
                   jax.ShapeDtypeStruct((n, nstate), F32)],
        compiler_params=pltpu.CompilerParams(vmem_limit_bytes=VMEM_LIMIT),
    )(u, h0r, h0i, *mats)


def _split_bf16(a):
    hi = a.astype(BF16)
    return hi, (a - hi.astype(F32)).astype(BF16)


def _mix_kernel(x_ref, mod_ref, attn_ref, ys_ref, u_ref, d_ref, wglu_ref, bglu_ref, ga_ref, gs_ref,
                wout_ref, g2_ref, wrh_ref, wrl_ref, br_ref, ungroup_ref, x1_ref, h2_ref, gate_ref, idx_ref, *,
                lane_tiled):
    def rows(ref):
        if lane_tiled:
            return jnp.concatenate([ref[0, i] for i in range(ref.shape[1])], axis=1).astype(F32)
        return ref[...]

    aw = ga_ref.shape[1]
    a_n = _rms(rows(attn_ref)) * ga_ref[...]

    y = rows(ys_ref) + d_ref[...] * rows(u_ref)
    y = 0.5 * y * (1.0 + jnp.tanh(np.float32(np.sqrt(2.0 / np.pi)) * (y + 0.044715 * (y * y * y))))
    z = jnp.dot(y.astype(BF16), wglu_ref[...], preferred_element_type=F32) + bglu_ref[...]
    ssm = y * jax.nn.sigmoid(z)
    s_n = (_rms(ssm) * gs_ref[...]).astype(BF16)
    if lane_tiled:
        s_n = jnp.dot(ungroup_ref[...], s_n, preferred_element_type=F32).astype(BF16)

    mixed = (jnp.dot(a_n.astype(BF16), wout_ref[0:aw, :], preferred_element_type=F32)
             + jnp.dot(s_n, wout_ref[aw:, :], preferred_element_type=F32))
    x1 = x_ref[...] + mod_ref[0, 2] * mixed
    x1_ref[...] = x1
    h2 = _rms(x1) * g2_ref[...] * (1.0 + mod_ref[0, 4]) + mod_ref[0, 3]
    h2_ref[...] = h2.astype(BF16)

    h_hi, h_lo = _split_bf16(h2)
    logits = (jnp.dot(h_hi, wrh_ref[...], preferred_element_type=F32)
              + jnp.dot(h_lo, wrh_ref[...], preferred_element_type=F32)
              + jnp.dot(h_hi, wrl_ref[...], preferred_element_type=F32)) + br_ref[...]
    lane = lax.broadcasted_iota(jnp.int32, logits.shape, 1)
    lane_f = lane.astype(F32)
    cur = logits
    vals, idxs = [], []
    for _ in range(TOP_K):
        mx = jnp.max(cur, axis=1, keepdims=True)
        ix = jnp.min(jnp.where(cur == mx, lane_f, float(LANES)), axis=1, keepdims=True)
        vals.append(mx)
        idxs.append(ix)
        cur = jnp.where(lane_f == ix, -jnp.inf, cur)
    exps = [jnp.exp(v - vals[0]) for v in vals]
    den = exps[0]
    for e in exps[1:]:
        den = den + e
    gate = jnp.zeros(logits.shape, F32)
    idx = jnp.zeros(logits.shape, F32)
    for k in range(TOP_K):
        gate = jnp.where(lane == k, exps[k] / den, gate)
        idx = jnp.where(lane == k, idxs[k], idx)
    gate_ref[...] = gate
    idx_ref[...] = idx.astype(jnp.int32)


def _mix(x2d, mod, attn, ys, u, weights, tm, rows_per_mod):
    n, d = x2d.shape
    r = mod.shape[2]
    tiles_per_mod = max(rows_per_mod // tm, 1)
    mod_map = (lambda i: (i // tiles_per_mod, 0, 0, 0)) if r == 1 else (lambda i: (0, 0, i, 0))
    rowspec = lambda w: pl.BlockSpec((tm, w), lambda i: (i, 0))
    full = lambda a: pl.BlockSpec(a.shape, lambda i: (0,) * a.ndim)
    lane_tiled = attn.ndim == 4
    assert ys.ndim == attn.ndim and u.ndim == attn.ndim
    weights = tuple(weights) + (_regroup_matrix(tm).T,)
    if lane_tiled:
        act_spec = lambda a: pl.BlockSpec((1, a.shape[1], tm, LANES),
                                          lambda i: (i // tiles_per_mod, 0, i % tiles_per_mod, 0))
    else:
        act_spec = lambda a: rowspec(a.shape[1])
    return pl.pallas_call(
        functools.partial(_mix_kernel, lane_tiled=lane_tiled),
        grid=(n // tm,),
        in_specs=[rowspec(d), pl.BlockSpec((1, 6, r, d), mod_map), act_spec(attn), act_spec(ys), act_spec(u)]
        + [full(w) for w in weights],
        out_specs=[rowspec(d), rowspec(d), rowspec(LANES), rowspec(LANES)],
        out_shape=[jax.ShapeDtypeStruct((n, d), F32), jax.ShapeDtypeStruct((n, d), BF16),
                   jax.ShapeDtypeStruct((n, LANES), F32), jax.ShapeDtypeStruct((n, LANES), jnp.int32)],
        compiler_params=_cparams("arbitrary"),
    )(x2d, mod, attn, ys, u, *weights)


def _moe_kernel(be_ref, first_ref, nb_ref, x_ref, wg_ref, bg_ref, wu_ref, bu_ref, wd_ref, bd_ref, o_ref,
                wg_bf, wu_bf, wd_bf):
    i = pl.program_id(0)

    @pl.when(first_ref[i] == 1)
    def _():
        wg_bf[...] = wg_ref[0].astype(BF16)
        wu_bf[...] = wu_ref[0].astype(BF16)
        wd_bf[...] = wd_ref[0].astype(BF16)

    @pl.when(i < nb_ref[0])
    def _():
        x = x_ref[...]
        g = jnp.dot(x, wg_bf[...], preferred_element_type=F32) + bg_ref[0]
        up = jnp.dot(x, wu_bf[...], preferred_element_type=F32) + bu_ref[0]
        g = jnp.minimum(g, SWIGLU_LIMIT)
        up = jnp.clip(up, -SWIGLU_LIMIT, SWIGLU_LIMIT)
        hid = (up + 1.0) * g * jax.nn.sigmoid(SWIGLU_ALPHA * g)
        y = jnp.dot(hid.astype(BF16), wd_bf[...], preferred_element_type=F32) + bd_ref[0]
        o_ref[...] = y.astype(o_ref.dtype)

    @pl.when(i >= nb_ref[0])
    def _():
        o_ref[...] = jnp.zeros(o_ref.shape, o_ref.dtype)


def _moe_experts(xb, block_e, first, n_used, wg, bg, wu, bu, wd, bd):
    n_rows, d = xb.shape
    e, _, f = wg.shape
    nb = n_rows // MOE_BLOCK
    wmap = lambda i, be, fi, nu: (be[i], 0, 0)
    grid_spec = pltpu.PrefetchScalarGridSpec(
        num_scalar_prefetch=3,
        grid=(nb,),
        in_specs=[pl.BlockSpec((MOE_BLOCK, d), lambda i, be, fi, nu: (i, 0)),
                  pl.BlockSpec((1, d, f), wmap), pl.BlockSpec((1, 1, f), wmap),
                  pl.BlockSpec((1, d, f), wmap), pl.BlockSpec((1, 1, f), wmap),
                  pl.BlockSpec((1, f, d), wmap), pl.BlockSpec((1, 1, d), wmap)],
        out_specs=pl.BlockSpec((MOE_BLOCK, d), lambda i, be, fi, nu: (i, 0)),
        scratch_shapes=[pltpu.VMEM((d, f), BF16), pltpu.VMEM((d, f), BF16), pltpu.VMEM((f, d), BF16)],
    )
    return pl.pallas_call(
        _moe_kernel,
        grid_spec=grid_spec,
        out_shape=jax.ShapeDtypeStruct((n_rows, d), BF16),
        compiler_params=_cparams("arbitrary"),
    )(block_e, first, n_used, xb, wg, bg.reshape(e, 1, f), wu, bu.reshape(e, 1, f), wd, bd.reshape(e, 1, d))


def _final_kernel(x1_ref, mod_ref, gate_ref, *rest, tiles_per_part):
    yg_refs, (gf_ref, o_ref) = rest[:-2], rest[-2:]
    i = pl.program_id(0)
    gate = gate_ref[...]
    for p, yg_ref in enumerate(yg_refs):
        @pl.when(i // tiles_per_part == p)
        def _(yg_ref=yg_ref):
            acc = jnp.zeros(x1_ref.shape, F32)
            for k in range(TOP_K):
                acc = acc + gate[:, k:k + 1] * yg_ref[k].astype(F32)
            x = x1_ref[...] + mod_ref[0, 5] * acc
            o_ref[...] = _rms(x) * gf_ref[...]


def _final(x1, mod, gates, yg_parts, g_final, tm, rows_per_mod):
    n, d = x1.shape
    tiles_per_mod = rows_per_mod // tm
    tiles_per_part = yg_parts[0].shape[1] // tm
    part_map = lambda p: (lambda i: (0, jnp.clip(i - p * tiles_per_part, 0, tiles_per_part - 1), 0))
    return pl.pallas_call(
        functools.partial(_final_kernel, tiles_per_part=tiles_per_part),
        grid=(n // tm,),
        in_specs=[pl.BlockSpec((tm, d), lambda i: (i, 0)),
                  pl.BlockSpec((1, 6, 1, d), lambda i: (i // tiles_per_mod, 0, 0, 0)),
                  pl.BlockSpec((tm, LANES), lambda i: (i, 0))]
        + [pl.BlockSpec((TOP_K, tm, d), part_map(p)) for p in range(len(yg_parts))]
        + [pl.BlockSpec((1, d), lambda i: (0, 0))],
        out_specs=pl.BlockSpec((tm, d), lambda i: (i, 0)),
        out_shape=jax.ShapeDtypeStruct((n, d), F32),
        compiler_params=_cparams("arbitrary"),
    )(x1, mod, gates, *yg_parts, g_final.reshape(1, d))


def _moe_decode_kernel(h2_ref, gate_ref, idx_ref, x1_ref, mod_ref, gf_ref,
                       wg_ref, bg_ref, wu_ref, bu_ref, wd_ref, bd_ref, o_ref, acc):
    e = pl.program_id(0)

    @pl.when(e == 0)
    def _():
        acc[...] = jnp.zeros(acc.shape, F32)

    gate_e = jnp.sum(jnp.where(idx_ref[...] == e, gate_ref[...], 0.0), axis=1, keepdims=True)
    x = h2_ref[...]
    g = jnp.dot(x, wg_ref[0].astype(BF16), preferred_element_type=F32) + bg_ref[0]
    up = jnp.dot(x, wu_ref[0].astype(BF16), preferred_element_type=F32) + bu_ref[0]
    g = jnp.minimum(g, SWIGLU_LIMIT)
    up = jnp.clip(up, -SWIGLU_LIMIT, SWIGLU_LIMIT)
    hid = (up + 1.0) * g * jax.nn.sigmoid(SWIGLU_ALPHA * g)
    y = jnp.dot(hid.astype(BF16), wd_ref[0].astype(BF16), preferred_element_type=F32) + bd_ref[0]
    acc[...] += gate_e * y

    @pl.when(e == pl.num_programs(0) - 1)
    def _():
        x1 = x1_ref[...] + mod_ref[0, 5] * acc[...]
        o_ref[...] = _rms(x1) * gf_ref[...]


def _moe_decode(h2, gates, idx, x1, mod, g_final, wg, bg, wu, bu, wd, bd):
    n, d = x1.shape
    e, _, f = wg.shape
    full = lambda a: pl.BlockSpec(a.shape, lambda i: (0,) * a.ndim)
    wmap = lambda i: (i, 0, 0)
    return pl.pallas_call(
        _moe_decode_kernel,
        grid=(e,),
        in_specs=[full(h2), full(gates), full(idx), full(x1), full(mod), pl.BlockSpec((1, d), lambda i: (0, 0)),
                  pl.BlockSpec((1, d, f), wmap), pl.BlockSpec((1, 1, f), wmap),
                  pl.BlockSpec((1, d, f), wmap), pl.BlockSpec((1, 1, f), wmap),
                  pl.BlockSpec((1, f, d), wmap), pl.BlockSpec((1, 1, d), wmap)],
        out_specs=pl.BlockSpec((n, d), lambda i: (0, 0)),
        out_shape=jax.ShapeDtypeStruct((n, d), F32),
        scratch_shapes=[pltpu.VMEM((n, d), F32)],
        compiler_params=_cparams("arbitrary"),
    )(h2, gates, idx, x1, mod, g_final.reshape(1, d), wg, bg.reshape(e, 1, f), wu, bu.reshape(e, 1, f),
      wd, bd.reshape(e, 1, d))


def _dispatch(top_e, n_exp):
    n_assign = top_e.shape[0] * TOP_K
    flat_e = top_e.reshape(-1)
    onehot = (flat_e[:, None] == jnp.arange(n_exp)[None, :]).astype(jnp.int32)
    csum = jnp.cumsum(onehot, axis=0)
    rank = jnp.sum(csum * onehot, axis=1) - 1
    counts = csum[-1]
    padded = (counts + MOE_BLOCK - 1) // MOE_BLOCK * MOE_BLOCK
    pend = jnp.cumsum(padded)
    pstart = pend - padded
    dest = jnp.sum(pstart[None, :] * onehot, axis=1) + rank
    nb = -(-n_assign // MOE_BLOCK) + n_exp
    block_start = jnp.arange(nb, dtype=jnp.int32) * MOE_BLOCK
    block_e = jnp.minimum(jnp.sum((pend[None, :] <= block_start[:, None]).astype(jnp.int32), axis=1), n_exp - 1)
    order = jnp.argsort(flat_e, stable=True).astype(jnp.int32)
    ustart = jnp.cumsum(counts) - counts
    first_src = ustart[block_e] + block_start - pstart[block_e]
    last_src = ustart[block_e] + counts[block_e] - 1
    src = jnp.minimum(first_src[:, None] + jnp.arange(MOE_BLOCK, dtype=jnp.int32)[None, :], last_src[:, None])
    row_tok = order[jnp.clip(src.reshape(-1), 0, n_assign - 1)] // TOP_K
    first = jnp.concatenate([jnp.ones((1,), jnp.int32), (block_e[1:] != block_e[:-1]).astype(jnp.int32)])
    n_used = (pend[-1] // MOE_BLOCK).astype(jnp.int32).reshape(1)
    return row_tok, dest, block_e, first, n_used


def kernel(x_prompt, x_sample, c_prompt, c_sample, cache_k, cache_v, state_ssm_re, state_ssm_im, w_ada, b_ada, g_norm1, g_norm2, w_in, lambda_re, lambda_im, log_dt, b_ssm_re, b_ssm_im, c_ssm_re, c_ssm_im, d_ssm, w_glu, b_glu, g_attn_out, g_ssm_out, w_out, w_router, b_router, w_gate, b_gate, w_up, b_up, w_down, b_down, g_final):
    assert w_ada.shape[0] == 1, "one layer"
    b, t, d = x_prompt.shape
    bs = x_sample.shape[0]
    assert x_sample.shape[1] == 1
    wbuf, n_heads = cache_k.shape[2], cache_k.shape[3]
    aw = n_heads * HEAD_DIM
    n_groups, n_state = lambda_re.shape[1:]
    sw = n_groups * SSM_GROUP
    n_exp = w_router.shape[2]
    keep = min(max(w for w, _ in DILATED_PATTERNS), t)
    tm = min(ROW_TILE, t)

    mod = _ada(jnp.concatenate([c_prompt, c_sample], axis=0), w_ada[0], b_ada[0])
    mod_p = mod[:b].reshape(b, 6, 1, d)
    mod_s = mod[b:].reshape(bs, 6, d).transpose(1, 0, 2)[None]

    w_in_bf = w_in[0].astype(BF16)
    g1 = g_norm1[0].reshape(1, d)
    ssm_params = (lambda_re[0], lambda_im[0], log_dt[0], b_ssm_re[0], b_ssm_im[0], c_ssm_re[0], c_ssm_im[0])
    wr_pad = jnp.zeros((d, LANES), F32).at[:, :n_exp].set(w_router[0])
    wr_hi = wr_pad.astype(BF16)
    wr_lo = (wr_pad - wr_hi.astype(F32)).astype(BF16)
    br_pad = jnp.full((1, LANES), NEG_BIG, F32).at[0, :n_exp].set(b_router[0])
    mix_w = (d_ssm[0].reshape(1, sw), w_glu[0].astype(BF16), b_glu[0].reshape(1, sw),
             g_attn_out[0].reshape(1, aw), g_ssm_out[0].reshape(1, sw), w_out[0].astype(BF16),
             g_norm2[0].reshape(1, d), wr_hi, wr_lo, br_pad)
    experts = (w_gate[0], b_gate[0], w_up[0], b_up[0], w_down[0], b_down[0])

    cos_p, sin_p = _rope_tables(jnp.arange(t), n_heads)
    q_p, k_p, v_p, kt_p, vt_p, u_p = _inproj_prompt(x_prompt, mod_p, g1, w_in_bf, cos_p, sin_p, tm)
    attn_p = _dilated_attention(q_p, k_p, v_p)
    ys_p, hr_p, hi_p = _s5_prompt(u_p, _s5_chunk_matrices(*ssm_params))
    x1_p, h2_p, gate_p, idx_p = _mix(x_prompt.reshape(b * t, d), mod_p, attn_p, ys_p, u_p, mix_w, tm, t)

    mp = b * t // MOE_SPLITS
    routed = []
    for p in range(MOE_SPLITS):
        row_tok, dest, block_e, first, n_used = _dispatch(idx_p[p * mp:(p + 1) * mp, :TOP_K], n_exp)
        routed.append((h2_p.at[row_tok + p * mp].get(mode="promise_in_bounds"), dest, block_e, first, n_used))

    cos_s, sin_s = _rope_tables(jnp.full((1,), PAST_LEN), n_heads)
    qt_s, kt_s, vt_s, u_s = _inproj_decode(x_sample.reshape(bs, d), mod_s, g1, w_in_bf, cos_s, sin_s)
    to_hdp = lambda c: jnp.transpose(c[0], (0, 2, 3, 1))
    from_hdp = lambda c: jnp.transpose(c, (0, 3, 1, 2))[None]
    ck_new, cv_new, attn_t = _decode_attention(qt_s, kt_s, vt_s, to_hdp(cache_k), to_hdp(cache_v))
    ys_s, hr_s, hi_s = _s5_step(u_s, state_ssm_re[0].reshape(bs, n_groups * n_state),
                                state_ssm_im[0].reshape(bs, n_groups * n_state),
                                _s5_step_matrices(*ssm_params))
    x1_s, h2_s, gate_s, idx_s = _mix(x_sample.reshape(bs, d), mod_s, attn_t.T, ys_s, u_s, mix_w, bs, 1)

    yg_parts = []
    for xb, dest, block_e, first, n_used in routed:
        yb = _moe_experts(xb, block_e, first, n_used, *experts)
        yg_parts.append(yb.at[dest.reshape(mp, TOP_K).T].get(mode="promise_in_bounds"))
    y_sample = _moe_decode(h2_s, gate_s, idx_s, x1_s, mod_s, g_final, *experts).reshape(bs, 1, d)
    y_prompt = _final(x1_p, mod_p, gate_p, yg_parts, g_final, tm, t).reshape(b, t, d)

    k_win_p = from_hdp(kt_p[:, :, t - keep:].reshape(b, n_heads, HEAD_DIM, keep))
    v_win_p = from_hdp(vt_p[:, :, t - keep:].reshape(b, n_heads, HEAD_DIM, keep))
    st = lambda a, n: a.reshape(1, n, n_groups, n_state)
    return (y_prompt, y_sample, k_win_p, v_win_p, st(hr_p, b), st(hi_p, b),
            from_hdp(ck_new), from_hdp(cv_new), st(hr_s, bs), st(hi_s, bs))
```

```python
import functools

import jax
import jax.numpy as jnp
import numpy as np
from jax import lax
from jax.experimental import pallas as pl
from jax.experimental.pallas import tpu as pltpu

F32 = jnp.float32
BF16 = jnp.bfloat16
HIGHEST = lax.Precision.HIGHEST

HEAD_DIM = 64
DILATED_PATTERNS = ((128, 1), (512, 4), (2048, 16))
ROPE_THETA = 10000.0
PAST_LEN = 8192
SSM_GROUP = 16
SSM_STATE = 64
TOP_K = 4
SWIGLU_LIMIT = 7.0
SWIGLU_ALPHA = 1.702
RMS_EPS = 1e-6

LANES = 128
HEADS_PER_LANE_TILE = LANES // HEAD_DIM
ATTN_BLOCK = 128
ATTN_UNROLL = 16
ATTN_RESIDUES = 16
DECODE_HEADS = 4
SSM_CHUNK = 16
S5_SLAB = 2048
ROW_TILE = 512
MOE_BLOCK = 512
MOE_SPLITS = 1
VMEM_LIMIT = 52 * 1024 * 1024
NEG_BIG = -1e30


def _cparams(*sem):
    return pltpu.CompilerParams(dimension_semantics=sem, vmem_limit_bytes=VMEM_LIMIT)


def _rms(x):
    return x * lax.rsqrt(jnp.mean(x * x, axis=-1, keepdims=True) + RMS_EPS)


def _ada_kernel(c_ref, w_ref, b_ref, o_ref):
    c = c_ref[...]
    s = c * jax.nn.sigmoid(c)
    o_ref[...] = jnp.dot(s, w_ref[...], precision=HIGHEST, preferred_element_type=F32) + b_ref[...]


def _ada(c, w, b):
    n, d = c.shape
    nout = w.shape[1]
    return pl.pallas_call(
        _ada_kernel,
        grid=(nout // d,),
        in_specs=[pl.BlockSpec((n, d), lambda j: (0, 0)),
                  pl.BlockSpec((d, d), lambda j: (0, j)),
                  pl.BlockSpec((1, d), lambda j: (0, j))],
        out_specs=pl.BlockSpec((n, d), lambda j: (0, j)),
        out_shape=jax.ShapeDtypeStruct((n, nout), F32),
        compiler_params=_cparams("arbitrary"),
    )(c, w, b.reshape(1, nout))


def _project(x, mod_ref, g_ref, w_ref, cos_ref, sin_ref, aw):
    h = _rms(x) * g_ref[...]
    h = h * (1.0 + mod_ref[0, 1]) + mod_ref[0, 0]
    proj = jnp.dot(h.astype(BF16), w_ref[...], preferred_element_type=F32)
    cos = cos_ref[...]
    sin = sin_ref[...]
    lane = lax.broadcasted_iota(jnp.int32, (1, aw), 1)
    first_half = (lane & (HEAD_DIM - 1)) < (HEAD_DIM // 2)

    def rope(t):
        rot = jnp.where(first_half, pltpu.roll(t, aw - HEAD_DIM // 2, 1), pltpu.roll(t, HEAD_DIM // 2, 1))
        return t * cos + rot * sin

    q = rope(proj[:, :aw]) * (HEAD_DIM ** -0.5)
    k = rope(proj[:, aw:2 * aw])
    return q, k, proj[:, 2 * aw:3 * aw], proj[:, 3 * aw:]


def _inproj_prompt_kernel(x_ref, mod_ref, g_ref, w_ref, cos_ref, sin_ref, regroup_ref,
                          q_ref, kv_ref, kt_ref, vt_ref, u_ref, *, aw):
    q, k, v, u = _project(x_ref[0], mod_ref, g_ref, w_ref, cos_ref, sin_ref, aw)
    kt_ref[0] = k.T
    vt_ref[0] = v.T
    g = jnp.dot(regroup_ref[...], jnp.concatenate([q, k, v, u], axis=1).astype(BF16), preferred_element_type=F32)
    kv = pltpu.bitcast(g[:, aw:2 * aw], jnp.uint32) | (pltpu.bitcast(g[:, 2 * aw:3 * aw], jnp.uint32) >> 16)
    n_res, rows = q_ref.shape[2], q_ref.shape[3]
    for hp in range(aw // LANES):
        cols = slice(hp * LANES, (hp + 1) * LANES)
        for r in range(n_res):
            q_ref[0, hp, r] = g[r * rows:(r + 1) * rows, cols]
            kv_ref[0, hp, r] = kv[r * rows:(r + 1) * rows, cols]
    u = g[:, 3 * aw:].astype(BF16)
    for lt in range(u_ref.shape[1]):
        u_ref[0, lt] = u[:, lt * LANES:(lt + 1) * LANES]


def _regroup_matrix(tm):
    lc = SSM_CHUNK
    dst = jnp.arange(tm)
    src = (dst % (tm // lc)) * lc + dst // (tm // lc)
    return (src[:, None] == jnp.arange(tm)[None, :]).astype(BF16)


def _inproj_prompt(x, mod, g, w_bf, cos, sin, tm):
    b, t, d = x.shape
    nproj = w_bf.shape[1]
    aw = cos.shape[1]
    sw = nproj - 3 * aw
    n_lt = aw // LANES
    tok = lambda i, j: (j, i, 0)
    hp_major = lambda i, j: (j, 0, i, 0)
    transposed = lambda i, j: (j, 0, i)
    assert tm // SSM_CHUNK * SSM_CHUNK == tm and SSM_CHUNK == ATTN_RESIDUES
    nr = ATTN_RESIDUES
    res_major = lambda i, j: (j, 0, 0, i, 0)
    out_shapes = [jax.ShapeDtypeStruct((b, n_lt, nr, t // nr, LANES), F32),
                  jax.ShapeDtypeStruct((b, n_lt, nr, t // nr, LANES), jnp.uint32),
        jax.ShapeDtypeStruct((b, aw, t), F32), jax.ShapeDtypeStruct((b, aw, t), F32),
        jax.ShapeDtypeStruct((b, sw // LANES, t, LANES), BF16)]
    return pl.pallas_call(
        functools.partial(_inproj_prompt_kernel, aw=aw),
        grid=(t // tm, b),
        in_specs=[pl.BlockSpec((1, tm, d), tok),
                  pl.BlockSpec((1, 6, 1, d), lambda i, j: (j, 0, 0, 0)),
                  pl.BlockSpec((1, d), lambda i, j: (0, 0)),
                  pl.BlockSpec((d, nproj), lambda i, j: (0, 0)),
                  pl.BlockSpec((tm, aw), lambda i, j: (i, 0)),
                  pl.BlockSpec((tm, aw), lambda i, j: (i, 0)),
                  pl.BlockSpec((tm, tm), lambda i, j: (0, 0))],
        out_specs=[pl.BlockSpec((1, n_lt, nr, tm // nr, LANES), res_major)] * 2 + [
            pl.BlockSpec((1, aw, tm), transposed), pl.BlockSpec((1, aw, tm), transposed),
            pl.BlockSpec((1, sw // LANES, tm, LANES), hp_major)],
        out_shape=out_shapes,
        compiler_params=_cparams("arbitrary", "arbitrary"),
    )(x, mod, g, w_bf, cos, sin, _regroup_matrix(tm))


def _inproj_decode_kernel(x_ref, mod_ref, g_ref, w_ref, cos_ref, sin_ref, qt_ref, kt_ref, vt_ref, u_ref, *, aw):
    q, k, v, u = _project(x_ref[...], mod_ref, g_ref, w_ref, cos_ref, sin_ref, aw)
    qt_ref[...] = q.T
    kt_ref[...] = k.T
    vt_ref[...] = v.T
    u_ref[...] = u


def _inproj_decode(x, mod, g, w_bf, cos, sin):
    n, d = x.shape
    nproj = w_bf.shape[1]
    aw = cos.shape[1]
    return pl.pallas_call(
        functools.partial(_inproj_decode_kernel, aw=aw),
        out_shape=[jax.ShapeDtypeStruct((aw, n), F32)] * 3 + [jax.ShapeDtypeStruct((n, nproj - 3 * aw), F32)],
        compiler_params=pltpu.CompilerParams(vmem_limit_bytes=VMEM_LIMIT),
    )(x, mod, g, w_bf, cos, sin)


def _rope_tables(pos, n_heads):
    half = HEAD_DIM // 2
    inv_freq = ROPE_THETA ** (-jnp.arange(half, dtype=F32) / half)
    ang = pos.astype(F32)[:, None] * inv_freq[None, :]
    cos = jnp.cos(ang)
    sin = jnp.sin(ang)
    cos_h = jnp.concatenate([cos, cos], axis=-1)
    sin_h = jnp.concatenate([-sin, sin], axis=-1)
    return jnp.tile(cos_h, (1, n_heads)), jnp.tile(sin_h, (1, n_heads))


def _dilated_attn_kernel(q_ref, kv_ref, o_ref, acc, mrun, lrun):
    t = q_ref.shape[2]
    blk = ATTN_BLOCK
    l16 = t // ATTN_RESIDUES
    lane = lax.broadcasted_iota(jnp.int32, (1, LANES), 1)
    head_mask = [(lane < HEAD_DIM).astype(F32), (lane >= HEAD_DIM).astype(F32)]
    first = lax.broadcasted_iota(jnp.int32, (blk, LANES), 1) < HEAD_DIM
    row = lax.broadcasted_iota(jnp.int32, (blk, 2 * blk), 0)
    col = lax.broadcasted_iota(jnp.int32, (blk, 2 * blk), 1)
    order = sorted(DILATED_PATTERNS, key=lambda p: p[1])
    for pi, (window, dil) in enumerate(order):
        n_keys = window // dil
        n_piece = ATTN_RESIDUES // dil
        pr = blk // n_piece
        nblk = t // (dil * blk)
        member = lambda p: (p % pr) * n_piece + p // pr
        dist = member(row) + blk - ((col // blk) * blk + member(col % blk))
        bias_std = jnp.where(dist >= 0, jnp.where(dist <= n_keys, 0.0, NEG_BIG), NEG_BIG)
        bias_first = jnp.where(col < blk, NEG_BIG, bias_std)

        def block(idx, carry, pi=pi, dil=dil, n_piece=n_piece, pr=pr, nblk=nblk,
                  bias_std=bias_std, bias_first=bias_first):
            c = idx // nblk
            a = idx % nblk
            a_prev = jnp.maximum(a - 1, 0)
            rows_of = lambda blk_i: [pl.ds(pl.multiple_of((c + dil * j) * l16 + blk_i * pr, pr), pr)
                                     for j in range(n_piece)]
            q_rows = rows_of(a)
            gather = lambda ref, rs: jnp.concatenate([ref[0, 0, r, :] for r in rs], axis=0)
            q2 = gather(q_ref, q_rows)
            kv2 = gather(kv_ref, rows_of(a_prev) + q_rows)
            k2 = pltpu.bitcast(kv2 & jnp.uint32(0xFFFF0000), F32).astype(BF16)
            v2 = pltpu.bitcast(kv2 << 16, F32).astype(BF16)
            bias = jnp.where(a == 0, bias_first, bias_std)
            os, ms, ls = [], [], []
            for half in range(HEADS_PER_LANE_TILE):
                qh = (q2 * head_mask[half]).astype(BF16)
                s = lax.dot_general(qh, k2, (((1,), (1,)), ((), ())), preferred_element_type=F32) + bias
                m = jnp.max(s, axis=1, keepdims=True)
                p = jnp.exp(s - m)
                ls.append(jnp.sum(p, axis=1, keepdims=True))
                ms.append(m)
                os.append(jnp.dot(p.astype(BF16), v2, preferred_element_type=F32))
            o_t = jnp.where(first, os[0], os[1])
            m_t = jnp.where(first, ms[0], ms[1])
            l_t = jnp.where(first, ls[0], ls[1])
            load = lambda ref: jnp.concatenate([ref[r, :] for r in q_rows], axis=0)

            def store(ref, val):
                for j, r in enumerate(q_rows):
                    ref[r, :] = val[j * pr:(j + 1) * pr]

            if pi == 0:
                store(acc, o_t)
                store(mrun, m_t)
                store(lrun, l_t)
            else:
                m_o = load(mrun)
                m_n = jnp.maximum(m_o, m_t)
                a_o = jnp.exp(m_o - m_n)
                a_t = jnp.exp(m_t - m_n)
                acc_n = a_o * load(acc) + a_t * o_t
                l_n = a_o * load(lrun) + a_t * l_t
                if pi == len(order) - 1:
                    assert n_piece == 1
                    o_ref[0, 0, q_rows[0], :] = (acc_n / l_n).astype(o_ref.dtype)
                else:
                    store(acc, acc_n)
                    store(mrun, m_n)
                    store(lrun, l_n)
            return carry

        lax.fori_loop(0, dil * nblk, block, 0, unroll=ATTN_UNROLL)


def _dilated_attention(q, kv):
    b, n_lt, t, _ = q.shape
    for window, dil in DILATED_PATTERNS:
        assert window // dil == ATTN_BLOCK and t % (dil * 2 * ATTN_BLOCK) == 0 and ATTN_RESIDUES % dil == 0
    assert max(dl for _, dl in DILATED_PATTERNS) == ATTN_RESIDUES
    spec = pl.BlockSpec((1, 1, t, LANES), lambda i, j: (i, j, 0, 0))
    return pl.pallas_call(
        _dilated_attn_kernel,
        grid=(b, n_lt),
        in_specs=[spec, spec],
        out_specs=spec,
        out_shape=jax.ShapeDtypeStruct((b, n_lt, t, LANES), BF16),
        scratch_shapes=[pltpu.VMEM((t, LANES), F32) for _ in range(3)],
        compiler_params=_cparams("arbitrary", "arbitrary"),
    )(q, kv)


def _decode_kernel(qt_ref, kt_ref, vt_ref, ck_ref, cv_ref, ok_ref, ov_ref, at_ref, *, hb):
    i = pl.program_id(0)
    hg = pl.program_id(1)
    w = ck_ref.shape[3]
    bs = qt_ref.shape[1]
    sel = lax.broadcasted_iota(jnp.int32, (HEAD_DIM, bs), 1) == i

    def column(ref, rs):
        return jnp.sum(jnp.where(sel, ref[rs, :], 0.0), axis=1, keepdims=True)

    pos = lax.broadcasted_iota(jnp.int32, (1, w), 1)
    dist = w - pos
    cnt = jnp.zeros((1, w), F32)
    for window, dil in DILATED_PATTERNS:
        cnt = cnt + jnp.where((dist & (dil - 1)) == 0, jnp.where(dist <= window, 1.0, 0.0), 0.0)
    n_pat = float(len(DILATED_PATTERNS))
    last = lax.broadcasted_iota(jnp.int32, (HEAD_DIM, w), 1) == w - 1

    @pl.when((i == 0) & (hg == 0))
    def _():
        at_ref[...] = jnp.zeros(at_ref.shape, F32)

    for h in range(hb):
        rs = pl.ds(pl.multiple_of((hg * hb + h) * HEAD_DIM, HEAD_DIM), HEAD_DIM)
        qc, kc, vc = column(qt_ref, rs), column(kt_ref, rs), column(vt_ref, rs)
        kk = ck_ref[0, h]
        vv = cv_ref[0, h]
        s = jnp.sum(kk * qc, axis=0, keepdims=True)
        s_new = jnp.sum(kc * qc, axis=0, keepdims=True)
        s = jnp.where(cnt > 0.0, s, NEG_BIG)
        m = jnp.maximum(jnp.max(s, axis=1, keepdims=True), s_new)
        p = cnt * jnp.exp(s - m)
        p_new = n_pat * jnp.exp(s_new - m)
        l = jnp.sum(p, axis=1, keepdims=True) + p_new
        o = (jnp.sum(vv * p, axis=1, keepdims=True) + p_new * vc) / l
        at_ref[rs, :] = jnp.where(sel, o, at_ref[rs, :])
        ok_ref[0, h] = jnp.where(last, kc, pltpu.roll(kk, w - 1, 1))
        ov_ref[0, h] = jnp.where(last, vc, pltpu.roll(vv, w - 1, 1))


def _decode_attention(qt, kt, vt, ck, cv):
    bs, n_heads, hd, w = ck.shape
    aw = n_heads * hd
    for window, dil in DILATED_PATTERNS:
        assert window <= w and dil & (dil - 1) == 0
    hb = DECODE_HEADS
    full = pl.BlockSpec((aw, bs), lambda i, j: (0, 0))
    buf = pl.BlockSpec((1, hb, hd, w), lambda i, j: (i, j, 0, 0))
    return pl.pallas_call(
        functools.partial(_decode_kernel, hb=hb),
        grid=(bs, n_heads // hb),
        in_specs=[full, full, full, buf, buf],
        out_specs=[buf, buf, full],
        out_shape=[jax.ShapeDtypeStruct(ck.shape, F32), jax.ShapeDtypeStruct(cv.shape, F32),
                   jax.ShapeDtypeStruct((aw, bs), F32)],
        compiler_params=_cparams("arbitrary", "arbitrary"),
    )(qt, kt, vt, ck, cv)


def _s5_discretise(lam_re, lam_im, log_dt, b_re, b_im):
    dt = jnp.exp(log_dt)[:, None]
    mag = jnp.exp(lam_re * dt)
    ar = mag * jnp.cos(lam_im * dt)
    ai = mag * jnp.sin(lam_im * dt)
    den = lam_re * lam_re + lam_im * lam_im
    fr = ((ar - 1.0) * lam_re + ai * lam_im) / den
    fi = (ai * lam_re - (ar - 1.0) * lam_im) / den
    bbr = fr[..., None] * b_re - fi[..., None] * b_im
    bbi = fr[..., None] * b_im + fi[..., None] * b_re
    return dt, ar, ai, bbr, bbi


def _s5_chunk_matrices(lam_re, lam_im, log_dt, b_re, b_im, c_re, c_im):
    lc = SSM_CHUNK
    g, p, c = b_re.shape
    dt, _, _, bbr, bbi = _s5_discretise(lam_re, lam_im, log_dt, b_re, b_im)
    kk = jnp.arange(lc + 1, dtype=F32)[:, None, None]
    mag = jnp.exp(kk * lam_re * dt)
    apr = mag * jnp.cos(kk * lam_im * dt)
    api = mag * jnp.sin(kk * lam_im * dt)
    akb_r = apr[:lc, :, :, None] * bbr - api[:lc, :, :, None] * bbi
    akb_i = apr[:lc, :, :, None] * bbi + api[:lc, :, :, None] * bbr
    kern = (jnp.einsum('gop,kgpc->kgoc', c_re, akb_r, precision=HIGHEST)
            - jnp.einsum('gop,kgpc->kgoc', c_im, akb_i, precision=HIGHEST))
    ii = jnp.arange(lc)
    lag = ii[None, :] - ii[:, None]
    toe = jnp.where((lag >= 0)[:, :, None, None, None], kern[jnp.clip(lag, 0, lc - 1)], 0.0)
    toe = toe.transpose(2, 0, 4, 1, 3).reshape(g, lc * c, lc * c)
    rev = lc - 1 - ii
    ws_r = akb_r[rev].transpose(1, 0, 3, 2).reshape(g, lc * c, p)
    ws_i = akb_i[rev].transpose(1, 0, 3, 2).reshape(g, lc * c, p)
    a1r, a1i = apr[1:], api[1:]
    ca_r = c_re[None] * a1r[:, :, None, :] - c_im[None] * a1i[:, :, None, :]
    ca_i = c_re[None] * a1i[:, :, None, :] + c_im[None] * a1r[:, :, None, :]
    wy_r = ca_r.transpose(1, 3, 0, 2).reshape(g, p, lc * c)
    wy_i = (-ca_i).transpose(1, 3, 0, 2).reshape(g, p, lc * c)

    def pair_diag(m):
        r, s = m.shape[1:]
        m2 = m.reshape(g // 2, 2, r, s)
        z = jnp.zeros((g // 2, r, s), m.dtype)
        top = jnp.concatenate([m2[:, 0], z], axis=2)
        bot = jnp.concatenate([z, m2[:, 1]], axis=2)
        return jnp.concatenate([top, bot], axis=1)

    alc_r = apr[lc].reshape(1, g * p)
    alc_i = api[lc].reshape(1, g * p)
    return (toe.astype(BF16), pair_diag(ws_r).astype(BF16), pair_diag(ws_i).astype(BF16),
            pair_diag(wy_r).astype(BF16), pair_diag(wy_i).astype(BF16), alc_r, alc_i)


def _s5_step_matrices(lam_re, lam_im, log_dt, b_re, b_im, c_re, c_im):
    g, p, c = b_re.shape
    _, ar, ai, bbr, bbi = _s5_discretise(lam_re, lam_im, log_dt, b_re, b_im)
    eye = jnp.eye(g, dtype=F32)
    bd_br = jnp.einsum('gpc,gh->gchp', bbr, eye).reshape(g * c, g * p)
    bd_bi = jnp.einsum('gpc,gh->gchp', bbi, eye).reshape(g * c, g * p)
    bd_cr = jnp.einsum('gcp,gh->gphc', c_re, eye).reshape(g * p, g * c)
    bd_ci = jnp.einsum('gcp,gh->gphc', c_im, eye).reshape(g * p, g * c)
    return bd_br, bd_bi, bd_cr, bd_ci, ar.reshape(1, g * p), ai.reshape(1, g * p)


def _s5_prompt_kernel(u_ref, toe_ref, wsr_ref, wsi_ref, wyr_ref, wyi_ref, ar_ref, ai_ref,
                      y_ref, hr_ref, hi_ref, uperm, yperm, sr, si, hr_hist, hi_hist, hcar_r, hcar_i):
    j = pl.program_id(1)
    nc = uperm.shape[0]
    lc, gc = SSM_CHUNK, SSM_GROUP
    gpt = LANES // gc
    n_lt = u_ref.shape[1]
    rb = ROW_TILE // lc
    gw = lc * gc
    lane_grp = lax.broadcasted_iota(jnp.int32, (1, LANES), 1) // gc

    def group_transpose(vs):
        d = 1
        while d < gpt:
            hi = (lane_grp & d) != 0
            nxt = list(vs)
            for a in range(gpt):
                if a & d == 0:
                    nxt[a] = jnp.where(hi, pltpu.roll(vs[a + d], d * gc, 1), vs[a])
                    nxt[a + d] = jnp.where(hi, vs[a + d], pltpu.roll(vs[a], LANES - d * gc, 1))
            vs = nxt
            d *= 2
        return vs

    def permute_in(rc, carry):
        r0 = pl.multiple_of(rc * rb, rb)
        for lt in range(n_lt):
            for h in range(lc // gpt):
                xs = [u_ref[0, lt, pl.ds(pl.multiple_of(r0 * lc + (h * gpt + il) * rb, rb), rb), :].astype(F32)
                      for il in range(gpt)]
                for gl, d in enumerate(group_transpose(xs)):
                    c0 = (lt * gpt + gl) * gw + h * LANES
                    uperm[pl.ds(r0, rb), c0:c0 + LANES] = d.astype(BF16)
        return carry

    lax.fori_loop(0, nc // rb, permute_in, 0)

    n_pairs = wsr_ref.shape[0]
    pw = wsr_ref.shape[1]
    sw = wsr_ref.shape[2]
    for pr in range(n_pairs):
        up = uperm[:, pr * pw:(pr + 1) * pw]
        sr[:, pr * sw:(pr + 1) * sw] = jnp.dot(up, wsr_ref[pr], preferred_element_type=F32)
        si[:, pr * sw:(pr + 1) * sw] = jnp.dot(up, wsi_ref[pr], preferred_element_type=F32)

    ar = ar_ref[...]
    ai = ai_ref[...]
    nstate = ar.shape[1]
    rowid = lax.broadcasted_iota(jnp.int32, (8, nstate), 0)

    @pl.when(j == 0)
    def _():
        hcar_r[...] = jnp.zeros(hcar_r.shape, F32)
        hcar_i[...] = jnp.zeros(hcar_i.shape, F32)

    def eight_chunks(c8, carry):
        h_r, h_i = carry
        base = pl.multiple_of(c8 * 8, 8)
        s_r8 = sr[pl.ds(base, 8), :]
        s_i8 = si[pl.ds(base, 8), :]
        t_r = jnp.zeros((8, nstate), F32)
        t_i = jnp.zeros((8, nstate), F32)
        for r in range(8):
            t_r = jnp.where(rowid == r, h_r, t_r)
            t_i = jnp.where(rowid == r, h_i, t_i)
            n_r = ar * h_r - ai * h_i + s_r8[r:r + 1, :]
            n_i = ar * h_i + ai * h_r + s_i8[r:r + 1, :]
            h_r, h_i = n_r, n_i
        hr_hist[pl.ds(base, 8), :] = t_r
        hi_hist[pl.ds(base, 8), :] = t_i
        return h_r, h_i

    h_r, h_i = lax.fori_loop(0, nc // 8, eight_chunks, (hcar_r[...], hcar_i[...]))
    hcar_r[...] = h_r
    hcar_i[...] = h_i
    hr_ref[0] = h_r
    hi_ref[0] = h_i

    for pr in range(n_pairs):
        st = slice(pr * sw, (pr + 1) * sw)
        y2 = (jnp.dot(hr_hist[:, st].astype(BF16), wyr_ref[pr], preferred_element_type=F32)
              + jnp.dot(hi_hist[:, st].astype(BF16), wyi_ref[pr], preferred_element_type=F32))
        for gg in range(2):
            g = pr * 2 + gg
            cols = slice(g * gw, (g + 1) * gw)
            yi = jnp.dot(uperm[:, cols], toe_ref[g], preferred_element_type=F32)
            yperm[:, cols] = yi + y2[:, gg * gw:(gg + 1) * gw]

    def permute_out(rc, carry):
        r0 = pl.multiple_of(rc * rb, rb)
        for lt in range(n_lt):
            for h in range(lc // gpt):
                ds_ = [yperm[pl.ds(r0, rb), (lt * gpt + gl) * gw + h * LANES:(lt * gpt + gl) * gw + (h + 1) * LANES]
                       for gl in range(gpt)]
                for il, yv in enumerate(group_transpose(ds_)):
                    y_ref[0, lt, pl.ds(pl.multiple_of(r0 * lc + (h * gpt + il) * rb, rb), rb), :] = yv.astype(y_ref.dtype)
        return carry

    lax.fori_loop(0, nc // rb, permute_out, 0)


def _s5_prompt(u, mats):
    b, n_lt, t, _ = u.shape
    wid = n_lt * LANES
    nstate = mats[-1].shape[1]
    ts = min(t, S5_SLAB)
    nc = ts // SSM_CHUNK
    assert t % ts == 0 and ts % ROW_TILE == 0
    full = lambda a: pl.BlockSpec(a.shape, lambda i, j: (0,) * a.ndim)
    return pl.pallas_call(
        _s5_prompt_kernel,
        grid=(b, t // ts),
        in_specs=[pl.BlockSpec((1, n_lt, ts, LANES), lambda i, j: (i, 0, j, 0))] + [full(a) for a in mats],
        out_specs=[pl.BlockSpec((1, n_lt, ts, LANES), lambda i, j: (i, 0, j, 0)),
                   pl.BlockSpec((1, 1, nstate), lambda i, j: (i, 0, 0)),
                   pl.BlockSpec((1, 1, nstate), lambda i, j: (i, 0, 0))],
        out_shape=[jax.ShapeDtypeStruct(u.shape, BF16),
                   jax.ShapeDtypeStruct((b, 1, nstate), F32),
                   jax.ShapeDtypeStruct((b, 1, nstate), F32)],
        scratch_shapes=[pltpu.VMEM((nc, wid * SSM_CHUNK), BF16), pltpu.VMEM((nc, wid * SSM_CHUNK), F32)]
        + [pltpu.VMEM((nc, nstate), F32) for _ in range(4)]
        + [pltpu.VMEM((1, nstate), F32) for _ in range(2)],
        compiler_params=_cparams("arbitrary", "arbitrary"),
    )(u, *mats)


def _s5_step_kernel(u_ref, h0r_ref, h0i_ref, bbr_ref, bbi_ref, cr_ref, ci_ref, ar_ref, ai_ref,
                    y_ref, xr_ref, xi_ref):
    u = u_ref[...]
    ar = ar_ref[...]
    ai = ai_ref[...]
    h0r = h0r_ref[...]
    h0i = h0i_ref[...]
    xr = ar * h0r - ai * h0i + jnp.dot(u, bbr_ref[...], precision=HIGHEST, preferred_element_type=F32)
    xi = ar * h0i + ai * h0r + jnp.dot(u, bbi_ref[...], precision=HIGHEST, preferred_element_type=F32)
    xr_ref[...] = xr
    xi_ref[...] = xi
    y_ref[...] = (jnp.dot(xr, cr_ref[...], precision=HIGHEST, preferred_element_type=F32)
                  - jnp.dot(xi, ci_ref[...], precision=HIGHEST, preferred_element_type=F32))


def _s5_step(u, h0r, h0i, mats):
    n, wid = u.shape
    nstate = h0r.shape[1]
    return pl.pallas_call(
        _s5_step_kernel,
        out_shape=[jax.ShapeDtypeStruct((n, wid), F32),
                   jax.ShapeDtypeStruct((n, nstate), F32),
                   jax.ShapeDtypeStruct((n, nstate), F32)],
        compiler_params=pltpu.CompilerParams(vmem_limit_bytes=VMEM_LIMIT),
    )(u, h0r, h0i, *mats)


def _split_bf16(a):
    hi = a.astype(BF16)
    return hi, (a - hi.astype(F32)).astype(BF16)


def _mix_kernel(x_ref, mod_ref, attn_ref, ys_ref, u_ref, d_ref, wglu_ref, bglu_ref, ga_ref, gs_ref,
                wout_ref, g2_ref, wrh_ref, wrl_ref, br_ref, ungroup_ref, x1_ref, h2_ref, gate_ref, idx_ref, *,
                lane_tiled):
    def rows(ref):
        if lane_tiled:
            tile = lambda i: ref[0, i].reshape(-1, LANES)
            return jnp.concatenate([tile(i) for i in range(ref.shape[1])], axis=1).astype(F32)
        return ref[...]

    a_n = (_rms(rows(attn_ref)) * ga_ref[...]).astype(BF16)

    y = rows(ys_ref) + d_ref[...] * rows(u_ref)
    y = 0.5 * y * (1.0 + jnp.tanh(np.float32(np.sqrt(2.0 / np.pi)) * (y + 0.044715 * (y * y * y))))
    z = jnp.dot(y.astype(BF16), wglu_ref[...], preferred_element_type=F32) + bglu_ref[...]
    ssm = y * jax.nn.sigmoid(z)
    s_n = (_rms(ssm) * gs_ref[...]).astype(BF16)
    both = jnp.concatenate([a_n, s_n], axis=1)
    if lane_tiled:
        both = jnp.dot(ungroup_ref[...], both, preferred_element_type=F32).astype(BF16)
    mixed = jnp.dot(both, wout_ref[...], preferred_element_type=F32)
    x1 = x_ref[...] + mod_ref[0, 2] * mixed
    x1_ref[...] = x1
    h2 = _rms(x1) * g2_ref[...] * (1.0 + mod_ref[0, 4]) + mod_ref[0, 3]
    h2_ref[...] = h2.astype(BF16)

    h_hi, h_lo = _split_bf16(h2)
    logits = (jnp.dot(h_hi, wrh_ref[...], preferred_element_type=F32)
              + jnp.dot(h_lo, wrh_ref[...], preferred_element_type=F32)
              + jnp.dot(h_hi, wrl_ref[...], preferred_element_type=F32)) + br_ref[...]
    lane = lax.broadcasted_iota(jnp.int32, logits.shape, 1)
    lane_f = lane.astype(F32)
    cur = logits
    vals, idxs = [], []
    for _ in range(TOP_K):
        mx = jnp.max(cur, axis=1, keepdims=True)
        ix = jnp.min(jnp.where(cur == mx, lane_f, float(LANES)), axis=1, keepdims=True)
        vals.append(mx)
        idxs.append(ix)
        cur = jnp.where(lane_f == ix, -jnp.inf, cur)
    exps = [jnp.exp(v - vals[0]) for v in vals]
    den = exps[0]
    for e in exps[1:]:
        den = den + e
    gate = jnp.zeros(logits.shape, F32)
    idx = jnp.zeros(logits.shape, F32)
    for k in range(TOP_K):
        gate = jnp.where(lane == k, exps[k] / den, gate)
        idx = jnp.where(lane == k, idxs[k], idx)
    gate_ref[...] = gate
    idx_ref[...] = idx.astype(jnp.int32)


def _mix(x2d, mod, attn, ys, u, weights, tm, rows_per_mod):
    n, d = x2d.shape
    r = mod.shape[2]
    tiles_per_mod = max(rows_per_mod // tm, 1)
    mod_map = (lambda i: (i // tiles_per_mod, 0, 0, 0)) if r == 1 else (lambda i: (0, 0, i, 0))
    rowspec = lambda w: pl.BlockSpec((tm, w), lambda i: (i, 0))
    full = lambda a: pl.BlockSpec(a.shape, lambda i: (0,) * a.ndim)
    lane_tiled = attn.ndim == 5
    weights = tuple(weights) + (_regroup_matrix(tm).T,)
    if lane_tiled:
        def act_spec(a):
            if a.ndim == 5:
                return pl.BlockSpec((1, a.shape[1], a.shape[2], tm // a.shape[2], LANES),
                                    lambda i: (i // tiles_per_mod, 0, 0, i % tiles_per_mod, 0))
            return pl.BlockSpec((1, a.shape[1], tm, LANES), lambda i: (i // tiles_per_mod, 0, i % tiles_per_mod, 0))
    else:
        act_spec = lambda a: rowspec(a.shape[1])
    return pl.pallas_call(
        functools.partial(_mix_kernel, lane_tiled=lane_tiled),
        grid=(n // tm,),
        in_specs=[rowspec(d), pl.BlockSpec((1, 6, r, d), mod_map), act_spec(attn), act_spec(ys), act_spec(u)]
        + [full(w) for w in weights],
        out_specs=[rowspec(d), rowspec(d), rowspec(LANES), rowspec(LANES)],
        out_shape=[jax.ShapeDtypeStruct((n, d), F32), jax.ShapeDtypeStruct((n, d), BF16),
                   jax.ShapeDtypeStruct((n, LANES), F32), jax.ShapeDtypeStruct((n, LANES), jnp.int32)],
        compiler_params=_cparams("arbitrary"),
    )(x2d, mod, attn, ys, u, *weights)


def _moe_kernel(be_ref, first_ref, nb_ref, x_ref, wg_ref, bg_ref, wu_ref, bu_ref, wd_ref, bd_ref, o_ref,
                wg_bf, wu_bf, wd_bf):
    i = pl.program_id(0)

    @pl.when(first_ref[i] == 1)
    def _():
        wg_bf[...] = wg_ref[0].astype(BF16)
        wu_bf[...] = wu_ref[0].astype(BF16)
        wd_bf[...] = wd_ref[0].astype(BF16)

    @pl.when(i < nb_ref[0])
    def _():
        x = x_ref[...]
        g = jnp.dot(x, wg_bf[...], preferred_element_type=F32) + bg_ref[0]
        up = jnp.dot(x, wu_bf[...], preferred_element_type=F32) + bu_ref[0]
        g = jnp.minimum(g, SWIGLU_LIMIT)
        up = jnp.clip(up, -SWIGLU_LIMIT, SWIGLU_LIMIT)
        hid = (up + 1.0) * g * jax.nn.sigmoid(SWIGLU_ALPHA * g)
        y = jnp.dot(hid.astype(BF16), wd_bf[...], preferred_element_type=F32) + bd_ref[0]
        o_ref[...] = y.astype(o_ref.dtype)

    @pl.when(i >= nb_ref[0])
    def _():
        o_ref[...] = jnp.zeros(o_ref.shape, o_ref.dtype)


def _moe_experts(xb, block_e, first, n_used, wg, bg, wu, bu, wd, bd):
    n_rows, d = xb.shape
    e, _, f = wg.shape
    nb = n_rows // MOE_BLOCK
    wmap = lambda i, be, fi, nu: (be[i], 0, 0)
    grid_spec = pltpu.PrefetchScalarGridSpec(
        num_scalar_prefetch=3,
        grid=(nb,),
        in_specs=[pl.BlockSpec((MOE_BLOCK, d), lambda i, be, fi, nu: (i, 0)),
                  pl.BlockSpec((1, d, f), wmap), pl.BlockSpec((1, 1, f), wmap),
                  pl.BlockSpec((1, d, f), wmap), pl.BlockSpec((1, 1, f), wmap),
                  pl.BlockSpec((1, f, d), wmap), pl.BlockSpec((1, 1, d), wmap)],
        out_specs=pl.BlockSpec((MOE_BLOCK, d), lambda i, be, fi, nu: (i, 0)),
        scratch_shapes=[pltpu.VMEM((d, f), BF16), pltpu.VMEM((d, f), BF16), pltpu.VMEM((f, d), BF16)],
    )
    return pl.pallas_call(
        _moe_kernel,
        grid_spec=grid_spec,
        out_shape=jax.ShapeDtypeStruct((n_rows, d), BF16),
        compiler_params=_cparams("arbitrary"),
    )(block_e, first, n_used, xb, wg, bg.reshape(e, 1, f), wu, bu.reshape(e, 1, f), wd, bd.reshape(e, 1, d))


def _final_kernel(x1_ref, mod_ref, gate_ref, *rest, tiles_per_part):
    yg_refs, (gf_ref, o_ref) = rest[:-2], rest[-2:]
    i = pl.program_id(0)
    gate = gate_ref[...]
    for p, yg_ref in enumerate(yg_refs):
        @pl.when(i // tiles_per_part == p)
        def _(yg_ref=yg_ref):
            acc = jnp.zeros(x1_ref.shape, F32)
            for k in range(TOP_K):
                acc = acc + gate[:, k:k + 1] * yg_ref[k].astype(F32)
            x = x1_ref[...] + mod_ref[0, 5] * acc
            o_ref[...] = _rms(x) * gf_ref[...]


def _final(x1, mod, gates, yg_parts, g_final, tm, rows_per_mod):
    n, d = x1.shape
    tiles_per_mod = rows_per_mod // tm
    tiles_per_part = yg_parts[0].shape[1] // tm
    part_map = lambda p: (lambda i: (0, jnp.clip(i - p * tiles_per_part, 0, tiles_per_part - 1), 0))
    return pl.pallas_call(
        functools.partial(_final_kernel, tiles_per_part=tiles_per_part),
        grid=(n // tm,),
        in_specs=[pl.BlockSpec((tm, d), lambda i: (i, 0)),
                  pl.BlockSpec((1, 6, 1, d), lambda i: (i // tiles_per_mod, 0, 0, 0)),
                  pl.BlockSpec((tm, LANES), lambda i: (i, 0))]
        + [pl.BlockSpec((TOP_K, tm, d), part_map(p)) for p in range(len(yg_parts))]
        + [pl.BlockSpec((1, d), lambda i: (0, 0))],
        out_specs=pl.BlockSpec((tm, d), lambda i: (i, 0)),
        out_shape=jax.ShapeDtypeStruct((n, d), F32),
        compiler_params=_cparams("arbitrary"),
    )(x1, mod, gates, *yg_parts, g_final.reshape(1, d))


def _moe_decode_kernel(h2_ref, gate_ref, idx_ref, x1_ref, mod_ref, gf_ref,
                       wg_ref, bg_ref, wu_ref, bu_ref, wd_ref, bd_ref, o_ref, acc):
    e = pl.program_id(0)

    @pl.when(e == 0)
    def _():
        acc[...] = jnp.zeros(acc.shape, F32)

    gate_e = jnp.sum(jnp.where(idx_ref[...] == e, gate_ref[...], 0.0), axis=1, keepdims=True)
    x = h2_ref[...]
    g = jnp.dot(x, wg_ref[0].astype(BF16), preferred_element_type=F32) + bg_ref[0]
    up = jnp.dot(x, wu_ref[0].astype(BF16), preferred_element_type=F32) + bu_ref[0]
    g = jnp.minimum(g, SWIGLU_LIMIT)
    up = jnp.clip(up, -SWIGLU_LIMIT, SWIGLU_LIMIT)
    hid = (up + 1.0) * g * jax.nn.sigmoid(SWIGLU_ALPHA * g)
    y = jnp.dot(hid.astype(BF16), wd_ref[0].astype(BF16), preferred_element_type=F32) + bd_ref[0]
    acc[...] += gate_e * y

    @pl.when(e == pl.num_programs(0) - 1)
    def _():
        x1 = x1_ref[...] + mod_ref[0, 5] * acc[...]
        o_ref[...] = _rms(x1) * gf_ref[...]


def _moe_decode(h2, gates, idx, x1, mod, g_final, wg, bg, wu, bu, wd, bd):
    n, d = x1.shape
    e, _, f = wg.shape
    full = lambda a: pl.BlockSpec(a.shape, lambda i: (0,) * a.ndim)
    wmap = lambda i: (i, 0, 0)
    return pl.pallas_call(
        _moe_decode_kernel,
        grid=(e,),
        in_specs=[full(h2), full(gates), full(idx), full(x1), full(mod), pl.BlockSpec((1, d), lambda i: (0, 0)),
                  pl.BlockSpec((1, d, f), wmap), pl.BlockSpec((1, 1, f), wmap),
                  pl.BlockSpec((1, d, f), wmap), pl.BlockSpec((1, 1, f), wmap),
                  pl.BlockSpec((1, f, d), wmap), pl.BlockSpec((1, 1, d), wmap)],
        out_specs=pl.BlockSpec((n, d), lambda i: (0, 0)),
        out_shape=jax.ShapeDtypeStruct((n, d), F32),
        scratch_shapes=[pltpu.VMEM((n, d), F32)],
        compiler_params=_cparams("arbitrary"),
    )(h2, gates, idx, x1, mod, g_final.reshape(1, d), wg, bg.reshape(e, 1, f), wu, bu.reshape(e, 1, f),
      wd, bd.reshape(e, 1, d))


def _dispatch(top_e, n_exp):
    n_assign = top_e.shape[0] * TOP_K
    flat_e = top_e.reshape(-1)
    onehot = (flat_e[:, None] == jnp.arange(n_exp)[None, :]).astype(jnp.int32)
    csum = jnp.cumsum(onehot, axis=0)
    rank = jnp.sum(csum * onehot, axis=1) - 1
    counts = csum[-1]
    padded = (counts + MOE_BLOCK - 1) // MOE_BLOCK * MOE_BLOCK
    pend = jnp.cumsum(padded)
    pstart = pend - padded
    dest = jnp.sum(pstart[None, :] * onehot, axis=1) + rank
    nb = -(-n_assign // MOE_BLOCK) + n_exp
    block_start = jnp.arange(nb, dtype=jnp.int32) * MOE_BLOCK
    block_e = jnp.minimum(jnp.sum((pend[None, :] <= block_start[:, None]).astype(jnp.int32), axis=1), n_exp - 1)
    order = jnp.argsort(flat_e, stable=True).astype(jnp.int32)
    ustart = jnp.cumsum(counts) - counts
    first_src = ustart[block_e] + block_start - pstart[block_e]
    last_src = ustart[block_e] + counts[block_e] - 1
    src = jnp.minimum(first_src[:, None] + jnp.arange(MOE_BLOCK, dtype=jnp.int32)[None, :], last_src[:, None])
    row_tok = order[jnp.clip(src.reshape(-1), 0, n_assign - 1)] // TOP_K
    first = jnp.concatenate([jnp.ones((1,), jnp.int32), (block_e[1:] != block_e[:-1]).astype(jnp.int32)])
    n_used = (pend[-1] // MOE_BLOCK).astype(jnp.int32).reshape(1)
    return row_tok, dest, block_e, first, n_used


def kernel(x_prompt, x_sample, c_prompt, c_sample, cache_k, cache_v, state_ssm_re, state_ssm_im, w_ada, b_ada, g_norm1, g_norm2, w_in, lambda_re, lambda_im, log_dt, b_ssm_re, b_ssm_im, c_ssm_re, c_ssm_im, d_ssm, w_glu, b_glu, g_attn_out, g_ssm_out, w_out, w_router, b_router, w_gate, b_gate, w_up, b_up, w_down, b_down, g_final):
    assert w_ada.shape[0] == 1, "one layer"
    b, t, d = x_prompt.shape
    bs = x_sample.shape[0]
    assert x_sample.shape[1] == 1
    wbuf, n_heads = cache_k.shape[2], cache_k.shape[3]
    aw = n_heads * HEAD_DIM
    n_groups, n_state = lambda_re.shape[1:]
    sw = n_groups * SSM_GROUP
    n_exp = w_router.shape[2]
    keep = min(max(w for w, _ in DILATED_PATTERNS), t)
    tm = min(ROW_TILE, t)

    mod = _ada(jnp.concatenate([c_prompt, c_sample], axis=0), w_ada[0], b_ada[0])
    mod_p = mod[:b].reshape(b, 6, 1, d)
    mod_s = mod[b:].reshape(bs, 6, d).transpose(1, 0, 2)[None]

    w_in_bf = w_in[0].astype(BF16)
    g1 = g_norm1[0].reshape(1, d)
    ssm_params = (lambda_re[0], lambda_im[0], log_dt[0], b_ssm_re[0], b_ssm_im[0], c_ssm_re[0], c_ssm_im[0])
    wr_pad = jnp.zeros((d, LANES), F32).at[:, :n_exp].set(w_router[0])
    wr_hi = wr_pad.astype(BF16)
    wr_lo = (wr_pad - wr_hi.astype(F32)).astype(BF16)
    br_pad = jnp.full((1, LANES), NEG_BIG, F32).at[0, :n_exp].set(b_router[0])
    mix_w = (d_ssm[0].reshape(1, sw), w_glu[0].astype(BF16), b_glu[0].reshape(1, sw),
             g_attn_out[0].reshape(1, aw), g_ssm_out[0].reshape(1, sw), w_out[0].astype(BF16),
             g_norm2[0].reshape(1, d), wr_hi, wr_lo, br_pad)
    experts = (w_gate[0], b_gate[0], w_up[0], b_up[0], w_down[0], b_down[0])

    cos_p, sin_p = _rope_tables(jnp.arange(t), n_heads)
    q_p, kv_p, kt_p, vt_p, u_p = _inproj_prompt(x_prompt, mod_p, g1, w_in_bf, cos_p, sin_p, tm)
    flat_t = lambda a: a.reshape(b, a.shape[1], t, LANES)
    attn_p = _dilated_attention(flat_t(q_p), flat_t(kv_p)).reshape(q_p.shape)
    ys_p, hr_p, hi_p = _s5_prompt(u_p, _s5_chunk_matrices(*ssm_params))
    x1_p, h2_p, gate_p, idx_p = _mix(x_prompt.reshape(b * t, d), mod_p, attn_p, ys_p, u_p, mix_w, tm, t)

    mp = b * t // MOE_SPLITS
    routed = []
    for p in range(MOE_SPLITS):
        row_tok, dest, block_e, first, n_used = _dispatch(idx_p[p * mp:(p + 1) * mp, :TOP_K], n_exp)
        routed.append((h2_p.at[row_tok + p * mp].get(mode="promise_in_bounds"), dest, block_e, first, n_used))

    cos_s, sin_s = _rope_tables(jnp.full((1,), PAST_LEN), n_heads)
    qt_s, kt_s, vt_s, u_s = _inproj_decode(x_sample.reshape(bs, d), mod_s, g1, w_in_bf, cos_s, sin_s)
    to_hdp = lambda c: jnp.transpose(c[0], (0, 2, 3, 1))
    from_hdp = lambda c: jnp.transpose(c, (0, 3, 1, 2))[None]
    ck_new, cv_new, attn_t = _decode_attention(qt_s, kt_s, vt_s, to_hdp(cache_k), to_hdp(cache_v))
    ys_s, hr_s, hi_s = _s5_step(u_s, state_ssm_re[0].reshape(bs, n_groups * n_state),
                                state_ssm_im[0].reshape(bs, n_groups * n_state),
                                _s5_step_matrices(*ssm_params))
    x1_s, h2_s, gate_s, idx_s = _mix(x_sample.reshape(bs, d), mod_s, attn_t.T, ys_s, u_s, mix_w, bs, 1)

    yg_parts = []
    for xb, dest, block_e, first, n_used in routed:
        yb = _moe_experts(xb, block_e, first, n_used, *experts)
        yg_parts.append(yb.at[dest.reshape(mp, TOP_K).T].get(mode="promise_in_bounds"))
    y_sample = _moe_decode(h2_s, gate_s, idx_s, x1_s, mod_s, g_final, *experts).reshape(bs, 1, d)
    y_prompt = _final(x1_p, mod_p, gate_p, yg_parts, g_final, tm, t).reshape(b, t, d)

    k_win_p = from_hdp(kt_p[:, :, t - keep:].reshape(b, n_heads, HEAD_DIM, keep))
    v_win_p = from_hdp(vt_p[:, :, t - keep:].reshape(b, n_heads, HEAD_DIM, keep))
    st = lambda a, n: a.reshape(1, n, n_groups, n_state)
    return (y_prompt, y_sample, k_win_p, v_win_p, st(hr_p, b), st(hi_p, b),
            from_hdp(ck_new), from_hdp(cv_new), st(hr_s, bs), st(hi_s, bs))
```

```python
import functools

import jax
import jax.numpy as jnp
import numpy as np
from jax import lax
from jax.experimental import pallas as pl
from jax.experimental.pallas import tpu as pltpu

F32 = jnp.float32
BF16 = jnp.bfloat16
HIGHEST = lax.Precision.HIGHEST

HEAD_DIM = 64
DILATED_PATTERNS = ((128, 1), (512, 4), (2048, 16))
ROPE_THETA = 10000.0
PAST_LEN = 8192
SSM_GROUP = 16
SSM_STATE = 64
TOP_K = 4
SWIGLU_LIMIT = 7.0
SWIGLU_ALPHA = 1.702
RMS_EPS = 1e-6

LANES = 128
HEADS_PER_LANE_TILE = LANES // HEAD_DIM
ATTN_BLOCK = 128
ATTN_UNROLL = 16
ATTN_RESIDUES = 16
DECODE_HEADS = 8
SSM_CHUNK = 16
S5_SLAB = 2048
ROW_TILE = 512
MOE_BLOCK = 512
VMEM_LIMIT = 52 * 1024 * 1024
NEG_BIG = -1e30


def _cparams(*sem):
    return pltpu.CompilerParams(dimension_semantics=sem, vmem_limit_bytes=VMEM_LIMIT)


def _rms(x):
    return x * lax.rsqrt(jnp.mean(x * x, axis=-1, keepdims=True) + RMS_EPS)


def _ada_kernel(c_ref, w_ref, b_ref, o_ref):
    c = c_ref[...]
    s = c * jax.nn.sigmoid(c)
    o_ref[...] = jnp.dot(s, w_ref[...], precision=HIGHEST, preferred_element_type=F32) + b_ref[...]


def _ada(c, w, b):
    n, d = c.shape
    nout = w.shape[1]
    return pl.pallas_call(
        _ada_kernel,
        grid=(nout // d,),
        in_specs=[pl.BlockSpec((n, d), lambda j: (0, 0)),
                  pl.BlockSpec((d, d), lambda j: (0, j)),
                  pl.BlockSpec((1, d), lambda j: (0, j))],
        out_specs=pl.BlockSpec((n, d), lambda j: (0, j)),
        out_shape=jax.ShapeDtypeStruct((n, nout), F32),
        compiler_params=_cparams("arbitrary"),
    )(c, w, b.reshape(1, nout))


def _project(x, mod_ref, g_ref, w_ref, cos_ref, sin_ref, aw):
    h = _rms(x) * g_ref[...]
    h = h * (1.0 + mod_ref[0, 1]) + mod_ref[0, 0]
    proj = jnp.dot(h.astype(BF16), w_ref[...], preferred_element_type=F32)
    cos = cos_ref[...]
    sin = sin_ref[...]
    lane = lax.broadcasted_iota(jnp.int32, (1, aw), 1)
    first_half = (lane & (HEAD_DIM - 1)) < (HEAD_DIM // 2)

    def rope(t):
        rot = jnp.where(first_half, pltpu.roll(t, aw - HEAD_DIM // 2, 1), pltpu.roll(t, HEAD_DIM // 2, 1))
        return t * cos + rot * sin

    q = rope(proj[:, :aw]) * (HEAD_DIM ** -0.5)
    k = rope(proj[:, aw:2 * aw])
    return q, k, proj[:, 2 * aw:3 * aw], proj[:, 3 * aw:]


def _inproj_prompt_kernel(x_ref, mod_ref, g_ref, w_ref, cos_ref, sin_ref, regroup_ref,
                          q_ref, kv_ref, kt_ref, vt_ref, u_ref, *, aw):
    q, k, v, u = _project(x_ref[0], mod_ref, g_ref, w_ref, cos_ref, sin_ref, aw)
    kt_ref[0] = k.T
    vt_ref[0] = v.T
    g = jnp.dot(regroup_ref[...], jnp.concatenate([q, k, v, u], axis=1).astype(BF16), preferred_element_type=F32)
    kv = pltpu.bitcast(g[:, aw:2 * aw], jnp.uint32) | (pltpu.bitcast(g[:, 2 * aw:3 * aw], jnp.uint32) >> 16)
    n_res, rows = q_ref.shape[2], q_ref.shape[3]
    for hp in range(aw // LANES):
        cols = slice(hp * LANES, (hp + 1) * LANES)
        for r in range(n_res):
            q_ref[0, hp, r] = g[r * rows:(r + 1) * rows, cols]
            kv_ref[0, hp, r] = kv[r * rows:(r + 1) * rows, cols]
    u = g[:, 3 * aw:].astype(BF16)
    for lt in range(u_ref.shape[1]):
        u_ref[0, lt] = u[:, lt * LANES:(lt + 1) * LANES]


def _regroup_matrix(tm):
    lc = SSM_CHUNK
    dst = jnp.arange(tm)
    src = (dst % (tm // lc)) * lc + dst // (tm // lc)
    return (src[:, None] == jnp.arange(tm)[None, :]).astype(BF16)


def _inproj_prompt(x, mod, g, w_bf, cos, sin, tm):
    b, t, d = x.shape
    nproj = w_bf.shape[1]
    aw = cos.shape[1]
    sw = nproj - 3 * aw
    n_lt = aw // LANES
    tok = lambda i, j: (j, i, 0)
    hp_major = lambda i, j: (j, 0, i, 0)
    transposed = lambda i, j: (j, 0, i)
    assert tm // SSM_CHUNK * SSM_CHUNK == tm and SSM_CHUNK == ATTN_RESIDUES
    nr = ATTN_RESIDUES
    res_major = lambda i, j: (j, 0, 0, i, 0)
    out_shapes = [jax.ShapeDtypeStruct((b, n_lt, nr, t // nr, LANES), F32),
                  jax.ShapeDtypeStruct((b, n_lt, nr, t // nr, LANES), jnp.uint32),
        jax.ShapeDtypeStruct((b, aw, t), F32), jax.ShapeDtypeStruct((b, aw, t), F32),
        jax.ShapeDtypeStruct((b, sw // LANES, t, LANES), BF16)]
    return pl.pallas_call(
        functools.partial(_inproj_prompt_kernel, aw=aw),
        grid=(t // tm, b),
        in_specs=[pl.BlockSpec((1, tm, d), tok),
                  pl.BlockSpec((1, 6, 1, d), lambda i, j: (j, 0, 0, 0)),
                  pl.BlockSpec((1, d), lambda i, j: (0, 0)),
                  pl.BlockSpec((d, nproj), lambda i, j: (0, 0)),
                  pl.BlockSpec((tm, aw), lambda i, j: (i, 0)),
                  pl.BlockSpec((tm, aw), lambda i, j: (i, 0)),
                  pl.BlockSpec((tm, tm), lambda i, j: (0, 0))],
        out_specs=[pl.BlockSpec((1, n_lt, nr, tm // nr, LANES), res_major)] * 2 + [
            pl.BlockSpec((1, aw, tm), transposed), pl.BlockSpec((1, aw, tm), transposed),
            pl.BlockSpec((1, sw // LANES, tm, LANES), hp_major)],
        out_shape=out_shapes,
        compiler_params=_cparams("arbitrary", "arbitrary"),
    )(x, mod, g, w_bf, cos, sin, _regroup_matrix(tm))


def _inproj_decode_kernel(x_ref, mod_ref, g_ref, w_ref, cos_ref, sin_ref, qt_ref, kt_ref, vt_ref, u_ref, *, aw):
    q, k, v, u = _project(x_ref[...], mod_ref, g_ref, w_ref, cos_ref, sin_ref, aw)
    qt_ref[...] = q.T
    kt_ref[...] = k.T
    vt_ref[...] = v.T
    u_ref[...] = u


def _inproj_decode(x, mod, g, w_bf, cos, sin):
    n, d = x.shape
    nproj = w_bf.shape[1]
    aw = cos.shape[1]
    return pl.pallas_call(
        functools.partial(_inproj_decode_kernel, aw=aw),
        out_shape=[jax.ShapeDtypeStruct((aw, n), F32)] * 3 + [jax.ShapeDtypeStruct((n, nproj - 3 * aw), F32)],
        compiler_params=pltpu.CompilerParams(vmem_limit_bytes=VMEM_LIMIT),
    )(x, mod, g, w_bf, cos, sin)


def _rope_tables(pos, n_heads):
    half = HEAD_DIM // 2
    inv_freq = ROPE_THETA ** (-jnp.arange(half, dtype=F32) / half)
    ang = pos.astype(F32)[:, None] * inv_freq[None, :]
    cos = jnp.cos(ang)
    sin = jnp.sin(ang)
    cos_h = jnp.concatenate([cos, cos], axis=-1)
    sin_h = jnp.concatenate([-sin, sin], axis=-1)
    return jnp.tile(cos_h, (1, n_heads)), jnp.tile(sin_h, (1, n_heads))


def _dilated_attn_kernel(q_ref, kv_ref, o_ref, acc, mrun, lrun):
    t = q_ref.shape[2]
    blk = ATTN_BLOCK
    l16 = t // ATTN_RESIDUES
    lane = lax.broadcasted_iota(jnp.int32, (1, LANES), 1)
    head_mask = [(lane < HEAD_DIM).astype(F32), (lane >= HEAD_DIM).astype(F32)]
    first = lax.broadcasted_iota(jnp.int32, (blk, LANES), 1) < HEAD_DIM
    row = lax.broadcasted_iota(jnp.int32, (blk, 2 * blk), 0)
    col = lax.broadcasted_iota(jnp.int32, (blk, 2 * blk), 1)
    order = sorted(DILATED_PATTERNS, key=lambda p: p[1])
    for pi, (window, dil) in enumerate(order):
        n_keys = window // dil
        n_piece = ATTN_RESIDUES // dil
        pr = blk // n_piece
        nblk = t // (dil * blk)
        member = lambda p: (p % pr) * n_piece + p // pr
        dist = member(row) + blk - ((col // blk) * blk + member(col % blk))
        bias_std = jnp.where(dist >= 0, jnp.where(dist <= n_keys, 0.0, NEG_BIG), NEG_BIG)
        bias_first = jnp.where(col < blk, NEG_BIG, bias_std)

        def block(idx, carry, pi=pi, dil=dil, n_piece=n_piece, pr=pr, nblk=nblk,
                  bias_std=bias_std, bias_first=bias_first):
            c = idx // nblk
            a = idx % nblk
            a_prev = jnp.maximum(a - 1, 0)
            rows_of = lambda blk_i: [pl.ds(pl.multiple_of((c + dil * j) * l16 + blk_i * pr, pr), pr)
                                     for j in range(n_piece)]
            q_rows = rows_of(a)
            gather = lambda ref, rs: jnp.concatenate([ref[0, 0, r, :] for r in rs], axis=0)
            q2 = gather(q_ref, q_rows)
            kv2 = gather(kv_ref, rows_of(a_prev) + q_rows)
            k2 = pltpu.bitcast(kv2 & jnp.uint32(0xFFFF0000), F32).astype(BF16)
            v2 = pltpu.bitcast(kv2 << 16, F32).astype(BF16)
            bias = jnp.where(a == 0, bias_first, bias_std)
            os, ms, ls = [], [], []
            for half in range(HEADS_PER_LANE_TILE):
                qh = (q2 * head_mask[half]).astype(BF16)
                s = lax.dot_general(qh, k2, (((1,), (1,)), ((), ())), preferred_element_type=F32) + bias
                m = jnp.max(s, axis=1, keepdims=True)
                p = jnp.exp(s - m)
                ls.append(jnp.sum(p, axis=1, keepdims=True))
                ms.append(m)
                os.append(jnp.dot(p.astype(BF16), v2, preferred_element_type=F32))
            o_t = jnp.where(first, os[0], os[1])
            m_t = jnp.where(first, ms[0], ms[1])
            l_t = jnp.where(first, ls[0], ls[1])
            load = lambda ref: jnp.concatenate([ref[r, :] for r in q_rows], axis=0)

            def store(ref, val):
                for j, r in enumerate(q_rows):
                    ref[r, :] = val[j * pr:(j + 1) * pr]

            if pi == 0:
                store(acc, o_t)
                store(mrun, m_t)
                store(lrun, l_t)
            else:
                m_o = load(mrun)
                m_n = jnp.maximum(m_o, m_t)
                a_o = jnp.exp(m_o - m_n)
                a_t = jnp.exp(m_t - m_n)
                acc_n = a_o * load(acc) + a_t * o_t
                l_n = a_o * load(lrun) + a_t * l_t
                if pi == len(order) - 1:
                    assert n_piece == 1
                    o_ref[0, 0, q_rows[0], :] = (acc_n / l_n).astype(o_ref.dtype)
                else:
                    store(acc, acc_n)
                    store(mrun, m_n)
                    store(lrun, l_n)
            return carry

        lax.fori_loop(0, dil * nblk, block, 0, unroll=ATTN_UNROLL)


def _dilated_attention(q, kv):
    b, n_lt, t, _ = q.shape
    for window, dil in DILATED_PATTERNS:
        assert window // dil == ATTN_BLOCK and t % (dil * 2 * ATTN_BLOCK) == 0 and ATTN_RESIDUES % dil == 0
    assert max(dl for _, dl in DILATED_PATTERNS) == ATTN_RESIDUES
    spec = pl.BlockSpec((1, 1, t, LANES), lambda i, j: (i, j, 0, 0))
    return pl.pallas_call(
        _dilated_attn_kernel,
        grid=(b, n_lt),
        in_specs=[spec, spec],
        out_specs=spec,
        out_shape=jax.ShapeDtypeStruct((b, n_lt, t, LANES), BF16),
        scratch_shapes=[pltpu.VMEM((t, LANES), F32) for _ in range(3)],
        compiler_params=_cparams("arbitrary", "arbitrary"),
    )(q, kv)


def _decode_kernel(qt_ref, kt_ref, vt_ref, ck_ref, cv_ref, ok_ref, ov_ref, at_ref, *, hb):
    i = pl.program_id(0)
    hg = pl.program_id(1)
    w = ck_ref.shape[3]
    bs = qt_ref.shape[1]
    sel = lax.broadcasted_iota(jnp.int32, (HEAD_DIM, bs), 1) == i

    def column(ref, rs):
        return jnp.sum(jnp.where(sel, ref[rs, :], 0.0), axis=1, keepdims=True)

    pos = lax.broadcasted_iota(jnp.int32, (1, w), 1)
    dist = w - pos
    cnt = jnp.zeros((1, w), F32)
    for window, dil in DILATED_PATTERNS:
        cnt = cnt + jnp.where((dist & (dil - 1)) == 0, jnp.where(dist <= window, 1.0, 0.0), 0.0)
    n_pat = float(len(DILATED_PATTERNS))
    last = lax.broadcasted_iota(jnp.int32, (HEAD_DIM, w), 1) == w - 1

    @pl.when((i == 0) & (hg == 0))
    def _():
        at_ref[...] = jnp.zeros(at_ref.shape, F32)

    for h in range(hb):
        rs = pl.ds(pl.multiple_of((hg * hb + h) * HEAD_DIM, HEAD_DIM), HEAD_DIM)
        qc, kc, vc = column(qt_ref, rs), column(kt_ref, rs), column(vt_ref, rs)
        kk = ck_ref[0, h]
        vv = cv_ref[0, h]
        s = jnp.sum(kk * qc, axis=0, keepdims=True)
        s_new = jnp.sum(kc * qc, axis=0, keepdims=True)
        s = jnp.where(cnt > 0.0, s, NEG_BIG)
        m = jnp.maximum(jnp.max(s, axis=1, keepdims=True), s_new)
        p = cnt * jnp.exp(s - m)
        p_new = n_pat * jnp.exp(s_new - m)
        l = jnp.sum(p, axis=1, keepdims=True) + p_new
        o = (jnp.sum(vv * p, axis=1, keepdims=True) + p_new * vc) / l
        at_ref[rs, :] = jnp.where(sel, o, at_ref[rs, :])
        ok_ref[0, h] = jnp.where(last, kc, pltpu.roll(kk, w - 1, 1))
        ov_ref[0, h] = jnp.where(last, vc, pltpu.roll(vv, w - 1, 1))


def _decode_attention(qt, kt, vt, ck, cv):
    bs, n_heads, hd, w = ck.shape
    aw = n_heads * hd
    for window, dil in DILATED_PATTERNS:
        assert window <= w and dil & (dil - 1) == 0
    hb = DECODE_HEADS
    full = pl.BlockSpec((aw, bs), lambda i, j: (0, 0))
    buf = pl.BlockSpec((1, hb, hd, w), lambda i, j: (i, j, 0, 0))
    return pl.pallas_call(
        functools.partial(_decode_kernel, hb=hb),
        grid=(bs, n_heads // hb),
        in_specs=[full, full, full, buf, buf],
        out_specs=[buf, buf, full],
        out_shape=[jax.ShapeDtypeStruct(ck.shape, F32), jax.ShapeDtypeStruct(cv.shape, F32),
                   jax.ShapeDtypeStruct((aw, bs), F32)],
        compiler_params=_cparams("arbitrary", "arbitrary"),
    )(qt, kt, vt, ck, cv)


def _s5_discretise(lam_re, lam_im, log_dt, b_re, b_im):
    dt = jnp.exp(log_dt)[:, None]
    mag = jnp.exp(lam_re * dt)
    ar = mag * jnp.cos(lam_im * dt)
    ai = mag * jnp.sin(lam_im * dt)
    den = lam_re * lam_re + lam_im * lam_im
    fr = ((ar - 1.0) * lam_re + ai * lam_im) / den
    fi = (ai * lam_re - (ar - 1.0) * lam_im) / den
    bbr = fr[..., None] * b_re - fi[..., None] * b_im
    bbi = fr[..., None] * b_im + fi[..., None] * b_re
    return dt, ar, ai, bbr, bbi


def _s5_chunk_matrices(lam_re, lam_im, log_dt, b_re, b_im, c_re, c_im):
    lc = SSM_CHUNK
    g, p, c = b_re.shape
    dt, _, _, bbr, bbi = _s5_discretise(lam_re, lam_im, log_dt, b_re, b_im)
    kk = jnp.arange(lc + 1, dtype=F32)[:, None, None]
    mag = jnp.exp(kk * lam_re * dt)
    apr = mag * jnp.cos(kk * lam_im * dt)
    api = mag * jnp.sin(kk * lam_im * dt)
    akb_r = apr[:lc, :, :, None] * bbr - api[:lc, :, :, None] * bbi
    akb_i = apr[:lc, :, :, None] * bbi + api[:lc, :, :, None] * bbr
    kern = (jnp.einsum('gop,kgpc->kgoc', c_re, akb_r, precision=HIGHEST)
            - jnp.einsum('gop,kgpc->kgoc', c_im, akb_i, precision=HIGHEST))
    ii = jnp.arange(lc)
    lag = ii[None, :] - ii[:, None]
    toe = jnp.where((lag >= 0)[:, :, None, None, None], kern[jnp.clip(lag, 0, lc - 1)], 0.0)
    toe = toe.transpose(2, 0, 4, 1, 3).reshape(g, lc * c, lc * c)
    rev = lc - 1 - ii
    ws_r = akb_r[rev].transpose(1, 0, 3, 2).reshape(g, lc * c, p)
    ws_i = akb_i[rev].transpose(1, 0, 3, 2).reshape(g, lc * c, p)
    a1r, a1i = apr[1:], api[1:]
    ca_r = c_re[None] * a1r[:, :, None, :] - c_im[None] * a1i[:, :, None, :]
    ca_i = c_re[None] * a1i[:, :, None, :] + c_im[None] * a1r[:, :, None, :]
    wy_r = ca_r.transpose(1, 3, 0, 2).reshape(g, p, lc * c)
    wy_i = (-ca_i).transpose(1, 3, 0, 2).reshape(g, p, lc * c)

    def pair_diag(m):
        r, s = m.shape[1:]
        m2 = m.reshape(g // 2, 2, r, s)
        z = jnp.zeros((g // 2, r, s), m.dtype)
        top = jnp.concatenate([m2[:, 0], z], axis=2)
        bot = jnp.concatenate([z, m2[:, 1]], axis=2)
        return jnp.concatenate([top, bot], axis=1)

    alc_r = apr[lc].reshape(1, g * p)
    alc_i = api[lc].reshape(1, g * p)
    return (toe.astype(BF16), pair_diag(ws_r).astype(BF16), pair_diag(ws_i).astype(BF16),
            pair_diag(wy_r).astype(BF16), pair_diag(wy_i).astype(BF16), alc_r, alc_i)


def _s5_step_matrices(lam_re, lam_im, log_dt, b_re, b_im, c_re, c_im):
    g, p, c = b_re.shape
    _, ar, ai, bbr, bbi = _s5_discretise(lam_re, lam_im, log_dt, b_re, b_im)
    eye = jnp.eye(g, dtype=F32)
    bd_br = jnp.einsum('gpc,gh->gchp', bbr, eye).reshape(g * c, g * p)
    bd_bi = jnp.einsum('gpc,gh->gchp', bbi, eye).reshape(g * c, g * p)
    bd_cr = jnp.einsum('gcp,gh->gphc', c_re, eye).reshape(g * p, g * c)
    bd_ci = jnp.einsum('gcp,gh->gphc', c_im, eye).reshape(g * p, g * c)
    return bd_br, bd_bi, bd_cr, bd_ci, ar.reshape(1, g * p), ai.reshape(1, g * p)


def _s5_prompt_kernel(u_ref, toe_ref, wsr_ref, wsi_ref, wyr_ref, wyi_ref, ar_ref, ai_ref,
                      y_ref, hr_ref, hi_ref, uperm, yperm, sr, si, hr_hist, hi_hist, hcar_r, hcar_i):
    j = pl.program_id(1)
    nc = uperm.shape[0]
    lc, gc = SSM_CHUNK, SSM_GROUP
    gpt = LANES // gc
    n_lt = u_ref.shape[1]
    rb = ROW_TILE // lc
    gw = lc * gc
    lane_grp = lax.broadcasted_iota(jnp.int32, (1, LANES), 1) // gc

    def group_transpose(vs):
        d = 1
        while d < gpt:
            hi = (lane_grp & d) != 0
            nxt = list(vs)
            for a in range(gpt):
                if a & d == 0:
                    nxt[a] = jnp.where(hi, pltpu.roll(vs[a + d], d * gc, 1), vs[a])
                    nxt[a + d] = jnp.where(hi, vs[a + d], pltpu.roll(vs[a], LANES - d * gc, 1))
            vs = nxt
            d *= 2
        return vs

    def permute_in(rc, carry):
        r0 = pl.multiple_of(rc * rb, rb)
        for lt in range(n_lt):
            for h in range(lc // gpt):
                xs = [u_ref[0, lt, pl.ds(pl.multiple_of(r0 * lc + (h * gpt + il) * rb, rb), rb), :].astype(F32)
                      for il in range(gpt)]
                for gl, d in enumerate(group_transpose(xs)):
                    c0 = (lt * gpt + gl) * gw + h * LANES
                    uperm[pl.ds(r0, rb), c0:c0 + LANES] = d.astype(BF16)
        return carry

    lax.fori_loop(0, nc // rb, permute_in, 0)

    n_pairs = wsr_ref.shape[0]
    pw = wsr_ref.shape[1]
    sw = wsr_ref.shape[2]
    for pr in range(n_pairs):
        up = uperm[:, pr * pw:(pr + 1) * pw]
        sr[:, pr * sw:(pr + 1) * sw] = jnp.dot(up, wsr_ref[pr], preferred_element_type=F32)
        si[:, pr * sw:(pr + 1) * sw] = jnp.dot(up, wsi_ref[pr], preferred_element_type=F32)

    ar = ar_ref[...]
    ai = ai_ref[...]
    nstate = ar.shape[1]
    rowid = lax.broadcasted_iota(jnp.int32, (8, nstate), 0)

    @pl.when(j == 0)
    def _():
        hcar_r[...] = jnp.zeros(hcar_r.shape, F32)
        hcar_i[...] = jnp.zeros(hcar_i.shape, F32)

    def eight_chunks(c8, carry):
        h_r, h_i = carry
        base = pl.multiple_of(c8 * 8, 8)
        s_r8 = sr[pl.ds(base, 8), :]
        s_i8 = si[pl.ds(base, 8), :]
        t_r = jnp.zeros((8, nstate), F32)
        t_i = jnp.zeros((8, nstate), F32)
        for r in range(8):
            t_r = jnp.where(rowid == r, h_r, t_r)
            t_i = jnp.where(rowid == r, h_i, t_i)
            n_r = ar * h_r - ai * h_i + s_r8[r:r + 1, :]
            n_i = ar * h_i + ai * h_r + s_i8[r:r + 1, :]
            h_r, h_i = n_r, n_i
        hr_hist[pl.ds(base, 8), :] = t_r
        hi_hist[pl.ds(base, 8), :] = t_i
        return h_r, h_i

    h_r, h_i = lax.fori_loop(0, nc // 8, eight_chunks, (hcar_r[...], hcar_i[...]))
    hcar_r[...] = h_r
    hcar_i[...] = h_i
    hr_ref[0] = h_r
    hi_ref[0] = h_i

    for pr in range(n_pairs):
        st = slice(pr * sw, (pr + 1) * sw)
        y2 = (jnp.dot(hr_hist[:, st].astype(BF16), wyr_ref[pr], preferred_element_type=F32)
              + jnp.dot(hi_hist[:, st].astype(BF16), wyi_ref[pr], preferred_element_type=F32))
        for gg in range(2):
            g = pr * 2 + gg
            cols = slice(g * gw, (g + 1) * gw)
            yi = jnp.dot(uperm[:, cols], toe_ref[g], preferred_element_type=F32)
            yperm[:, cols] = yi + y2[:, gg * gw:(gg + 1) * gw]

    def permute_out(rc, carry):
        r0 = pl.multiple_of(rc * rb, rb)
        for lt in range(n_lt):
            for h in range(lc // gpt):
                ds_ = [yperm[pl.ds(r0, rb), (lt * gpt + gl) * gw + h * LANES:(lt * gpt + gl) * gw + (h + 1) * LANES]
                       for gl in range(gpt)]
                for il, yv in enumerate(group_transpose(ds_)):
                    y_ref[0, lt, pl.ds(pl.multiple_of(r0 * lc + (h * gpt + il) * rb, rb), rb), :] = yv.astype(y_ref.dtype)
        return carry

    lax.fori_loop(0, nc // rb, permute_out, 0)


def _s5_prompt(u, mats):
    b, n_lt, t, _ = u.shape
    wid = n_lt * LANES
    nstate = mats[-1].shape[1]
    ts = min(t, S5_SLAB)
    nc = ts // SSM_CHUNK
    assert t % ts == 0 and ts % ROW_TILE == 0
    full = lambda a: pl.BlockSpec(a.shape, lambda i, j: (0,) * a.ndim)
    return pl.pallas_call(
        _s5_prompt_kernel,
        grid=(b, t // ts),
        in_specs=[pl.BlockSpec((1, n_lt, ts, LANES), lambda i, j: (i, 0, j, 0))] + [full(a) for a in mats],
        out_specs=[pl.BlockSpec((1, n_lt, ts, LANES), lambda i, j: (i, 0, j, 0)),
                   pl.BlockSpec((1, 1, nstate), lambda i, j: (i, 0, 0)),
                   pl.BlockSpec((1, 1, nstate), lambda i, j: (i, 0, 0))],
        out_shape=[jax.ShapeDtypeStruct(u.shape, BF16),
                   jax.ShapeDtypeStruct((b, 1, nstate), F32),
                   jax.ShapeDtypeStruct((b, 1, nstate), F32)],
        scratch_shapes=[pltpu.VMEM((nc, wid * SSM_CHUNK), BF16), pltpu.VMEM((nc, wid * SSM_CHUNK), F32)]
        + [pltpu.VMEM((nc, nstate), F32) for _ in range(4)]
        + [pltpu.VMEM((1, nstate), F32) for _ in range(2)],
        compiler_params=_cparams("arbitrary", "arbitrary"),
    )(u, *mats)


def _s5_step_kernel(u_ref, h0r_ref, h0i_ref, bbr_ref, bbi_ref, cr_ref, ci_ref, ar_ref, ai_ref,
                    y_ref, xr_ref, xi_ref):
    u = u_ref[...]
    ar = ar_ref[...]
    ai = ai_ref[...]
    h0r = h0r_ref[...]
    h0i = h0i_ref[...]
    xr = ar * h0r - ai * h0i + jnp.dot(u, bbr_ref[...], precision=HIGHEST, preferred_element_type=F32)
    xi = ar * h0i + ai * h0r + jnp.dot(u, bbi_ref[...], precision=HIGHEST, preferred_element_type=F32)
    xr_ref[...] = xr
    xi_ref[...] = xi
    y_ref[...] = (jnp.dot(xr, cr_ref[...], precision=HIGHEST, preferred_element_type=F32)
                  - jnp.dot(xi, ci_ref[...], precision=HIGHEST, preferred_element_type=F32))


def _s5_step(u, h0r, h0i, mats):
    n, wid = u.shape
    nstate = h0r.shape[1]
    return pl.pallas_call(
        _s5_step_kernel,
        out_shape=[jax.ShapeDtypeStruct((n, wid), F32),
                   jax.ShapeDtypeStruct((n, nstate), F32),
                   jax.ShapeDtypeStruct((n, nstate), F32)],
        compiler_params=pltpu.CompilerParams(vmem_limit_bytes=VMEM_LIMIT),
    )(u, h0r, h0i, *mats)


def _split_bf16(a):
    hi = a.astype(BF16)
    return hi, (a - hi.astype(F32)).astype(BF16)


def _mix_kernel(x_ref, mod_ref, attn_ref, ys_ref, u_ref, d_ref, wglu_ref, bglu_ref, ga_ref, gs_ref,
                wout_ref, g2_ref, wrh_ref, wrl_ref, br_ref, ungroup_ref, below_ref,
                x1_ref, h2_ref, gate_ref, idx_ref, rank_ref, count_ref, seen, *, lane_tiled):
    def rows(ref):
        if lane_tiled:
            tile = lambda i: ref[0, i].reshape(-1, LANES)
            return jnp.concatenate([tile(i) for i in range(ref.shape[1])], axis=1).astype(F32)
        return ref[...]

    a_n = (_rms(rows(attn_ref)) * ga_ref[...]).astype(BF16)

    y = rows(ys_ref) + d_ref[...] * rows(u_ref)
    y = 0.5 * y * (1.0 + jnp.tanh(np.float32(np.sqrt(2.0 / np.pi)) * (y + 0.044715 * (y * y * y))))
    z = jnp.dot(y.astype(BF16), wglu_ref[...], preferred_element_type=F32) + bglu_ref[...]
    ssm = y * jax.nn.sigmoid(z)
    s_n = (_rms(ssm) * gs_ref[...]).astype(BF16)
    both = jnp.concatenate([a_n, s_n], axis=1)
    if lane_tiled:
        both = jnp.dot(ungroup_ref[...], both, preferred_element_type=F32).astype(BF16)
    mixed = jnp.dot(both, wout_ref[...], preferred_element_type=F32)
    x1 = x_ref[...] + mod_ref[0, 2] * mixed
    x1_ref[...] = x1
    h2 = _rms(x1) * g2_ref[...] * (1.0 + mod_ref[0, 4]) + mod_ref[0, 3]
    h2_ref[...] = h2.astype(BF16)

    h_hi, h_lo = _split_bf16(h2)
    logits = (jnp.dot(h_hi, wrh_ref[...], preferred_element_type=F32)
              + jnp.dot(h_lo, wrh_ref[...], preferred_element_type=F32)
              + jnp.dot(h_hi, wrl_ref[...], preferred_element_type=F32)) + br_ref[...]
    lane = lax.broadcasted_iota(jnp.int32, logits.shape, 1)
    lane_f = lane.astype(F32)
    cur = logits
    vals, idxs = [], []
    for _ in range(TOP_K):
        mx = jnp.max(cur, axis=1, keepdims=True)
        ix = jnp.min(jnp.where(cur == mx, lane_f, float(LANES)), axis=1, keepdims=True)
        vals.append(mx)
        idxs.append(ix)
        cur = jnp.where(lane_f == ix, -jnp.inf, cur)
    exps = [jnp.exp(v - vals[0]) for v in vals]
    den = exps[0]
    for e in exps[1:]:
        den = den + e
    gate = jnp.zeros(logits.shape, F32)
    idx = jnp.zeros(logits.shape, F32)
    for k in range(TOP_K):
        gate = jnp.where(lane == k, exps[k] / den, gate)
        idx = jnp.where(lane == k, idxs[k], idx)
    gate_ref[...] = gate
    idx_ref[...] = idx.astype(jnp.int32)

    @pl.when(pl.program_id(0) == 0)
    def _():
        seen[...] = jnp.zeros(seen.shape, F32)

    chosen = jnp.zeros(logits.shape, F32)
    for k in range(TOP_K):
        chosen = jnp.where(lane_f == idxs[k], 1.0, chosen)
    earlier = jnp.dot(below_ref[...], chosen.astype(BF16), preferred_element_type=F32) + seen[...]
    rank = jnp.zeros(logits.shape, F32)
    for k in range(TOP_K):
        r_k = jnp.sum(jnp.where(lane_f == idxs[k], earlier, 0.0), axis=1, keepdims=True)
        rank = jnp.where(lane == k, r_k, rank)
    rank_ref[...] = rank.astype(jnp.int32)
    seen[...] += jnp.sum(chosen, axis=0, keepdims=True)
    count_ref[...] = seen[...].astype(jnp.int32)


def _mix(x2d, mod, attn, ys, u, weights, tm, rows_per_mod):
    n, d = x2d.shape
    r = mod.shape[2]
    tiles_per_mod = max(rows_per_mod // tm, 1)
    mod_map = (lambda i: (i // tiles_per_mod, 0, 0, 0)) if r == 1 else (lambda i: (0, 0, i, 0))
    rowspec = lambda w: pl.BlockSpec((tm, w), lambda i: (i, 0))
    full = lambda a: pl.BlockSpec(a.shape, lambda i: (0,) * a.ndim)
    lane_tiled = attn.ndim == 5
    below = (jnp.arange(tm)[:, None] > jnp.arange(tm)[None, :]).astype(BF16)
    weights = tuple(weights) + (_regroup_matrix(tm).T, below)
    if lane_tiled:
        def act_spec(a):
            if a.ndim == 5:
                return pl.BlockSpec((1, a.shape[1], a.shape[2], tm // a.shape[2], LANES),
                                    lambda i: (i // tiles_per_mod, 0, 0, i % tiles_per_mod, 0))
            return pl.BlockSpec((1, a.shape[1], tm, LANES), lambda i: (i // tiles_per_mod, 0, i % tiles_per_mod, 0))
    else:
        act_spec = lambda a: rowspec(a.shape[1])
    return pl.pallas_call(
        functools.partial(_mix_kernel, lane_tiled=lane_tiled),
        grid=(n // tm,),
        in_specs=[rowspec(d), pl.BlockSpec((1, 6, r, d), mod_map), act_spec(attn), act_spec(ys), act_spec(u)]
        + [full(w) for w in weights],
        out_specs=[rowspec(d), rowspec(d), rowspec(LANES), rowspec(LANES), rowspec(LANES),
                   pl.BlockSpec((1, LANES), lambda i: (0, 0))],
        out_shape=[jax.ShapeDtypeStruct((n, d), F32), jax.ShapeDtypeStruct((n, d), BF16),
                   jax.ShapeDtypeStruct((n, LANES), F32), jax.ShapeDtypeStruct((n, LANES), jnp.int32),
                   jax.ShapeDtypeStruct((n, LANES), jnp.int32), jax.ShapeDtypeStruct((1, LANES), jnp.int32)],
        scratch_shapes=[pltpu.VMEM((1, LANES), F32)],
        compiler_params=_cparams("arbitrary"),
    )(x2d, mod, attn, ys, u, *weights)


def _moe_kernel(be_ref, first_ref, nb_ref, x_ref, wg_ref, bg_ref, wu_ref, bu_ref, wd_ref, bd_ref, o_ref,
                wg_bf, wu_bf, wd_bf):
    i = pl.program_id(0)

    @pl.when(first_ref[i] == 1)
    def _():
        wg_bf[...] = wg_ref[0].astype(BF16)
        wu_bf[...] = wu_ref[0].astype(BF16)
        wd_bf[...] = wd_ref[0].astype(BF16)

    @pl.when(i < nb_ref[0])
    def _():
        x = x_ref[...]
        g = jnp.dot(x, wg_bf[...], preferred_element_type=F32) + bg_ref[0]
        up = jnp.dot(x, wu_bf[...], preferred_element_type=F32) + bu_ref[0]
        g = jnp.minimum(g, SWIGLU_LIMIT)
        up = jnp.clip(up, -SWIGLU_LIMIT, SWIGLU_LIMIT)
        hid = (up + 1.0) * g * jax.nn.sigmoid(SWIGLU_ALPHA * g)
        y = jnp.dot(hid.astype(BF16), wd_bf[...], preferred_element_type=F32) + bd_ref[0]
        o_ref[...] = y.astype(o_ref.dtype)

    @pl.when(i >= nb_ref[0])
    def _():
        o_ref[...] = jnp.zeros(o_ref.shape, o_ref.dtype)


def _moe_experts(xb, block_e, first, n_used, wg, bg, wu, bu, wd, bd):
    n_rows, d = xb.shape
    e, _, f = wg.shape
    nb = n_rows // MOE_BLOCK
    wmap = lambda i, be, fi, nu: (be[i], 0, 0)
    grid_spec = pltpu.PrefetchScalarGridSpec(
        num_scalar_prefetch=3,
        grid=(nb,),
        in_specs=[pl.BlockSpec((MOE_BLOCK, d), lambda i, be, fi, nu: (i, 0)),
                  pl.BlockSpec((1, d, f), wmap), pl.BlockSpec((1, 1, f), wmap),
                  pl.BlockSpec((1, d, f), wmap), pl.BlockSpec((1, 1, f), wmap),
                  pl.BlockSpec((1, f, d), wmap), pl.BlockSpec((1, 1, d), wmap)],
        out_specs=pl.BlockSpec((MOE_BLOCK, d), lambda i, be, fi, nu: (i, 0)),
        scratch_shapes=[pltpu.VMEM((d, f), BF16), pltpu.VMEM((d, f), BF16), pltpu.VMEM((f, d), BF16)],
    )
    return pl.pallas_call(
        _moe_kernel,
        grid_spec=grid_spec,
        out_shape=jax.ShapeDtypeStruct((n_rows, d), BF16),
        compiler_params=_cparams("arbitrary"),
    )(block_e, first, n_used, xb, wg, bg.reshape(e, 1, f), wu, bu.reshape(e, 1, f), wd, bd.reshape(e, 1, d))


def _final_kernel(x1_ref, mod_ref, gate_ref, *rest, tiles_per_part):
    yg_refs, (gf_ref, o_ref) = rest[:-2], rest[-2:]
    i = pl.program_id(0)
    gate = gate_ref[...]
    for p, yg_ref in enumerate(yg_refs):
        @pl.when(i // tiles_per_part == p)
        def _(yg_ref=yg_ref):
            acc = jnp.zeros(x1_ref.shape, F32)
            for k in range(TOP_K):
                acc = acc + gate[:, k:k + 1] * yg_ref[k].astype(F32)
            x = x1_ref[...] + mod_ref[0, 5] * acc
            o_ref[...] = _rms(x) * gf_ref[...]


def _final(x1, mod, gates, yg_parts, g_final, tm, rows_per_mod):
    n, d = x1.shape
    tiles_per_mod = rows_per_mod // tm
    tiles_per_part = yg_parts[0].shape[1] // tm
    part_map = lambda p: (lambda i: (0, jnp.clip(i - p * tiles_per_part, 0, tiles_per_part - 1), 0))
    return pl.pallas_call(
        functools.partial(_final_kernel, tiles_per_part=tiles_per_part),
        grid=(n // tm,),
        in_specs=[pl.BlockSpec((tm, d), lambda i: (i, 0)),
                  pl.BlockSpec((1, 6, 1, d), lambda i: (i // tiles_per_mod, 0, 0, 0)),
                  pl.BlockSpec((tm, LANES), lambda i: (i, 0))]
        + [pl.BlockSpec((TOP_K, tm, d), part_map(p)) for p in range(len(yg_parts))]
        + [pl.BlockSpec((1, d), lambda i: (0, 0))],
        out_specs=pl.BlockSpec((tm, d), lambda i: (i, 0)),
        out_shape=jax.ShapeDtypeStruct((n, d), F32),
        compiler_params=_cparams("arbitrary"),
    )(x1, mod, gates, *yg_parts, g_final.reshape(1, d))


def _moe_decode_kernel(h2_ref, gate_ref, idx_ref, x1_ref, mod_ref, gf_ref,
                       wg_ref, bg_ref, wu_ref, bu_ref, wd_ref, bd_ref, o_ref, acc):
    e = pl.program_id(0)

    @pl.when(e == 0)
    def _():
        acc[...] = jnp.zeros(acc.shape, F32)

    gate_e = jnp.sum(jnp.where(idx_ref[...] == e, gate_ref[...], 0.0), axis=1, keepdims=True)
    x = h2_ref[...]
    g = jnp.dot(x, wg_ref[0].astype(BF16), preferred_element_type=F32) + bg_ref[0]
    up = jnp.dot(x, wu_ref[0].astype(BF16), preferred_element_type=F32) + bu_ref[0]
    g = jnp.minimum(g, SWIGLU_LIMIT)
    up = jnp.clip(up, -SWIGLU_LIMIT, SWIGLU_LIMIT)
    hid = (up + 1.0) * g * jax.nn.sigmoid(SWIGLU_ALPHA * g)
    y = jnp.dot(hid.astype(BF16), wd_ref[0].astype(BF16), preferred_element_type=F32) + bd_ref[0]
    acc[...] += gate_e * y

    @pl.when(e == pl.num_programs(0) - 1)
    def _():
        x1 = x1_ref[...] + mod_ref[0, 5] * acc[...]
        o_ref[...] = _rms(x1) * gf_ref[...]


def _moe_decode(h2, gates, idx, x1, mod, g_final, wg, bg, wu, bu, wd, bd):
    n, d = x1.shape
    e, _, f = wg.shape
    full = lambda a: pl.BlockSpec(a.shape, lambda i: (0,) * a.ndim)
    wmap = lambda i: (i, 0, 0)
    return pl.pallas_call(
        _moe_decode_kernel,
        grid=(e,),
        in_specs=[full(h2), full(gates), full(idx), full(x1), full(mod), pl.BlockSpec((1, d), lambda i: (0, 0)),
                  pl.BlockSpec((1, d, f), wmap), pl.BlockSpec((1, 1, f), wmap),
                  pl.BlockSpec((1, d, f), wmap), pl.BlockSpec((1, 1, f), wmap),
                  pl.BlockSpec((1, f, d), wmap), pl.BlockSpec((1, 1, d), wmap)],
        out_specs=pl.BlockSpec((n, d), lambda i: (0, 0)),
        out_shape=jax.ShapeDtypeStruct((n, d), F32),
        scratch_shapes=[pltpu.VMEM((n, d), F32)],
        compiler_params=_cparams("arbitrary"),
    )(h2, gates, idx, x1, mod, g_final.reshape(1, d), wg, bg.reshape(e, 1, f), wu, bu.reshape(e, 1, f),
      wd, bd.reshape(e, 1, d))


def _dispatch(top_e, rank, counts):
    n_exp = counts.shape[0]
    n_assign = top_e.shape[0] * TOP_K
    padded = (counts + MOE_BLOCK - 1) // MOE_BLOCK * MOE_BLOCK
    pend = jnp.cumsum(padded)
    pstart = pend - padded
    dest = pstart[top_e] + rank
    nb = -(-n_assign // MOE_BLOCK) + n_exp
    block_start = jnp.arange(nb, dtype=jnp.int32) * MOE_BLOCK
    block_e = jnp.minimum(jnp.sum((pend[None, :] <= block_start[:, None]).astype(jnp.int32), axis=1), n_exp - 1)
    bits = max(n_assign - 1, 1).bit_length()
    assert (n_exp << bits) < 2 ** 31
    packed = (top_e.reshape(-1) << bits) | jnp.arange(n_assign, dtype=jnp.int32)
    order = jnp.sort(packed) & ((1 << bits) - 1)
    ustart = jnp.cumsum(counts) - counts
    first_src = ustart[block_e] + block_start - pstart[block_e]
    last_src = ustart[block_e] + counts[block_e] - 1
    src = jnp.minimum(first_src[:, None] + jnp.arange(MOE_BLOCK, dtype=jnp.int32)[None, :], last_src[:, None])
    row_tok = order[jnp.clip(src.reshape(-1), 0, n_assign - 1)] // TOP_K
    first = jnp.concatenate([jnp.ones((1,), jnp.int32), (block_e[1:] != block_e[:-1]).astype(jnp.int32)])
    n_used = (pend[-1] // MOE_BLOCK).astype(jnp.int32).reshape(1)
    return row_tok, dest, block_e, first, n_used


def kernel(x_prompt, x_sample, c_prompt, c_sample, cache_k, cache_v, state_ssm_re, state_ssm_im, w_ada, b_ada, g_norm1, g_norm2, w_in, lambda_re, lambda_im, log_dt, b_ssm_re, b_ssm_im, c_ssm_re, c_ssm_im, d_ssm, w_glu, b_glu, g_attn_out, g_ssm_out, w_out, w_router, b_router, w_gate, b_gate, w_up, b_up, w_down, b_down, g_final):
    assert w_ada.shape[0] == 1, "one layer"
    b, t, d = x_prompt.shape
    bs = x_sample.shape[0]
    assert x_sample.shape[1] == 1
    wbuf, n_heads = cache_k.shape[2], cache_k.shape[3]
    aw = n_heads * HEAD_DIM
    n_groups, n_state = lambda_re.shape[1:]
    sw = n_groups * SSM_GROUP
    n_exp = w_router.shape[2]
    keep = min(max(w for w, _ in DILATED_PATTERNS), t)
    tm = min(ROW_TILE, t)

    mod = _ada(jnp.concatenate([c_prompt, c_sample], axis=0), w_ada[0], b_ada[0])
    mod_p = mod[:b].reshape(b, 6, 1, d)
    mod_s = mod[b:].reshape(bs, 6, d).transpose(1, 0, 2)[None]

    w_in_bf = w_in[0].astype(BF16)
    g1 = g_norm1[0].reshape(1, d)
    ssm_params = (lambda_re[0], lambda_im[0], log_dt[0], b_ssm_re[0], b_ssm_im[0], c_ssm_re[0], c_ssm_im[0])
    wr_pad = jnp.zeros((d, LANES), F32).at[:, :n_exp].set(w_router[0])
    wr_hi = wr_pad.astype(BF16)
    wr_lo = (wr_pad - wr_hi.astype(F32)).astype(BF16)
    br_pad = jnp.full((1, LANES), NEG_BIG, F32).at[0, :n_exp].set(b_router[0])
    mix_w = (d_ssm[0].reshape(1, sw), w_glu[0].astype(BF16), b_glu[0].reshape(1, sw),
             g_attn_out[0].reshape(1, aw), g_ssm_out[0].reshape(1, sw), w_out[0].astype(BF16),
             g_norm2[0].reshape(1, d), wr_hi, wr_lo, br_pad)
    experts = (w_gate[0], b_gate[0], w_up[0], b_up[0], w_down[0], b_down[0])

    cos_p, sin_p = _rope_tables(jnp.arange(t), n_heads)
    q_p, kv_p, kt_p, vt_p, u_p = _inproj_prompt(x_prompt, mod_p, g1, w_in_bf, cos_p, sin_p, tm)
    flat_t = lambda a: a.reshape(b, a.shape[1], t, LANES)
    attn_p = _dilated_attention(flat_t(q_p), flat_t(kv_p)).reshape(q_p.shape)
    ys_p, hr_p, hi_p = _s5_prompt(u_p, _s5_chunk_matrices(*ssm_params))
    x1_p, h2_p, gate_p, idx_p, rank_p, count_p = _mix(x_prompt.reshape(b * t, d), mod_p, attn_p, ys_p, u_p,
                                                      mix_w, tm, t)

    row_tok, dest, block_e, first, n_used = _dispatch(idx_p[:, :TOP_K], rank_p[:, :TOP_K], count_p[0, :n_exp])
    xb = h2_p.at[row_tok].get(mode="promise_in_bounds")

    cos_s, sin_s = _rope_tables(jnp.full((1,), PAST_LEN), n_heads)
    qt_s, kt_s, vt_s, u_s = _inproj_decode(x_sample.reshape(bs, d), mod_s, g1, w_in_bf, cos_s, sin_s)
    to_hdp = lambda c: jnp.transpose(c[0], (0, 2, 3, 1))
    from_hdp = lambda c: jnp.transpose(c, (0, 3, 1, 2))[None]
    ck_new, cv_new, attn_t = _decode_attention(qt_s, kt_s, vt_s, to_hdp(cache_k), to_hdp(cache_v))
    ys_s, hr_s, hi_s = _s5_step(u_s, state_ssm_re[0].reshape(bs, n_groups * n_state),
                                state_ssm_im[0].reshape(bs, n_groups * n_state),
                                _s5_step_matrices(*ssm_params))
    x1_s, h2_s, gate_s, idx_s, _, _ = _mix(x_sample.reshape(bs, d), mod_s, attn_t.T, ys_s, u_s, mix_w, bs, 1)

    yb = _moe_experts(xb, block_e, first, n_used, *experts)
    yg = yb.at[dest.T].get(mode="promise_in_bounds")
    y_sample = _moe_decode(h2_s, gate_s, idx_s, x1_s, mod_s, g_final, *experts).reshape(bs, 1, d)
    y_prompt = _final(x1_p, mod_p, gate_p, [yg], g_final, tm, t).reshape(b, t, d)

    k_win_p = from_hdp(kt_p[:, :, t - keep:].reshape(b, n_heads, HEAD_DIM, keep))
    v_win_p = from_hdp(vt_p[:, :, t - keep:].reshape(b, n_heads, HEAD_DIM, keep))
    st = lambda a, n: a.reshape(1, n, n_groups, n_state)
    return (y_prompt, y_sample, k_win_p, v_win_p, st(hr_p, b), st(hi_p, b),
            from_hdp(ck_new), from_hdp(cv_new), st(hr_s, bs), st(hi_s, bs))
```

```python
import functools

import jax
import jax.numpy as jnp
import numpy as np
from jax import lax
from jax.experimental import pallas as pl
from jax.experimental.pallas import tpu as pltpu

F32 = jnp.float32
BF16 = jnp.bfloat16
HIGHEST = lax.Precision.HIGHEST

HEAD_DIM = 64
DILATED_PATTERNS = ((128, 1), (512, 4), (2048, 16))
ROPE_THETA = 10000.0
PAST_LEN = 8192
SSM_GROUP = 16
SSM_STATE = 64
TOP_K = 4
SWIGLU_LIMIT = 7.0
SWIGLU_ALPHA = 1.702
RMS_EPS = 1e-6

LANES = 128
HEADS_PER_LANE_TILE = LANES // HEAD_DIM
ATTN_BLOCK = 128
ATTN_UNROLL = 16
ATTN_RESIDUES = 16
DECODE_HEADS = 8
SSM_CHUNK = 16
S5_SLAB = 2048
ROW_TILE = 512
MOE_BLOCK = 512
FINAL_TILE = 1024
VMEM_LIMIT = 52 * 1024 * 1024
NEG_BIG = -1e30


def _cparams(*sem):
    return pltpu.CompilerParams(dimension_semantics=sem, vmem_limit_bytes=VMEM_LIMIT)


def _rms(x):
    return x * lax.rsqrt(jnp.mean(x * x, axis=-1, keepdims=True) + RMS_EPS)


def _ada_kernel(c_ref, w_ref, b_ref, o_ref):
    c = c_ref[...]
    s = c * jax.nn.sigmoid(c)
    o_ref[...] = jnp.dot(s, w_ref[...], precision=HIGHEST, preferred_element_type=F32) + b_ref[...]


def _ada(c, w, b):
    n, d = c.shape
    nout = w.shape[1]
    return pl.pallas_call(
        _ada_kernel,
        grid=(nout // d,),
        in_specs=[pl.BlockSpec((n, d), lambda j: (0, 0)),
                  pl.BlockSpec((d, d), lambda j: (0, j)),
                  pl.BlockSpec((1, d), lambda j: (0, j))],
        out_specs=pl.BlockSpec((n, d), lambda j: (0, j)),
        out_shape=jax.ShapeDtypeStruct((n, nout), F32),
        compiler_params=_cparams("arbitrary"),
    )(c, w, b.reshape(1, nout))


def _project(x, mod_ref, g_ref, w_ref, cos_ref, sin_ref, aw):
    h = _rms(x) * g_ref[...]
    h = h * (1.0 + mod_ref[0, 1]) + mod_ref[0, 0]
    proj = jnp.dot(h.astype(BF16), w_ref[...], preferred_element_type=F32)
    cos = cos_ref[...]
    sin = sin_ref[...]
    lane = lax.broadcasted_iota(jnp.int32, (1, aw), 1)
    first_half = (lane & (HEAD_DIM - 1)) < (HEAD_DIM // 2)

    def rope(t):
        rot = jnp.where(first_half, pltpu.roll(t, aw - HEAD_DIM // 2, 1), pltpu.roll(t, HEAD_DIM // 2, 1))
        return t * cos + rot * sin

    q = rope(proj[:, :aw]) * (HEAD_DIM ** -0.5)
    k = rope(proj[:, aw:2 * aw])
    return q, k, proj[:, 2 * aw:3 * aw], proj[:, 3 * aw:]


def _inproj_prompt_kernel(x_ref, mod_ref, g_ref, w_ref, cos_ref, sin_ref, regroup_ref,
                          q_ref, kv_ref, kt_ref, vt_ref, u_ref, *, aw):
    q, k, v, u = _project(x_ref[0], mod_ref, g_ref, w_ref, cos_ref, sin_ref, aw)
    kt_ref[0] = k.T
    vt_ref[0] = v.T
    g = jnp.dot(regroup_ref[...], jnp.concatenate([q, k, v, u], axis=1).astype(BF16), preferred_element_type=F32)
    kv = pltpu.bitcast(g[:, aw:2 * aw], jnp.uint32) | (pltpu.bitcast(g[:, 2 * aw:3 * aw], jnp.uint32) >> 16)
    n_res, rows = q_ref.shape[2], q_ref.shape[3]
    for hp in range(aw // LANES):
        cols = slice(hp * LANES, (hp + 1) * LANES)
        for r in range(n_res):
            q_ref[0, hp, r] = g[r * rows:(r + 1) * rows, cols]
            kv_ref[0, hp, r] = kv[r * rows:(r + 1) * rows, cols]
    u = g[:, 3 * aw:].astype(BF16)
    for lt in range(u_ref.shape[1]):
        u_ref[0, lt] = u[:, lt * LANES:(lt + 1) * LANES]


def _regroup_matrix(tm):
    lc = SSM_CHUNK
    dst = jnp.arange(tm)
    src = (dst % (tm // lc)) * lc + dst // (tm // lc)
    return (src[:, None] == jnp.arange(tm)[None, :]).astype(BF16)


def _inproj_prompt(x, mod, g, w_bf, cos, sin, tm):
    b, t, d = x.shape
    nproj = w_bf.shape[1]
    aw = cos.shape[1]
    sw = nproj - 3 * aw
    n_lt = aw // LANES
    tok = lambda i, j: (j, i, 0)
    hp_major = lambda i, j: (j, 0, i, 0)
    transposed = lambda i, j: (j, 0, i)
    assert tm // SSM_CHUNK * SSM_CHUNK == tm and SSM_CHUNK == ATTN_RESIDUES
    nr = ATTN_RESIDUES
    res_major = lambda i, j: (j, 0, 0, i, 0)
    out_shapes = [jax.ShapeDtypeStruct((b, n_lt, nr, t // nr, LANES), F32),
                  jax.ShapeDtypeStruct((b, n_lt, nr, t // nr, LANES), jnp.uint32),
        jax.ShapeDtypeStruct((b, aw, t), F32), jax.ShapeDtypeStruct((b, aw, t), F32),
        jax.ShapeDtypeStruct((b, sw // LANES, t, LANES), BF16)]
    return pl.pallas_call(
        functools.partial(_inproj_prompt_kernel, aw=aw),
        grid=(t // tm, b),
        in_specs=[pl.BlockSpec((1, tm, d), tok),
                  pl.BlockSpec((1, 6, 1, d), lambda i, j: (j, 0, 0, 0)),
                  pl.BlockSpec((1, d), lambda i, j: (0, 0)),
                  pl.BlockSpec((d, nproj), lambda i, j: (0, 0)),
                  pl.BlockSpec((tm, aw), lambda i, j: (i, 0)),
                  pl.BlockSpec((tm, aw), lambda i, j: (i, 0)),
                  pl.BlockSpec((tm, tm), lambda i, j: (0, 0))],
        out_specs=[pl.BlockSpec((1, n_lt, nr, tm // nr, LANES), res_major)] * 2 + [
            pl.BlockSpec((1, aw, tm), transposed), pl.BlockSpec((1, aw, tm), transposed),
            pl.BlockSpec((1, sw // LANES, tm, LANES), hp_major)],
        out_shape=out_shapes,
        compiler_params=_cparams("arbitrary", "arbitrary"),
    )(x, mod, g, w_bf, cos, sin, _regroup_matrix(tm))


def _inproj_decode_kernel(x_ref, mod_ref, g_ref, w_ref, cos_ref, sin_ref, qt_ref, kt_ref, vt_ref, u_ref, *, aw):
    q, k, v, u = _project(x_ref[...], mod_ref, g_ref, w_ref, cos_ref, sin_ref, aw)
    qt_ref[...] = q.T
    kt_ref[...] = k.T
    vt_ref[...] = v.T
    u_ref[...] = u


def _inproj_decode(x, mod, g, w_bf, cos, sin):
    n, d = x.shape
    nproj = w_bf.shape[1]
    aw = cos.shape[1]
    return pl.pallas_call(
        functools.partial(_inproj_decode_kernel, aw=aw),
        out_shape=[jax.ShapeDtypeStruct((aw, n), F32)] * 3 + [jax.ShapeDtypeStruct((n, nproj - 3 * aw), F32)],
        compiler_params=pltpu.CompilerParams(vmem_limit_bytes=VMEM_LIMIT),
    )(x, mod, g, w_bf, cos, sin)


def _rope_tables(pos, n_heads):
    half = HEAD_DIM // 2
    inv_freq = ROPE_THETA ** (-jnp.arange(half, dtype=F32) / half)
    ang = pos.astype(F32)[:, None] * inv_freq[None, :]
    cos = jnp.cos(ang)
    sin = jnp.sin(ang)
    cos_h = jnp.concatenate([cos, cos], axis=-1)
    sin_h = jnp.concatenate([-sin, sin], axis=-1)
    return jnp.tile(cos_h, (1, n_heads)), jnp.tile(sin_h, (1, n_heads))


def _dilated_attn_kernel(q_ref, kv_ref, o_ref, acc, mrun, lrun):
    t = q_ref.shape[2]
    blk = ATTN_BLOCK
    l16 = t // ATTN_RESIDUES
    lane = lax.broadcasted_iota(jnp.int32, (1, LANES), 1)
    head_mask = [(lane < HEAD_DIM).astype(F32), (lane >= HEAD_DIM).astype(F32)]
    first = lax.broadcasted_iota(jnp.int32, (blk, LANES), 1) < HEAD_DIM
    row = lax.broadcasted_iota(jnp.int32, (blk, 2 * blk), 0)
    col = lax.broadcasted_iota(jnp.int32, (blk, 2 * blk), 1)
    order = sorted(DILATED_PATTERNS, key=lambda p: p[1])
    for pi, (window, dil) in enumerate(order):
        n_keys = window // dil
        n_piece = ATTN_RESIDUES // dil
        pr = blk // n_piece
        nblk = t // (dil * blk)
        member = lambda p: (p % pr) * n_piece + p // pr
        dist = member(row) + blk - ((col // blk) * blk + member(col % blk))
        bias_std = jnp.where(dist >= 0, jnp.where(dist <= n_keys, 0.0, NEG_BIG), NEG_BIG)
        bias_first = jnp.where(col < blk, NEG_BIG, bias_std)

        def block(idx, carry, pi=pi, dil=dil, n_piece=n_piece, pr=pr, nblk=nblk,
                  bias_std=bias_std, bias_first=bias_first):
            c = idx // nblk
            a = idx % nblk
            a_prev = jnp.maximum(a - 1, 0)
            rows_of = lambda blk_i: [pl.ds(pl.multiple_of((c + dil * j) * l16 + blk_i * pr, pr), pr)
                                     for j in range(n_piece)]
            q_rows = rows_of(a)
            gather = lambda ref, rs: jnp.concatenate([ref[0, 0, r, :] for r in rs], axis=0)
            q2 = gather(q_ref, q_rows)
            kv2 = gather(kv_ref, rows_of(a_prev) + q_rows)
            k2 = pltpu.bitcast(kv2 & jnp.uint32(0xFFFF0000), F32).astype(BF16)
            v2 = pltpu.bitcast(kv2 << 16, F32).astype(BF16)
            bias = jnp.where(a == 0, bias_first, bias_std)
            os, ms, ls = [], [], []
            for half in range(HEADS_PER_LANE_TILE):
                qh = (q2 * head_mask[half]).astype(BF16)
                s = lax.dot_general(qh, k2, (((1,), (1,)), ((), ())), preferred_element_type=F32) + bias
                m = jnp.max(s, axis=1, keepdims=True)
                p = jnp.exp(s - m)
                ls.append(jnp.sum(p, axis=1, keepdims=True))
                ms.append(m)
                os.append(jnp.dot(p.astype(BF16), v2, preferred_element_type=F32))
            o_t = jnp.where(first, os[0], os[1])
            m_t = jnp.where(first, ms[0], ms[1])
            l_t = jnp.where(first, ls[0], ls[1])
            load = lambda ref: jnp.concatenate([ref[r, :] for r in q_rows], axis=0)

            def store(ref, val):
                for j, r in enumerate(q_rows):
                    ref[r, :] = val[j * pr:(j + 1) * pr]

            if pi == 0:
                store(acc, o_t)
                store(mrun, m_t)
                store(lrun, l_t)
            else:
                m_o = load(mrun)
                m_n = jnp.maximum(m_o, m_t)
                a_o = jnp.exp(m_o - m_n)
                a_t = jnp.exp(m_t - m_n)
                acc_n = a_o * load(acc) + a_t * o_t
                l_n = a_o * load(lrun) + a_t * l_t
                if pi == len(order) - 1:
                    assert n_piece == 1
                    o_ref[0, 0, q_rows[0], :] = (acc_n / l_n).astype(o_ref.dtype)
                else:
                    store(acc, acc_n)
                    store(mrun, m_n)
                    store(lrun, l_n)
            return carry

        lax.fori_loop(0, dil * nblk, block, 0, unroll=ATTN_UNROLL)


def _dilated_attention(q, kv):
    b, n_lt, t, _ = q.shape
    for window, dil in DILATED_PATTERNS:
        assert window // dil == ATTN_BLOCK and t % (dil * 2 * ATTN_BLOCK) == 0 and ATTN_RESIDUES % dil == 0
    assert max(dl for _, dl in DILATED_PATTERNS) == ATTN_RESIDUES
    spec = pl.BlockSpec((1, 1, t, LANES), lambda i, j: (i, j, 0, 0))
    return pl.pallas_call(
        _dilated_attn_kernel,
        grid=(b, n_lt),
        in_specs=[spec, spec],
        out_specs=spec,
        out_shape=jax.ShapeDtypeStruct((b, n_lt, t, LANES), BF16),
        scratch_shapes=[pltpu.VMEM((t, LANES), F32) for _ in range(3)],
        compiler_params=_cparams("arbitrary", "arbitrary"),
    )(q, kv)


def _decode_kernel(qt_ref, kt_ref, vt_ref, ck_ref, cv_ref, ok_ref, ov_ref, at_ref, *, hb):
    i = pl.program_id(0)
    hg = pl.program_id(1)
    w = ck_ref.shape[3]
    bs = qt_ref.shape[1]
    sel = lax.broadcasted_iota(jnp.int32, (HEAD_DIM, bs), 1) == i

    def column(ref, rs):
        return jnp.sum(jnp.where(sel, ref[rs, :], 0.0), axis=1, keepdims=True)

    pos = lax.broadcasted_iota(jnp.int32, (1, w), 1)
    dist = w - pos
    cnt = jnp.zeros((1, w), F32)
    for window, dil in DILATED_PATTERNS:
        cnt = cnt + jnp.where((dist & (dil - 1)) == 0, jnp.where(dist <= window, 1.0, 0.0), 0.0)
    n_pat = float(len(DILATED_PATTERNS))
    last = lax.broadcasted_iota(jnp.int32, (HEAD_DIM, w), 1) == w - 1

    @pl.when((i == 0) & (hg == 0))
    def _():
        at_ref[...] = jnp.zeros(at_ref.shape, F32)

    for h in range(hb):
        rs = pl.ds(pl.multiple_of((hg * hb + h) * HEAD_DIM, HEAD_DIM), HEAD_DIM)
        qc, kc, vc = column(qt_ref, rs), column(kt_ref, rs), column(vt_ref, rs)
        kk = ck_ref[0, h]
        vv = cv_ref[0, h]
        s = jnp.sum(kk * qc, axis=0, keepdims=True)
        s_new = jnp.sum(kc * qc, axis=0, keepdims=True)
        s = jnp.where(cnt > 0.0, s, NEG_BIG)
        m = jnp.maximum(jnp.max(s, axis=1, keepdims=True), s_new)
        p = cnt * jnp.exp(s - m)
        p_new = n_pat * jnp.exp(s_new - m)
        l = jnp.sum(p, axis=1, keepdims=True) + p_new
        o = (jnp.sum(vv * p, axis=1, keepdims=True) + p_new * vc) / l
        at_ref[rs, :] = jnp.where(sel, o, at_ref[rs, :])
        ok_ref[0, h] = jnp.where(last, kc, pltpu.roll(kk, w - 1, 1))
        ov_ref[0, h] = jnp.where(last, vc, pltpu.roll(vv, w - 1, 1))


def _decode_attention(qt, kt, vt, ck, cv):
    bs, n_heads, hd, w = ck.shape
    aw = n_heads * hd
    for window, dil in DILATED_PATTERNS:
        assert window <= w and dil & (dil - 1) == 0
    hb = DECODE_HEADS
    full = pl.BlockSpec((aw, bs), lambda i, j: (0, 0))
    buf = pl.BlockSpec((1, hb, hd, w), lambda i, j: (i, j, 0, 0))
    return pl.pallas_call(
        functools.partial(_decode_kernel, hb=hb),
        grid=(bs, n_heads // hb),
        in_specs=[full, full, full, buf, buf],
        out_specs=[buf, buf, full],
        out_shape=[jax.ShapeDtypeStruct(ck.shape, F32), jax.ShapeDtypeStruct(cv.shape, F32),
                   jax.ShapeDtypeStruct((aw, bs), F32)],
        compiler_params=_cparams("arbitrary", "arbitrary"),
    )(qt, kt, vt, ck, cv)


def _s5_discretise(lam_re, lam_im, log_dt, b_re, b_im):
    dt = jnp.exp(log_dt)[:, None]
    mag = jnp.exp(lam_re * dt)
    ar = mag * jnp.cos(lam_im * dt)
    ai = mag * jnp.sin(lam_im * dt)
    den = lam_re * lam_re + lam_im * lam_im
    fr = ((ar - 1.0) * lam_re + ai * lam_im) / den
    fi = (ai * lam_re - (ar - 1.0) * lam_im) / den
    bbr = fr[..., None] * b_re - fi[..., None] * b_im
    bbi = fr[..., None] * b_im + fi[..., None] * b_re
    return dt, ar, ai, bbr, bbi


def _s5_chunk_matrices(lam_re, lam_im, log_dt, b_re, b_im, c_re, c_im):
    lc = SSM_CHUNK
    g, p, c = b_re.shape
    dt, _, _, bbr, bbi = _s5_discretise(lam_re, lam_im, log_dt, b_re, b_im)
    kk = jnp.arange(lc + 1, dtype=F32)[:, None, None]
    mag = jnp.exp(kk * lam_re * dt)
    apr = mag * jnp.cos(kk * lam_im * dt)
    api = mag * jnp.sin(kk * lam_im * dt)
    akb_r = apr[:lc, :, :, None] * bbr - api[:lc, :, :, None] * bbi
    akb_i = apr[:lc, :, :, None] * bbi + api[:lc, :, :, None] * bbr
    kern = (jnp.einsum('gop,kgpc->kgoc', c_re, akb_r, precision=HIGHEST)
            - jnp.einsum('gop,kgpc->kgoc', c_im, akb_i, precision=HIGHEST))
    ii = jnp.arange(lc)
    lag = ii[None, :] - ii[:, None]
    toe = jnp.where((lag >= 0)[:, :, None, None, None], kern[jnp.clip(lag, 0, lc - 1)], 0.0)
    toe = toe.transpose(2, 0, 4, 1, 3).reshape(g, lc * c, lc * c)
    rev = lc - 1 - ii
    ws_r = akb_r[rev].transpose(1, 0, 3, 2).reshape(g, lc * c, p)
    ws_i = akb_i[rev].transpose(1, 0, 3, 2).reshape(g, lc * c, p)
    a1r, a1i = apr[1:], api[1:]
    ca_r = c_re[None] * a1r[:, :, None, :] - c_im[None] * a1i[:, :, None, :]
    ca_i = c_re[None] * a1i[:, :, None, :] + c_im[None] * a1r[:, :, None, :]
    wy_r = ca_r.transpose(1, 3, 0, 2).reshape(g, p, lc * c)
    wy_i = (-ca_i).transpose(1, 3, 0, 2).reshape(g, p, lc * c)

    def pair_diag(m):
        r, s = m.shape[1:]
        m2 = m.reshape(g // 2, 2, r, s)
        z = jnp.zeros((g // 2, r, s), m.dtype)
        top = jnp.concatenate([m2[:, 0], z], axis=2)
        bot = jnp.concatenate([z, m2[:, 1]], axis=2)
        return jnp.concatenate([top, bot], axis=1)

    alc_r = apr[lc].reshape(1, g * p)
    alc_i = api[lc].reshape(1, g * p)
    return (toe.astype(BF16), pair_diag(ws_r).astype(BF16), pair_diag(ws_i).astype(BF16),
            pair_diag(wy_r).astype(BF16), pair_diag(wy_i).astype(BF16), alc_r, alc_i)


def _s5_step_matrices(lam_re, lam_im, log_dt, b_re, b_im, c_re, c_im):
    g, p, c = b_re.shape
    _, ar, ai, bbr, bbi = _s5_discretise(lam_re, lam_im, log_dt, b_re, b_im)
    eye = jnp.eye(g, dtype=F32)
    bd_br = jnp.einsum('gpc,gh->gchp', bbr, eye).reshape(g * c, g * p)
    bd_bi = jnp.einsum('gpc,gh->gchp', bbi, eye).reshape(g * c, g * p)
    bd_cr = jnp.einsum('gcp,gh->gphc', c_re, eye).reshape(g * p, g * c)
    bd_ci = jnp.einsum('gcp,gh->gphc', c_im, eye).reshape(g * p, g * c)
    return bd_br, bd_bi, bd_cr, bd_ci, ar.reshape(1, g * p), ai.reshape(1, g * p)


def _s5_prompt_kernel(u_ref, toe_ref, wsr_ref, wsi_ref, wyr_ref, wyi_ref, ar_ref, ai_ref,
                      y_ref, hr_ref, hi_ref, uperm, yperm, sr, si, hr_hist, hi_hist, hcar_r, hcar_i):
    j = pl.program_id(1)
    nc = uperm.shape[0]
    lc, gc = SSM_CHUNK, SSM_GROUP
    gpt = LANES // gc
    n_lt = u_ref.shape[1]
    rb = ROW_TILE // lc
    gw = lc * gc
    lane_grp = lax.broadcasted_iota(jnp.int32, (1, LANES), 1) // gc

    def group_transpose(vs):
        d = 1
        while d < gpt:
            hi = (lane_grp & d) != 0
            nxt = list(vs)
            for a in range(gpt):
                if a & d == 0:
                    nxt[a] = jnp.where(hi, pltpu.roll(vs[a + d], d * gc, 1), vs[a])
                    nxt[a + d] = jnp.where(hi, vs[a + d], pltpu.roll(vs[a], LANES - d * gc, 1))
            vs = nxt
            d *= 2
        return vs

    def permute_in(rc, carry):
        r0 = pl.multiple_of(rc * rb, rb)
        for lt in range(n_lt):
            for h in range(lc // gpt):
                xs = [u_ref[0, lt, pl.ds(pl.multiple_of(r0 * lc + (h * gpt + il) * rb, rb), rb), :].astype(F32)
                      for il in range(gpt)]
                for gl, d in enumerate(group_transpose(xs)):
                    c0 = (lt * gpt + gl) * gw + h * LANES
                    uperm[pl.ds(r0, rb), c0:c0 + LANES] = d.astype(BF16)
        return carry

    lax.fori_loop(0, nc // rb, permute_in, 0)

    n_pairs = wsr_ref.shape[0]
    pw = wsr_ref.shape[1]
    sw = wsr_ref.shape[2]
    for pr in range(n_pairs):
        up = uperm[:, pr * pw:(pr + 1) * pw]
        sr[:, pr * sw:(pr + 1) * sw] = jnp.dot(up, wsr_ref[pr], preferred_element_type=F32)
        si[:, pr * sw:(pr + 1) * sw] = jnp.dot(up, wsi_ref[pr], preferred_element_type=F32)

    ar = ar_ref[...]
    ai = ai_ref[...]
    nstate = ar.shape[1]
    rowid = lax.broadcasted_iota(jnp.int32, (8, nstate), 0)

    @pl.when(j == 0)
    def _():
        hcar_r[...] = jnp.zeros(hcar_r.shape, F32)
        hcar_i[...] = jnp.zeros(hcar_i.shape, F32)

    def eight_chunks(c8, carry):
        h_r, h_i = carry
        base = pl.multiple_of(c8 * 8, 8)
        s_r8 = sr[pl.ds(base, 8), :]
        s_i8 = si[pl.ds(base, 8), :]
        t_r = jnp.zeros((8, nstate), F32)
        t_i = jnp.zeros((8, nstate), F32)
        for r in range(8):
            t_r = jnp.where(rowid == r, h_r, t_r)
            t_i = jnp.where(rowid == r, h_i, t_i)
            n_r = ar * h_r - ai * h_i + s_r8[r:r + 1, :]
            n_i = ar * h_i + ai * h_r + s_i8[r:r + 1, :]
            h_r, h_i = n_r, n_i
        hr_hist[pl.ds(base, 8), :] = t_r
        hi_hist[pl.ds(base, 8), :] = t_i
        return h_r, h_i

    h_r, h_i = lax.fori_loop(0, nc // 8, eight_chunks, (hcar_r[...], hcar_i[...]))
    hcar_r[...] = h_r
    hcar_i[...] = h_i
    hr_ref[0] = h_r
    hi_ref[0] = h_i

    for pr in range(n_pairs):
        st = slice(pr * sw, (pr + 1) * sw)
        y2 = (jnp.dot(hr_hist[:, st].astype(BF16), wyr_ref[pr], preferred_element_type=F32)
              + jnp.dot(hi_hist[:, st].astype(BF16), wyi_ref[pr], preferred_element_type=F32))
        for gg in range(2):
            g = pr * 2 + gg
            cols = slice(g * gw, (g + 1) * gw)
            yi = jnp.dot(uperm[:, cols], toe_ref[g], preferred_element_type=F32)
            yperm[:, cols] = yi + y2[:, gg * gw:(gg + 1) * gw]

    def permute_out(rc, carry):
        r0 = pl.multiple_of(rc * rb, rb)
        for lt in range(n_lt):
            for h in range(lc // gpt):
                ds_ = [yperm[pl.ds(r0, rb), (lt * gpt + gl) * gw + h * LANES:(lt * gpt + gl) * gw + (h + 1) * LANES]
                       for gl in range(gpt)]
                for il, yv in enumerate(group_transpose(ds_)):
                    y_ref[0, lt, pl.ds(pl.multiple_of(r0 * lc + (h * gpt + il) * rb, rb), rb), :] = yv.astype(y_ref.dtype)
        return carry

    lax.fori_loop(0, nc // rb, permute_out, 0)


def _s5_prompt(u, mats):
    b, n_lt, t, _ = u.shape
    wid = n_lt * LANES
    nstate = mats[-1].shape[1]
    ts = min(t, S5_SLAB)
    nc = ts // SSM_CHUNK
    assert t % ts == 0 and ts % ROW_TILE == 0
    full = lambda a: pl.BlockSpec(a.shape, lambda i, j: (0,) * a.ndim)
    return pl.pallas_call(
        _s5_prompt_kernel,
        grid=(b, t // ts),
        in_specs=[pl.BlockSpec((1, n_lt, ts, LANES), lambda i, j: (i, 0, j, 0))] + [full(a) for a in mats],
        out_specs=[pl.BlockSpec((1, n_lt, ts, LANES), lambda i, j: (i, 0, j, 0)),
                   pl.BlockSpec((1, 1, nstate), lambda i, j: (i, 0, 0)),
                   pl.BlockSpec((1, 1, nstate), lambda i, j: (i, 0, 0))],
        out_shape=[jax.ShapeDtypeStruct(u.shape, BF16),
                   jax.ShapeDtypeStruct((b, 1, nstate), F32),
                   jax.ShapeDtypeStruct((b, 1, nstate), F32)],
        scratch_shapes=[pltpu.VMEM((nc, wid * SSM_CHUNK), BF16), pltpu.VMEM((nc, wid * SSM_CHUNK), F32)]
        + [pltpu.VMEM((nc, nstate), F32) for _ in range(4)]
        + [pltpu.VMEM((1, nstate), F32) for _ in range(2)],
        compiler_params=_cparams("arbitrary", "arbitrary"),
    )(u, *mats)


def _s5_step_kernel(u_ref, h0r_ref, h0i_ref, bbr_ref, bbi_ref, cr_ref, ci_ref, ar_ref, ai_ref,
                    y_ref, xr_ref, xi_ref):
    u = u_ref[...]
    ar = ar_ref[...]
    ai = ai_ref[...]
    h0r = h0r_ref[...]
    h0i = h0i_ref[...]
    xr = ar * h0r - ai * h0i + jnp.dot(u, bbr_ref[...], precision=HIGHEST, preferred_element_type=F32)
    xi = ar * h0i + ai * h0r + jnp.dot(u, bbi_ref[...], precision=HIGHEST, preferred_element_type=F32)
    xr_ref[...] = xr
    xi_ref[...] = xi
    y_ref[...] = (jnp.dot(xr, cr_ref[...], precision=HIGHEST, preferred_element_type=F32)
                  - jnp.dot(xi, ci_ref[...], precision=HIGHEST, preferred_element_type=F32))


def _s5_step(u, h0r, h0i, mats):
    n, wid = u.shape
    nstate = h0r.shape[1]
    return pl.pallas_call(
        _s5_step_kernel,
        out_shape=[jax.ShapeDtypeStruct((n, wid), F32),
                   jax.ShapeDtypeStruct((n, nstate), F32),
                   jax.ShapeDtypeStruct((n, nstate), F32)],
        compiler_params=pltpu.CompilerParams(vmem_limit_bytes=VMEM_LIMIT),
    )(u, h0r, h0i, *mats)


def _split_bf16(a):
    hi = a.astype(BF16)
    return hi, (a - hi.astype(F32)).astype(BF16)


def _mix_kernel(x_ref, mod_ref, attn_ref, ys_ref, u_ref, d_ref, wglu_ref, bglu_ref, ga_ref, gs_ref,
                wout_ref, g2_ref, wr_ref, br_ref, ungroup_ref, below_ref,
                x1_ref, h2_ref, gate_ref, idx_ref, rank_ref, count_ref, seen, *, lane_tiled):
    def rows(ref):
        if lane_tiled:
            tile = lambda i: ref[0, i].reshape(-1, LANES)
            return jnp.concatenate([tile(i) for i in range(ref.shape[1])], axis=1).astype(F32)
        return ref[...]

    a_n = (_rms(rows(attn_ref)) * ga_ref[...]).astype(BF16)

    y = rows(ys_ref) + d_ref[...] * rows(u_ref)
    y = 0.5 * y * (1.0 + jnp.tanh(np.float32(np.sqrt(2.0 / np.pi)) * (y + 0.044715 * (y * y * y))))
    z = jnp.dot(y.astype(BF16), wglu_ref[...], preferred_element_type=F32) + bglu_ref[...]
    ssm = y * jax.nn.sigmoid(z)
    s_n = (_rms(ssm) * gs_ref[...]).astype(BF16)
    both = jnp.concatenate([a_n, s_n], axis=1)
    if lane_tiled:
        both = jnp.dot(ungroup_ref[...], both, preferred_element_type=F32).astype(BF16)
    mixed = jnp.dot(both, wout_ref[...], preferred_element_type=F32)
    x1 = x_ref[...] + mod_ref[0, 2] * mixed
    x1_ref[...] = x1
    h2 = _rms(x1) * g2_ref[...] * (1.0 + mod_ref[0, 4]) + mod_ref[0, 3]
    h2_ref[...] = h2.astype(BF16)

    h_hi, h_lo = _split_bf16(h2)
    hh = jnp.dot(h_hi, wr_ref[...], preferred_element_type=F32)
    logits = (hh[:, :LANES] + hh[:, LANES:]
              + jnp.dot(h_lo, wr_ref[:, 0:LANES], preferred_element_type=F32)) + br_ref[...]
    lane = lax.broadcasted_iota(jnp.int32, logits.shape, 1)
    lane_f = lane.astype(F32)
    cur = logits
    vals, idxs = [], []
    for _ in range(TOP_K):
        mx = jnp.max(cur, axis=1, keepdims=True)
        ix = jnp.min(jnp.where(cur == mx, lane_f, float(LANES)), axis=1, keepdims=True)
        vals.append(mx)
        idxs.append(ix)
        cur = jnp.where(lane_f == ix, -jnp.inf, cur)
    exps = [jnp.exp(v - vals[0]) for v in vals]
    den = exps[0]
    for e in exps[1:]:
        den = den + e
    gate = jnp.zeros(logits.shape, F32)
    idx = jnp.zeros(logits.shape, F32)
    for k in range(TOP_K):
        gate = jnp.where(lane == k, exps[k] / den, gate)
        idx = jnp.where(lane == k, idxs[k], idx)
    gate_ref[...] = gate
    idx_ref[...] = idx.astype(jnp.int32)

    @pl.when(pl.program_id(0) == 0)
    def _():
        seen[...] = jnp.zeros(seen.shape, F32)

    chosen = jnp.zeros(logits.shape, F32)
    for k in range(TOP_K):
        chosen = jnp.where(lane_f == idxs[k], 1.0, chosen)
    earlier = jnp.dot(below_ref[...], chosen.astype(BF16), preferred_element_type=F32) + seen[...]
    rank = jnp.zeros(logits.shape, F32)
    for k in range(TOP_K):
        r_k = jnp.sum(jnp.where(lane_f == idxs[k], earlier, 0.0), axis=1, keepdims=True)
        rank = jnp.where(lane == k, r_k, rank)
    rank_ref[...] = rank.astype(jnp.int32)
    seen[...] += jnp.sum(chosen, axis=0, keepdims=True)
    count_ref[...] = seen[...].astype(jnp.int32)


def _mix(x2d, mod, attn, ys, u, weights, tm, rows_per_mod):
    n, d = x2d.shape
    r = mod.shape[2]
    tiles_per_mod = max(rows_per_mod // tm, 1)
    mod_map = (lambda i: (i // tiles_per_mod, 0, 0, 0)) if r == 1 else (lambda i: (0, 0, i, 0))
    rowspec = lambda w: pl.BlockSpec((tm, w), lambda i: (i, 0))
    full = lambda a: pl.BlockSpec(a.shape, lambda i: (0,) * a.ndim)
    lane_tiled = attn.ndim == 5
    below = (jnp.arange(tm)[:, None] > jnp.arange(tm)[None, :]).astype(BF16)
    weights = tuple(weights) + (_regroup_matrix(tm).T, below)
    if lane_tiled:
        def act_spec(a):
            if a.ndim == 5:
                return pl.BlockSpec((1, a.shape[1], a.shape[2], tm // a.shape[2], LANES),
                                    lambda i: (i // tiles_per_mod, 0, 0, i % tiles_per_mod, 0))
            return pl.BlockSpec((1, a.shape[1], tm, LANES), lambda i: (i // tiles_per_mod, 0, i % tiles_per_mod, 0))
    else:
        act_spec = lambda a: rowspec(a.shape[1])
    return pl.pallas_call(
        functools.partial(_mix_kernel, lane_tiled=lane_tiled),
        grid=(n // tm,),
        in_specs=[rowspec(d), pl.BlockSpec((1, 6, r, d), mod_map), act_spec(attn), act_spec(ys), act_spec(u)]
        + [full(w) for w in weights],
        out_specs=[rowspec(d), rowspec(d), rowspec(LANES), rowspec(LANES), rowspec(LANES),
                   pl.BlockSpec((1, LANES), lambda i: (0, 0))],
        out_shape=[jax.ShapeDtypeStruct((n, d), F32), jax.ShapeDtypeStruct((n, d), BF16),
                   jax.ShapeDtypeStruct((n, LANES), F32), jax.ShapeDtypeStruct((n, LANES), jnp.int32),
                   jax.ShapeDtypeStruct((n, LANES), jnp.int32), jax.ShapeDtypeStruct((1, LANES), jnp.int32)],
        scratch_shapes=[pltpu.VMEM((1, LANES), F32)],
        compiler_params=_cparams("arbitrary"),
    )(x2d, mod, attn, ys, u, *weights)


def _moe_kernel(be_ref, first_ref, nb_ref, x_ref, wg_ref, bg_ref, wu_ref, bu_ref, wd_ref, bd_ref, o_ref,
                wg_bf, wu_bf, wd_bf):
    i = pl.program_id(0)

    @pl.when(first_ref[i] == 1)
    def _():
        wg_bf[...] = wg_ref[0].astype(BF16)
        wu_bf[...] = wu_ref[0].astype(BF16)
        wd_bf[...] = wd_ref[0].astype(BF16)

    @pl.when(i < nb_ref[0])
    def _():
        x = x_ref[...]
        g = jnp.dot(x, wg_bf[...], preferred_element_type=F32) + bg_ref[0]
        up = jnp.dot(x, wu_bf[...], preferred_element_type=F32) + bu_ref[0]
        g = jnp.minimum(g, SWIGLU_LIMIT)
        up = jnp.clip(up, -SWIGLU_LIMIT, SWIGLU_LIMIT)
        hid = (up + 1.0) * g * jax.nn.sigmoid(SWIGLU_ALPHA * g)
        y = jnp.dot(hid.astype(BF16), wd_bf[...], preferred_element_type=F32) + bd_ref[0]
        o_ref[...] = y.astype(o_ref.dtype)

    @pl.when(i >= nb_ref[0])
    def _():
        o_ref[...] = jnp.zeros(o_ref.shape, o_ref.dtype)


def _moe_experts(xb, block_e, first, n_used, wg, bg, wu, bu, wd, bd):
    n_rows, d = xb.shape
    e, _, f = wg.shape
    nb = n_rows // MOE_BLOCK
    wmap = lambda i, be, fi, nu: (be[i], 0, 0)
    grid_spec = pltpu.PrefetchScalarGridSpec(
        num_scalar_prefetch=3,
        grid=(nb,),
        in_specs=[pl.BlockSpec((MOE_BLOCK, d), lambda i, be, fi, nu: (i, 0)),
                  pl.BlockSpec((1, d, f), wmap), pl.BlockSpec((1, 1, f), wmap),
                  pl.BlockSpec((1, d, f), wmap), pl.BlockSpec((1, 1, f), wmap),
                  pl.BlockSpec((1, f, d), wmap), pl.BlockSpec((1, 1, d), wmap)],
        out_specs=pl.BlockSpec((MOE_BLOCK, d), lambda i, be, fi, nu: (i, 0)),
        scratch_shapes=[pltpu.VMEM((d, f), BF16), pltpu.VMEM((d, f), BF16), pltpu.VMEM((f, d), BF16)],
    )
    return pl.pallas_call(
        _moe_kernel,
        grid_spec=grid_spec,
        out_shape=jax.ShapeDtypeStruct((n_rows, d), BF16),
        compiler_params=_cparams("arbitrary"),
    )(block_e, first, n_used, xb, wg, bg.reshape(e, 1, f), wu, bu.reshape(e, 1, f), wd, bd.reshape(e, 1, d))


def _final_kernel(x1_ref, mod_ref, gate_ref, *rest):
    yg_refs, (gf_ref, o_ref) = rest[:-2], rest[-2:]
    gate = gate_ref[...]
    acc = jnp.zeros(x1_ref.shape, F32)
    k = 0
    for yg_ref in yg_refs:
        for j in range(yg_ref.shape[0]):
            acc = acc + gate[:, k:k + 1] * yg_ref[j].astype(F32)
            k += 1
    x = x1_ref[...] + mod_ref[0, 5] * acc
    o_ref[...] = _rms(x) * gf_ref[...]


def _final(x1, mod, gates, yg_parts, g_final, tm, rows_per_mod):
    n, d = x1.shape
    tiles_per_mod = rows_per_mod // tm
    assert sum(p.shape[0] for p in yg_parts) == TOP_K
    return pl.pallas_call(
        _final_kernel,
        grid=(n // tm,),
        in_specs=[pl.BlockSpec((tm, d), lambda i: (i, 0)),
                  pl.BlockSpec((1, 6, 1, d), lambda i: (i // tiles_per_mod, 0, 0, 0)),
                  pl.BlockSpec((tm, LANES), lambda i: (i, 0))]
        + [pl.BlockSpec((p.shape[0], tm, d), lambda i: (0, i, 0)) for p in yg_parts]
        + [pl.BlockSpec((1, d), lambda i: (0, 0))],
        out_specs=pl.BlockSpec((tm, d), lambda i: (i, 0)),
        out_shape=jax.ShapeDtypeStruct((n, d), F32),
        compiler_params=_cparams("arbitrary"),
    )(x1, mod, gates, *yg_parts, g_final.reshape(1, d))


def _moe_decode_kernel(h2_ref, gate_ref, idx_ref, x1_ref, mod_ref, gf_ref,
                       wg_ref, bg_ref, wu_ref, bu_ref, wd_ref, bd_ref, o_ref, acc):
    e = pl.program_id(0)

    @pl.when(e == 0)
    def _():
        acc[...] = jnp.zeros(acc.shape, F32)

    gate_e = jnp.sum(jnp.where(idx_ref[...] == e, gate_ref[...], 0.0), axis=1, keepdims=True)
    x = h2_ref[...]
    g = jnp.dot(x, wg_ref[0].astype(BF16), preferred_element_type=F32) + bg_ref[0]
    up = jnp.dot(x, wu_ref[0].astype(BF16), preferred_element_type=F32) + bu_ref[0]
    g = jnp.minimum(g, SWIGLU_LIMIT)
    up = jnp.clip(up, -SWIGLU_LIMIT, SWIGLU_LIMIT)
    hid = (up + 1.0) * g * jax.nn.sigmoid(SWIGLU_ALPHA * g)
    y = jnp.dot(hid.astype(BF16), wd_ref[0].astype(BF16), preferred_element_type=F32) + bd_ref[0]
    acc[...] += gate_e * y

    @pl.when(e == pl.num_programs(0) - 1)
    def _():
        x1 = x1_ref[...] + mod_ref[0, 5] * acc[...]
        o_ref[...] = _rms(x1) * gf_ref[...]


def _moe_decode(h2, gates, idx, x1, mod, g_final, wg, bg, wu, bu, wd, bd):
    n, d = x1.shape
    e, _, f = wg.shape
    full = lambda a: pl.BlockSpec(a.shape, lambda i: (0,) * a.ndim)
    wmap = lambda i: (i, 0, 0)
    return pl.pallas_call(
        _moe_decode_kernel,
        grid=(e,),
        in_specs=[full(h2), full(gates), full(idx), full(x1), full(mod), pl.BlockSpec((1, d), lambda i: (0, 0)),
                  pl.BlockSpec((1, d, f), wmap), pl.BlockSpec((1, 1, f), wmap),
                  pl.BlockSpec((1, d, f), wmap), pl.BlockSpec((1, 1, f), wmap),
                  pl.BlockSpec((1, f, d), wmap), pl.BlockSpec((1, 1, d), wmap)],
        out_specs=pl.BlockSpec((n, d), lambda i: (0, 0)),
        out_shape=jax.ShapeDtypeStruct((n, d), F32),
        scratch_shapes=[pltpu.VMEM((n, d), F32)],
        compiler_params=_cparams("arbitrary"),
    )(h2, gates, idx, x1, mod, g_final.reshape(1, d), wg, bg.reshape(e, 1, f), wu, bu.reshape(e, 1, f),
      wd, bd.reshape(e, 1, d))


def _dispatch(top_e, rank, counts):
    n_exp = counts.shape[0]
    n_assign = top_e.shape[0] * TOP_K
    padded = (counts + MOE_BLOCK - 1) // MOE_BLOCK * MOE_BLOCK
    pend = jnp.cumsum(padded)
    pstart = pend - padded
    dest = pstart[top_e] + rank
    nb = -(-n_assign // MOE_BLOCK) + n_exp
    block_start = jnp.arange(nb, dtype=jnp.int32) * MOE_BLOCK
    block_e = jnp.minimum(jnp.sum((pend[None, :] <= block_start[:, None]).astype(jnp.int32), axis=1), n_exp - 1)
    bits = max(n_assign - 1, 1).bit_length()
    assert (n_exp << bits) < 2 ** 31
    packed = (top_e.reshape(-1) << bits) | jnp.arange(n_assign, dtype=jnp.int32)
    order = jnp.sort(packed) & ((1 << bits) - 1)
    ustart = jnp.cumsum(counts) - counts
    first_src = ustart[block_e] + block_start - pstart[block_e]
    last_src = ustart[block_e] + counts[block_e] - 1
    src = jnp.minimum(first_src[:, None] + jnp.arange(MOE_BLOCK, dtype=jnp.int32)[None, :], last_src[:, None])
    row_tok = order[jnp.clip(src.reshape(-1), 0, n_assign - 1)] // TOP_K
    first = jnp.concatenate([jnp.ones((1,), jnp.int32), (block_e[1:] != block_e[:-1]).astype(jnp.int32)])
    n_used = (pend[-1] // MOE_BLOCK).astype(jnp.int32).reshape(1)
    return row_tok, dest, block_e, first, n_used


def kernel(x_prompt, x_sample, c_prompt, c_sample, cache_k, cache_v, state_ssm_re, state_ssm_im, w_ada, b_ada, g_norm1, g_norm2, w_in, lambda_re, lambda_im, log_dt, b_ssm_re, b_ssm_im, c_ssm_re, c_ssm_im, d_ssm, w_glu, b_glu, g_attn_out, g_ssm_out, w_out, w_router, b_router, w_gate, b_gate, w_up, b_up, w_down, b_down, g_final):
    assert w_ada.shape[0] == 1, "one layer"
    b, t, d = x_prompt.shape
    bs = x_sample.shape[0]
    assert x_sample.shape[1] == 1
    wbuf, n_heads = cache_k.shape[2], cache_k.shape[3]
    aw = n_heads * HEAD_DIM
    n_groups, n_state = lambda_re.shape[1:]
    sw = n_groups * SSM_GROUP
    n_exp = w_router.shape[2]
    keep = min(max(w for w, _ in DILATED_PATTERNS), t)
    tm = min(ROW_TILE, t)

    mod = _ada(jnp.concatenate([c_prompt, c_sample], axis=0), w_ada[0], b_ada[0])
    mod_p = mod[:b].reshape(b, 6, 1, d)
    mod_s = mod[b:].reshape(bs, 6, d).transpose(1, 0, 2)[None]

    w_in_bf = w_in[0].astype(BF16)
    g1 = g_norm1[0].reshape(1, d)
    ssm_params = (lambda_re[0], lambda_im[0], log_dt[0], b_ssm_re[0], b_ssm_im[0], c_ssm_re[0], c_ssm_im[0])
    wr_pad = jnp.zeros((d, LANES), F32).at[:, :n_exp].set(w_router[0])
    wr_hi = wr_pad.astype(BF16)
    wr_lo = (wr_pad - wr_hi.astype(F32)).astype(BF16)
    br_pad = jnp.full((1, LANES), NEG_BIG, F32).at[0, :n_exp].set(b_router[0])
    mix_w = (d_ssm[0].reshape(1, sw), w_glu[0].astype(BF16), b_glu[0].reshape(1, sw),
             g_attn_out[0].reshape(1, aw), g_ssm_out[0].reshape(1, sw), w_out[0].astype(BF16),
             g_norm2[0].reshape(1, d), jnp.concatenate([wr_hi, wr_lo], axis=1), br_pad)
    experts = (w_gate[0], b_gate[0], w_up[0], b_up[0], w_down[0], b_down[0])

    cos_p, sin_p = _rope_tables(jnp.arange(t), n_heads)
    q_p, kv_p, kt_p, vt_p, u_p = _inproj_prompt(x_prompt, mod_p, g1, w_in_bf, cos_p, sin_p, tm)
    flat_t = lambda a: a.reshape(b, a.shape[1], t, LANES)
    attn_p = _dilated_attention(flat_t(q_p), flat_t(kv_p)).reshape(q_p.shape)
    ys_p, hr_p, hi_p = _s5_prompt(u_p, _s5_chunk_matrices(*ssm_params))
    x1_p, h2_p, gate_p, idx_p, rank_p, count_p = _mix(x_prompt.reshape(b * t, d), mod_p, attn_p, ys_p, u_p,
                                                      mix_w, tm, t)

    row_tok, dest, block_e, first, n_used = _dispatch(idx_p[:, :TOP_K], rank_p[:, :TOP_K], count_p[0, :n_exp])
    xb = h2_p.at[row_tok].get(mode="promise_in_bounds")

    cos_s, sin_s = _rope_tables(jnp.full((1,), PAST_LEN), n_heads)
    qt_s, kt_s, vt_s, u_s = _inproj_decode(x_sample.reshape(bs, d), mod_s, g1, w_in_bf, cos_s, sin_s)
    to_hdp = lambda c: jnp.transpose(c[0], (0, 2, 3, 1))
    from_hdp = lambda c: jnp.transpose(c, (0, 3, 1, 2))[None]
    ck_new, cv_new, attn_t = _decode_attention(qt_s, kt_s, vt_s, to_hdp(cache_k), to_hdp(cache_v))
    ys_s, hr_s, hi_s = _s5_step(u_s, state_ssm_re[0].reshape(bs, n_groups * n_state),
                                state_ssm_im[0].reshape(bs, n_groups * n_state),
                                _s5_step_matrices(*ssm_params))
    x1_s, h2_s, gate_s, idx_s, _, _ = _mix(x_sample.reshape(bs, d), mod_s, attn_t.T, ys_s, u_s, mix_w, bs, 1)

    yb = _moe_experts(xb, block_e, first, n_used, *experts)
    half = TOP_K // 2
    yg = [yb.at[dest[:, :half].T].get(mode="promise_in_bounds"),
          yb.at[dest[:, half:].T].get(mode="promise_in_bounds")]
    y_sample = _moe_decode(h2_s, gate_s, idx_s, x1_s, mod_s, g_final, *experts).reshape(bs, 1, d)
    y_prompt = _final(x1_p, mod_p, gate_p, yg, g_final, FINAL_TILE, t).reshape(b, t, d)

    k_win_p = from_hdp(kt_p[:, :, t - keep:].reshape(b, n_heads, HEAD_DIM, keep))
    v_win_p = from_hdp(vt_p[:, :, t - keep:].reshape(b, n_heads, HEAD_DIM, keep))
    st = lambda a, n: a.reshape(1, n, n_groups, n_state)
    return (y_prompt, y_sample, k_win_p, v_win_p, st(hr_p, b), st(hi_p, b),
            from_hdp(ck_new), from_hdp(cv_new), st(hr_s, bs), st(hi_s, bs))
```

```python
import functools

import jax
import jax.numpy as jnp
import numpy as np
from jax import lax
from jax.experimental import pallas as pl
from jax.experimental.pallas import tpu as pltpu

F32 = jnp.float32
BF16 = jnp.bfloat16
HIGHEST = lax.Precision.HIGHEST

HEAD_DIM = 64
DILATED_PATTERNS = ((128, 1), (512, 4), (2048, 16))
ROPE_THETA = 10000.0
PAST_LEN = 8192
SSM_GROUP = 16
SSM_STATE = 64
TOP_K = 4
SWIGLU_LIMIT = 7.0
SWIGLU_ALPHA = 1.702
RMS_EPS = 1e-6

LANES = 128
HEADS_PER_LANE_TILE = LANES // HEAD_DIM
ATTN_BLOCK = 128
ATTN_UNROLL = 32
ATTN_RESIDUES = 16
DECODE_HEADS = 8
SSM_CHUNK = 16
S5_SLAB = 2048
ROW_TILE = 512
MOE_BLOCK = 512
FINAL_TILE = 1024
VMEM_LIMIT = 52 * 1024 * 1024
NEG_BIG = -1e30


def _cparams(*sem):
    return pltpu.CompilerParams(dimension_semantics=sem, vmem_limit_bytes=VMEM_LIMIT)


def _rms(x):
    return x * lax.rsqrt(jnp.mean(x * x, axis=-1, keepdims=True) + RMS_EPS)


def _ada_kernel(c_ref, w_ref, b_ref, o_ref):
    c = c_ref[...]
    s = c * jax.nn.sigmoid(c)
    o_ref[...] = jnp.dot(s, w_ref[...], precision=HIGHEST, preferred_element_type=F32) + b_ref[...]


def _ada(c, w, b):
    n, d = c.shape
    nout = w.shape[1]
    return pl.pallas_call(
        _ada_kernel,
        grid=(nout // d,),
        in_specs=[pl.BlockSpec((n, d), lambda j: (0, 0)),
                  pl.BlockSpec((d, d), lambda j: (0, j)),
                  pl.BlockSpec((1, d), lambda j: (0, j))],
        out_specs=pl.BlockSpec((n, d), lambda j: (0, j)),
        out_shape=jax.ShapeDtypeStruct((n, nout), F32),
        compiler_params=_cparams("arbitrary"),
    )(c, w, b.reshape(1, nout))


def _project(x, mod_ref, g_ref, w_ref, cos_ref, sin_ref, aw):
    h = _rms(x) * g_ref[...]
    h = h * (1.0 + mod_ref[0, 1]) + mod_ref[0, 0]
    proj = jnp.dot(h.astype(BF16), w_ref[...], preferred_element_type=F32)
    cos = cos_ref[...]
    sin = sin_ref[...]
    lane = lax.broadcasted_iota(jnp.int32, (1, aw), 1)
    first_half = (lane & (HEAD_DIM - 1)) < (HEAD_DIM // 2)

    def rope(t):
        rot = jnp.where(first_half, pltpu.roll(t, aw - HEAD_DIM // 2, 1), pltpu.roll(t, HEAD_DIM // 2, 1))
        return t * cos + rot * sin

    q = rope(proj[:, :aw]) * (HEAD_DIM ** -0.5)
    k = rope(proj[:, aw:2 * aw])
    return q, k, proj[:, 2 * aw:3 * aw], proj[:, 3 * aw:]


def _inproj_prompt_kernel(x_ref, mod_ref, g_ref, w_ref, cos_ref, sin_ref, regroup_ref,
                          q_ref, kv_ref, kt_ref, vt_ref, u_ref, *, aw):
    q, k, v, u = _project(x_ref[0], mod_ref, g_ref, w_ref, cos_ref, sin_ref, aw)
    kt_ref[0] = k.T
    vt_ref[0] = v.T
    g = jnp.dot(regroup_ref[...], jnp.concatenate([q, k, v, u], axis=1).astype(BF16), preferred_element_type=F32)
    kv = pltpu.bitcast(g[:, aw:2 * aw], jnp.uint32) | (pltpu.bitcast(g[:, 2 * aw:3 * aw], jnp.uint32) >> 16)
    n_res, rows = q_ref.shape[2], q_ref.shape[3]
    for hp in range(aw // LANES):
        cols = slice(hp * LANES, (hp + 1) * LANES)
        for r in range(n_res):
            q_ref[0, hp, r] = g[r * rows:(r + 1) * rows, cols]
            kv_ref[0, hp, r] = kv[r * rows:(r + 1) * rows, cols]
    u = g[:, 3 * aw:].astype(BF16)
    for lt in range(u_ref.shape[1]):
        u_ref[0, lt] = u[:, lt * LANES:(lt + 1) * LANES]


def _regroup_matrix(tm):
    lc = SSM_CHUNK
    dst = jnp.arange(tm)
    src = (dst % (tm // lc)) * lc + dst // (tm // lc)
    return (src[:, None] == jnp.arange(tm)[None, :]).astype(BF16)


def _inproj_prompt(x, mod, g, w_bf, cos, sin, tm):
    b, t, d = x.shape
    nproj = w_bf.shape[1]
    aw = cos.shape[1]
    sw = nproj - 3 * aw
    n_lt = aw // LANES
    tok = lambda i, j: (j, i, 0)
    hp_major = lambda i, j: (j, 0, i, 0)
    transposed = lambda i, j: (j, 0, i)
    assert tm // SSM_CHUNK * SSM_CHUNK == tm and SSM_CHUNK == ATTN_RESIDUES
    nr = ATTN_RESIDUES
    res_major = lambda i, j: (j, 0, 0, i, 0)
    out_shapes = [jax.ShapeDtypeStruct((b, n_lt, nr, t // nr, LANES), F32),
                  jax.ShapeDtypeStruct((b, n_lt, nr, t // nr, LANES), jnp.uint32),
        jax.ShapeDtypeStruct((b, aw, t), F32), jax.ShapeDtypeStruct((b, aw, t), F32),
        jax.ShapeDtypeStruct((b, sw // LANES, t, LANES), BF16)]
    return pl.pallas_call(
        functools.partial(_inproj_prompt_kernel, aw=aw),
        grid=(t // tm, b),
        in_specs=[pl.BlockSpec((1, tm, d), tok),
                  pl.BlockSpec((1, 6, 1, d), lambda i, j: (j, 0, 0, 0)),
                  pl.BlockSpec((1, d), lambda i, j: (0, 0)),
                  pl.BlockSpec((d, nproj), lambda i, j: (0, 0)),
                  pl.BlockSpec((tm, aw), lambda i, j: (i, 0)),
                  pl.BlockSpec((tm, aw), lambda i, j: (i, 0)),
                  pl.BlockSpec((tm, tm), lambda i, j: (0, 0))],
        out_specs=[pl.BlockSpec((1, n_lt, nr, tm // nr, LANES), res_major)] * 2 + [
            pl.BlockSpec((1, aw, tm), transposed), pl.BlockSpec((1, aw, tm), transposed),
            pl.BlockSpec((1, sw // LANES, tm, LANES), hp_major)],
        out_shape=out_shapes,
        compiler_params=_cparams("arbitrary", "arbitrary"),
    )(x, mod, g, w_bf, cos, sin, _regroup_matrix(tm))


def _inproj_decode_kernel(x_ref, mod_ref, g_ref, w_ref, cos_ref, sin_ref, qt_ref, kt_ref, vt_ref, u_ref, *, aw):
    q, k, v, u = _project(x_ref[...], mod_ref, g_ref, w_ref, cos_ref, sin_ref, aw)
    qt_ref[...] = q.T
    kt_ref[...] = k.T
    vt_ref[...] = v.T
    u_ref[...] = u


def _inproj_decode(x, mod, g, w_bf, cos, sin):
    n, d = x.shape
    nproj = w_bf.shape[1]
    aw = cos.shape[1]
    return pl.pallas_call(
        functools.partial(_inproj_decode_kernel, aw=aw),
        out_shape=[jax.ShapeDtypeStruct((aw, n), F32)] * 3 + [jax.ShapeDtypeStruct((n, nproj - 3 * aw), F32)],
        compiler_params=pltpu.CompilerParams(vmem_limit_bytes=VMEM_LIMIT),
    )(x, mod, g, w_bf, cos, sin)


def _rope_tables(pos, n_heads):
    half = HEAD_DIM // 2
    inv_freq = ROPE_THETA ** (-jnp.arange(half, dtype=F32) / half)
    ang = pos.astype(F32)[:, None] * inv_freq[None, :]
    cos = jnp.cos(ang)
    sin = jnp.sin(ang)
    cos_h = jnp.concatenate([cos, cos], axis=-1)
    sin_h = jnp.concatenate([-sin, sin], axis=-1)
    return jnp.tile(cos_h, (1, n_heads)), jnp.tile(sin_h, (1, n_heads))


def _dilated_attn_kernel(q_ref, kv_ref, o_ref, acc, mrun, lrun):
    t = q_ref.shape[2]
    blk = ATTN_BLOCK
    l16 = t // ATTN_RESIDUES
    lane = lax.broadcasted_iota(jnp.int32, (1, LANES), 1)
    head_mask = [(lane < HEAD_DIM).astype(F32), (lane >= HEAD_DIM).astype(F32)]
    first = lax.broadcasted_iota(jnp.int32, (blk, LANES), 1) < HEAD_DIM
    row = lax.broadcasted_iota(jnp.int32, (blk, 2 * blk), 0)
    col = lax.broadcasted_iota(jnp.int32, (blk, 2 * blk), 1)
    order = sorted(DILATED_PATTERNS, key=lambda p: p[1])
    for pi, (window, dil) in enumerate(order):
        n_keys = window // dil
        n_piece = ATTN_RESIDUES // dil
        pr = blk // n_piece
        nblk = t // (dil * blk)
        member = lambda p: (p % pr) * n_piece + p // pr
        dist = member(row) + blk - ((col // blk) * blk + member(col % blk))
        bias_std = jnp.where(dist >= 0, jnp.where(dist <= n_keys, 0.0, NEG_BIG), NEG_BIG)
        bias_first = jnp.where(col < blk, NEG_BIG, bias_std)

        def block(idx, carry, pi=pi, dil=dil, n_piece=n_piece, pr=pr, nblk=nblk,
                  bias_std=bias_std, bias_first=bias_first):
            c = idx // nblk
            a = idx % nblk
            a_prev = jnp.maximum(a - 1, 0)
            rows_of = lambda blk_i: [pl.ds(pl.multiple_of((c + dil * j) * l16 + blk_i * pr, pr), pr)
                                     for j in range(n_piece)]
            q_rows = rows_of(a)
            gather = lambda ref, rs: jnp.concatenate([ref[0, 0, r, :] for r in rs], axis=0)
            q2 = gather(q_ref, q_rows)
            kv2 = gather(kv_ref, rows_of(a_prev) + q_rows)
            k2 = pltpu.bitcast(kv2 & jnp.uint32(0xFFFF0000), F32).astype(BF16)
            v2 = pltpu.bitcast(kv2 << 16, F32).astype(BF16)
            bias = jnp.where(a == 0, bias_first, bias_std)
            os, ms, ls = [], [], []
            for half in range(HEADS_PER_LANE_TILE):
                qh = (q2 * head_mask[half]).astype(BF16)
                s = lax.dot_general(qh, k2, (((1,), (1,)), ((), ())), preferred_element_type=F32) + bias
                m = jnp.max(s, axis=1, keepdims=True)
                p = jnp.exp(s - m)
                ls.append(jnp.sum(p, axis=1, keepdims=True))
                ms.append(m)
                os.append(jnp.dot(p.astype(BF16), v2, preferred_element_type=F32))
            o_t = jnp.where(first, os[0], os[1])
            m_t = jnp.where(first, ms[0], ms[1])
            l_t = jnp.where(first, ls[0], ls[1])
            load = lambda ref: jnp.concatenate([ref[r, :] for r in q_rows], axis=0)

            def store(ref, val):
                for j, r in enumerate(q_rows):
                    ref[r, :] = val[j * pr:(j + 1) * pr]

            if pi == 0:
                store(acc, o_t)
                store(mrun, m_t)
                store(lrun, l_t)
            else:
                m_o = load(mrun)
                m_n = jnp.maximum(m_o, m_t)
                a_o = jnp.exp(m_o - m_n)
                a_t = jnp.exp(m_t - m_n)
                acc_n = a_o * load(acc) + a_t * o_t
                l_n = a_o * load(lrun) + a_t * l_t
                if pi == len(order) - 1:
                    assert n_piece == 1
                    o_ref[0, 0, q_rows[0], :] = (acc_n / l_n).astype(o_ref.dtype)
                else:
                    store(acc, acc_n)
                    store(mrun, m_n)
                    store(lrun, l_n)
            return carry

        lax.fori_loop(0, dil * nblk, block, 0, unroll=ATTN_UNROLL)


def _dilated_attention(q, kv):
    b, n_lt, t, _ = q.shape
    for window, dil in DILATED_PATTERNS:
        assert window // dil == ATTN_BLOCK and t % (dil * 2 * ATTN_BLOCK) == 0 and ATTN_RESIDUES % dil == 0
    assert max(dl for _, dl in DILATED_PATTERNS) == ATTN_RESIDUES
    spec = pl.BlockSpec((1, 1, t, LANES), lambda i, j: (i, j, 0, 0))
    return pl.pallas_call(
        _dilated_attn_kernel,
        grid=(b, n_lt),
        in_specs=[spec, spec],
        out_specs=spec,
        out_shape=jax.ShapeDtypeStruct((b, n_lt, t, LANES), BF16),
        scratch_shapes=[pltpu.VMEM((t, LANES), F32) for _ in range(3)],
        compiler_params=_cparams("arbitrary", "arbitrary"),
    )(q, kv)


def _decode_kernel(qt_ref, kt_ref, vt_ref, ck_ref, cv_ref, ok_ref, ov_ref, at_ref, *, hb):
    i = pl.program_id(0)
    hg = pl.program_id(1)
    w = ck_ref.shape[3]
    bs = qt_ref.shape[1]
    sel = lax.broadcasted_iota(jnp.int32, (HEAD_DIM, bs), 1) == i

    def column(ref, rs):
        return jnp.sum(jnp.where(sel, ref[rs, :], 0.0), axis=1, keepdims=True)

    pos = lax.broadcasted_iota(jnp.int32, (1, w), 1)
    dist = w - pos
    cnt = jnp.zeros((1, w), F32)
    for window, dil in DILATED_PATTERNS:
        cnt = cnt + jnp.where((dist & (dil - 1)) == 0, jnp.where(dist <= window, 1.0, 0.0), 0.0)
    n_pat = float(len(DILATED_PATTERNS))
    last = lax.broadcasted_iota(jnp.int32, (HEAD_DIM, w), 1) == w - 1

    @pl.when((i == 0) & (hg == 0))
    def _():
        at_ref[...] = jnp.zeros(at_ref.shape, F32)

    for h in range(hb):
        rs = pl.ds(pl.multiple_of((hg * hb + h) * HEAD_DIM, HEAD_DIM), HEAD_DIM)
        qc, kc, vc = column(qt_ref, rs), column(kt_ref, rs), column(vt_ref, rs)
        kk = ck_ref[0, h]
        vv = cv_ref[0, h]
        s = jnp.sum(kk * qc, axis=0, keepdims=True)
        s_new = jnp.sum(kc * qc, axis=0, keepdims=True)
        s = jnp.where(cnt > 0.0, s, NEG_BIG)
        m = jnp.maximum(jnp.max(s, axis=1, keepdims=True), s_new)
        p = cnt * jnp.exp(s - m)
        p_new = n_pat * jnp.exp(s_new - m)
        l = jnp.sum(p, axis=1, keepdims=True) + p_new
        o = (jnp.sum(vv * p, axis=1, keepdims=True) + p_new * vc) / l
        at_ref[rs, :] = jnp.where(sel, o, at_ref[rs, :])
        ok_ref[0, h] = jnp.where(last, kc, pltpu.roll(kk, w - 1, 1))
        ov_ref[0, h] = jnp.where(last, vc, pltpu.roll(vv, w - 1, 1))


def _decode_attention(qt, kt, vt, ck, cv):
    bs, n_heads, hd, w = ck.shape
    aw = n_heads * hd
    for window, dil in DILATED_PATTERNS:
        assert window <= w and dil & (dil - 1) == 0
    hb = DECODE_HEADS
    full = pl.BlockSpec((aw, bs), lambda i, j: (0, 0))
    buf = pl.BlockSpec((1, hb, hd, w), lambda i, j: (i, j, 0, 0))
    return pl.pallas_call(
        functools.partial(_decode_kernel, hb=hb),
        grid=(bs, n_heads // hb),
        in_specs=[full, full, full, buf, buf],
        out_specs=[buf, buf, full],
        out_shape=[jax.ShapeDtypeStruct(ck.shape, F32), jax.ShapeDtypeStruct(cv.shape, F32),
                   jax.ShapeDtypeStruct((aw, bs), F32)],
        compiler_params=_cparams("arbitrary", "arbitrary"),
    )(qt, kt, vt, ck, cv)


def _s5_discretise(lam_re, lam_im, log_dt, b_re, b_im):
    dt = jnp.exp(log_dt)[:, None]
    mag = jnp.exp(lam_re * dt)
    ar = mag * jnp.cos(lam_im * dt)
    ai = mag * jnp.sin(lam_im * dt)
    den = lam_re * lam_re + lam_im * lam_im
    fr = ((ar - 1.0) * lam_re + ai * lam_im) / den
    fi = (ai * lam_re - (ar - 1.0) * lam_im) / den
    bbr = fr[..., None] * b_re - fi[..., None] * b_im
    bbi = fr[..., None] * b_im + fi[..., None] * b_re
    return dt, ar, ai, bbr, bbi


def _s5_chunk_matrices(lam_re, lam_im, log_dt, b_re, b_im, c_re, c_im):
    lc = SSM_CHUNK
    g, p, c = b_re.shape
    dt, _, _, bbr, bbi = _s5_discretise(lam_re, lam_im, log_dt, b_re, b_im)
    kk = jnp.arange(lc + 1, dtype=F32)[:, None, None]
    mag = jnp.exp(kk * lam_re * dt)
    apr = mag * jnp.cos(kk * lam_im * dt)
    api = mag * jnp.sin(kk * lam_im * dt)
    akb_r = apr[:lc, :, :, None] * bbr - api[:lc, :, :, None] * bbi
    akb_i = apr[:lc, :, :, None] * bbi + api[:lc, :, :, None] * bbr
    kern = (jnp.einsum('gop,kgpc->kgoc', c_re, akb_r, precision=HIGHEST)
            - jnp.einsum('gop,kgpc->kgoc', c_im, akb_i, precision=HIGHEST))
    ii = jnp.arange(lc)
    lag = ii[None, :] - ii[:, None]
    toe = jnp.where((lag >= 0)[:, :, None, None, None], kern[jnp.clip(lag, 0, lc - 1)], 0.0)
    toe = toe.transpose(2, 0, 4, 1, 3).reshape(g, lc * c, lc * c)
    rev = lc - 1 - ii
    ws_r = akb_r[rev].transpose(1, 0, 3, 2).reshape(g, lc * c, p)
    ws_i = akb_i[rev].transpose(1, 0, 3, 2).reshape(g, lc * c, p)
    a1r, a1i = apr[1:], api[1:]
    ca_r = c_re[None] * a1r[:, :, None, :] - c_im[None] * a1i[:, :, None, :]
    ca_i = c_re[None] * a1i[:, :, None, :] + c_im[None] * a1r[:, :, None, :]
    wy_r = ca_r.transpose(1, 3, 0, 2).reshape(g, p, lc * c)
    wy_i = (-ca_i).transpose(1, 3, 0, 2).reshape(g, p, lc * c)

    def pair_diag(m):
        r, s = m.shape[1:]
        m2 = m.reshape(g // 2, 2, r, s)
        z = jnp.zeros((g // 2, r, s), m.dtype)
        top = jnp.concatenate([m2[:, 0], z], axis=2)
        bot = jnp.concatenate([z, m2[:, 1]], axis=2)
        return jnp.concatenate([top, bot], axis=1)

    alc_r = apr[lc].reshape(1, g * p)
    alc_i = api[lc].reshape(1, g * p)
    return (toe.astype(BF16), pair_diag(ws_r).astype(BF16), pair_diag(ws_i).astype(BF16),
            pair_diag(wy_r).astype(BF16), pair_diag(wy_i).astype(BF16), alc_r, alc_i)


def _s5_step_matrices(lam_re, lam_im, log_dt, b_re, b_im, c_re, c_im):
    g, p, c = b_re.shape
    _, ar, ai, bbr, bbi = _s5_discretise(lam_re, lam_im, log_dt, b_re, b_im)
    eye = jnp.eye(g, dtype=F32)
    bd_br = jnp.einsum('gpc,gh->gchp', bbr, eye).reshape(g * c, g * p)
    bd_bi = jnp.einsum('gpc,gh->gchp', bbi, eye).reshape(g * c, g * p)
    bd_cr = jnp.einsum('gcp,gh->gphc', c_re, eye).reshape(g * p, g * c)
    bd_ci = jnp.einsum('gcp,gh->gphc', c_im, eye).reshape(g * p, g * c)
    return bd_br, bd_bi, bd_cr, bd_ci, ar.reshape(1, g * p), ai.reshape(1, g * p)


def _s5_prompt_kernel(u_ref, toe_ref, wsr_ref, wsi_ref, wyr_ref, wyi_ref, ar_ref, ai_ref,
                      y_ref, hr_ref, hi_ref, uperm, yperm, sr, si, hr_hist, hi_hist, hcar_r, hcar_i):
    j = pl.program_id(1)
    nc = uperm.shape[0]
    lc, gc = SSM_CHUNK, SSM_GROUP
    gpt = LANES // gc
    n_lt = u_ref.shape[1]
    rb = ROW_TILE // lc
    gw = lc * gc
    lane_grp = lax.broadcasted_iota(jnp.int32, (1, LANES), 1) // gc

    def group_transpose(vs):
        d = 1
        while d < gpt:
            hi = (lane_grp & d) != 0
            nxt = list(vs)
            for a in range(gpt):
                if a & d == 0:
                    nxt[a] = jnp.where(hi, pltpu.roll(vs[a + d], d * gc, 1), vs[a])
                    nxt[a + d] = jnp.where(hi, vs[a + d], pltpu.roll(vs[a], LANES - d * gc, 1))
            vs = nxt
            d *= 2
        return vs

    def permute_in(rc, carry):
        r0 = pl.multiple_of(rc * rb, rb)
        for lt in range(n_lt):
            for h in range(lc // gpt):
                xs = [u_ref[0, lt, pl.ds(pl.multiple_of(r0 * lc + (h * gpt + il) * rb, rb), rb), :].astype(F32)
                      for il in range(gpt)]
                for gl, d in enumerate(group_transpose(xs)):
                    c0 = (lt * gpt + gl) * gw + h * LANES
                    uperm[pl.ds(r0, rb), c0:c0 + LANES] = d.astype(BF16)
        return carry

    lax.fori_loop(0, nc // rb, permute_in, 0)

    n_pairs = wsr_ref.shape[0]
    pw = wsr_ref.shape[1]
    sw = wsr_ref.shape[2]
    for pr in range(n_pairs):
        up = uperm[:, pr * pw:(pr + 1) * pw]
        sr[:, pr * sw:(pr + 1) * sw] = jnp.dot(up, wsr_ref[pr], preferred_element_type=F32)
        si[:, pr * sw:(pr + 1) * sw] = jnp.dot(up, wsi_ref[pr], preferred_element_type=F32)

    ar = ar_ref[...]
    ai = ai_ref[...]
    nstate = ar.shape[1]
    rowid = lax.broadcasted_iota(jnp.int32, (8, nstate), 0)

    @pl.when(j == 0)
    def _():
        hcar_r[...] = jnp.zeros(hcar_r.shape, F32)
        hcar_i[...] = jnp.zeros(hcar_i.shape, F32)

    def eight_chunks(c8, carry):
        h_r, h_i = carry
        base = pl.multiple_of(c8 * 8, 8)
        s_r8 = sr[pl.ds(base, 8), :]
        s_i8 = si[pl.ds(base, 8), :]
        t_r = jnp.zeros((8, nstate), F32)
        t_i = jnp.zeros((8, nstate), F32)
        for r in range(8):
            t_r = jnp.where(rowid == r, h_r, t_r)
            t_i = jnp.where(rowid == r, h_i, t_i)
            n_r = ar * h_r - ai * h_i + s_r8[r:r + 1, :]
            n_i = ar * h_i + ai * h_r + s_i8[r:r + 1, :]
            h_r, h_i = n_r, n_i
        hr_hist[pl.ds(base, 8), :] = t_r
        hi_hist[pl.ds(base, 8), :] = t_i
        return h_r, h_i

    h_r, h_i = lax.fori_loop(0, nc // 8, eight_chunks, (hcar_r[...], hcar_i[...]))
    hcar_r[...] = h_r
    hcar_i[...] = h_i
    hr_ref[0] = h_r
    hi_ref[0] = h_i

    for pr in range(n_pairs):
        st = slice(pr * sw, (pr + 1) * sw)
        y2 = (jnp.dot(hr_hist[:, st].astype(BF16), wyr_ref[pr], preferred_element_type=F32)
              + jnp.dot(hi_hist[:, st].astype(BF16), wyi_ref[pr], preferred_element_type=F32))
        for gg in range(2):
            g = pr * 2 + gg
            cols = slice(g * gw, (g + 1) * gw)
            yi = jnp.dot(uperm[:, cols], toe_ref[g], preferred_element_type=F32)
            yperm[:, cols] = yi + y2[:, gg * gw:(gg + 1) * gw]

    def permute_out(rc, carry):
        r0 = pl.multiple_of(rc * rb, rb)
        for lt in range(n_lt):
            for h in range(lc // gpt):
                ds_ = [yperm[pl.ds(r0, rb), (lt * gpt + gl) * gw + h * LANES:(lt * gpt + gl) * gw + (h + 1) * LANES]
                       for gl in range(gpt)]
                for il, yv in enumerate(group_transpose(ds_)):
                    y_ref[0, lt, pl.ds(pl.multiple_of(r0 * lc + (h * gpt + il) * rb, rb), rb), :] = yv.astype(y_ref.dtype)
        return carry

    lax.fori_loop(0, nc // rb, permute_out, 0)


def _s5_prompt(u, mats):
    b, n_lt, t, _ = u.shape
    wid = n_lt * LANES
    nstate = mats[-1].shape[1]
    ts = min(t, S5_SLAB)
    nc = ts // SSM_CHUNK
    assert t % ts == 0 and ts % ROW_TILE == 0
    full = lambda a: pl.BlockSpec(a.shape, lambda i, j: (0,) * a.ndim)
    return pl.pallas_call(
        _s5_prompt_kernel,
        grid=(b, t // ts),
        in_specs=[pl.BlockSpec((1, n_lt, ts, LANES), lambda i, j: (i, 0, j, 0))] + [full(a) for a in mats],
        out_specs=[pl.BlockSpec((1, n_lt, ts, LANES), lambda i, j: (i, 0, j, 0)),
                   pl.BlockSpec((1, 1, nstate), lambda i, j: (i, 0, 0)),
                   pl.BlockSpec((1, 1, nstate), lambda i, j: (i, 0, 0))],
        out_shape=[jax.ShapeDtypeStruct(u.shape, BF16),
                   jax.ShapeDtypeStruct((b, 1, nstate), F32),
                   jax.ShapeDtypeStruct((b, 1, nstate), F32)],
        scratch_shapes=[pltpu.VMEM((nc, wid * SSM_CHUNK), BF16), pltpu.VMEM((nc, wid * SSM_CHUNK), F32)]
        + [pltpu.VMEM((nc, nstate), F32) for _ in range(4)]
        + [pltpu.VMEM((1, nstate), F32) for _ in range(2)],
        compiler_params=_cparams("arbitrary", "arbitrary"),
    )(u, *mats)


def _s5_step_kernel(u_ref, h0r_ref, h0i_ref, bbr_ref, bbi_ref, cr_ref, ci_ref, ar_ref, ai_ref,
                    y_ref, xr_ref, xi_ref):
    u = u_ref[...]
    ar = ar_ref[...]
    ai = ai_ref[...]
    h0r = h0r_ref[...]
    h0i = h0i_ref[...]
    xr = ar * h0r - ai * h0i + jnp.dot(u, bbr_ref[...], precision=HIGHEST, preferred_element_type=F32)
    xi = ar * h0i + ai * h0r + jnp.dot(u, bbi_ref[...], precision=HIGHEST, preferred_element_type=F32)
    xr_ref[...] = xr
    xi_ref[...] = xi
    y_ref[...] = (jnp.dot(xr, cr_ref[...], precision=HIGHEST, preferred_element_type=F32)
                  - jnp.dot(xi, ci_ref[...], precision=HIGHEST, preferred_element_type=F32))


def _s5_step(u, h0r, h0i, mats):
    n, wid = u.shape
    nstate = h0r.shape[1]
    return pl.pallas_call(
        _s5_step_kernel,
        out_shape=[jax.ShapeDtypeStruct((n, wid), F32),
                   jax.ShapeDtypeStruct((n, nstate), F32),
                   jax.ShapeDtypeStruct((n, nstate), F32)],
        compiler_params=pltpu.CompilerParams(vmem_limit_bytes=VMEM_LIMIT),
    )(u, h0r, h0i, *mats)


def _split_bf16(a):
    hi = a.astype(BF16)
    return hi, (a - hi.astype(F32)).astype(BF16)


def _mix_kernel(x_ref, mod_ref, attn_ref, ys_ref, u_ref, d_ref, wglu_ref, bglu_ref, ga_ref, gs_ref,
                wout_ref, g2_ref, wr_ref, br_ref, ungroup_ref, below_ref,
                x1_ref, h2_ref, gate_ref, idx_ref, rank_ref, count_ref, seen, *, lane_tiled):
    def rows(ref):
        if lane_tiled:
            tile = lambda i: ref[0, i].reshape(-1, LANES)
            return jnp.concatenate([tile(i) for i in range(ref.shape[1])], axis=1).astype(F32)
        return ref[...]

    a_n = (_rms(rows(attn_ref)) * ga_ref[...]).astype(BF16)

    y = rows(ys_ref) + d_ref[...] * rows(u_ref)
    y = 0.5 * y * (1.0 + jnp.tanh(np.float32(np.sqrt(2.0 / np.pi)) * (y + 0.044715 * (y * y * y))))
    z = jnp.dot(y.astype(BF16), wglu_ref[...], preferred_element_type=F32) + bglu_ref[...]
    ssm = y * jax.nn.sigmoid(z)
    s_n = (_rms(ssm) * gs_ref[...]).astype(BF16)
    both = jnp.concatenate([a_n, s_n], axis=1)
    if lane_tiled:
        both = jnp.dot(ungroup_ref[...], both, preferred_element_type=F32).astype(BF16)
    mixed = jnp.dot(both, wout_ref[...], preferred_element_type=F32)
    x1 = x_ref[...] + mod_ref[0, 2] * mixed
    x1_ref[...] = x1
    h2 = _rms(x1) * g2_ref[...] * (1.0 + mod_ref[0, 4]) + mod_ref[0, 3]
    h2_ref[...] = h2.astype(BF16)

    h_hi, h_lo = _split_bf16(h2)
    hh = jnp.dot(h_hi, wr_ref[...], preferred_element_type=F32)
    logits = (hh[:, :LANES] + hh[:, LANES:]
              + jnp.dot(h_lo, wr_ref[:, 0:LANES], preferred_element_type=F32)) + br_ref[...]
    lane = lax.broadcasted_iota(jnp.int32, logits.shape, 1)
    lane_f = lane.astype(F32)
    cur = logits
    vals, idxs = [], []
    for _ in range(TOP_K):
        mx = jnp.max(cur, axis=1, keepdims=True)
        ix = jnp.min(jnp.where(cur == mx, lane_f, float(LANES)), axis=1, keepdims=True)
        vals.append(mx)
        idxs.append(ix)
        cur = jnp.where(lane_f == ix, -jnp.inf, cur)
    exps = [jnp.exp(v - vals[0]) for v in vals]
    den = exps[0]
    for e in exps[1:]:
        den = den + e
    gate = jnp.zeros(logits.shape, F32)
    idx = jnp.zeros(logits.shape, F32)
    for k in range(TOP_K):
        gate = jnp.where(lane == k, exps[k] / den, gate)
        idx = jnp.where(lane == k, idxs[k], idx)
    gate_ref[...] = gate
    idx_ref[...] = idx.astype(jnp.int32)

    @pl.when(pl.program_id(0) == 0)
    def _():
        seen[...] = jnp.zeros(seen.shape, F32)

    chosen = jnp.zeros(logits.shape, F32)
    for k in range(TOP_K):
        chosen = jnp.where(lane_f == idxs[k], 1.0, chosen)
    earlier = jnp.dot(below_ref[...], chosen.astype(BF16), preferred_element_type=F32) + seen[...]
    rank = jnp.zeros(logits.shape, F32)
    for k in range(TOP_K):
        r_k = jnp.sum(jnp.where(lane_f == idxs[k], earlier, 0.0), axis=1, keepdims=True)
        rank = jnp.where(lane == k, r_k, rank)
    rank_ref[...] = rank.astype(jnp.int32)
    seen[...] += jnp.sum(chosen, axis=0, keepdims=True)
    count_ref[...] = seen[...].astype(jnp.int32)


def _mix(x2d, mod, attn, ys, u, weights, tm, rows_per_mod):
    n, d = x2d.shape
    r = mod.shape[2]
    tiles_per_mod = max(rows_per_mod // tm, 1)
    mod_map = (lambda i: (i // tiles_per_mod, 0, 0, 0)) if r == 1 else (lambda i: (0, 0, i, 0))
    rowspec = lambda w: pl.BlockSpec((tm, w), lambda i: (i, 0))
    full = lambda a: pl.BlockSpec(a.shape, lambda i: (0,) * a.ndim)
    lane_tiled = attn.ndim == 5
    below = (jnp.arange(tm)[:, None] > jnp.arange(tm)[None, :]).astype(BF16)
    weights = tuple(weights) + (_regroup_matrix(tm).T, below)
    if lane_tiled:
        def act_spec(a):
            if a.ndim == 5:
                return pl.BlockSpec((1, a.shape[1], a.shape[2], tm // a.shape[2], LANES),
                                    lambda i: (i // tiles_per_mod, 0, 0, i % tiles_per_mod, 0))
            return pl.BlockSpec((1, a.shape[1], tm, LANES), lambda i: (i // tiles_per_mod, 0, i % tiles_per_mod, 0))
    else:
        act_spec = lambda a: rowspec(a.shape[1])
    return pl.pallas_call(
        functools.partial(_mix_kernel, lane_tiled=lane_tiled),
        grid=(n // tm,),
        in_specs=[rowspec(d), pl.BlockSpec((1, 6, r, d), mod_map), act_spec(attn), act_spec(ys), act_spec(u)]
        + [full(w) for w in weights],
        out_specs=[rowspec(d), rowspec(d), rowspec(LANES), rowspec(LANES), rowspec(LANES),
                   pl.BlockSpec((1, LANES), lambda i: (0, 0))],
        out_shape=[jax.ShapeDtypeStruct((n, d), F32), jax.ShapeDtypeStruct((n, d), BF16),
                   jax.ShapeDtypeStruct((n, LANES), F32), jax.ShapeDtypeStruct((n, LANES), jnp.int32),
                   jax.ShapeDtypeStruct((n, LANES), jnp.int32), jax.ShapeDtypeStruct((1, LANES), jnp.int32)],
        scratch_shapes=[pltpu.VMEM((1, LANES), F32)],
        compiler_params=_cparams("arbitrary"),
    )(x2d, mod, attn, ys, u, *weights)


def _moe_kernel(be_ref, first_ref, nb_ref, x_ref, wg_ref, bg_ref, wu_ref, bu_ref, wd_ref, bd_ref, o_ref,
                wg_bf, wu_bf, wd_bf):
    i = pl.program_id(0)

    @pl.when(first_ref[i] == 1)
    def _():
        wg_bf[...] = wg_ref[0].astype(BF16)
        wu_bf[...] = wu_ref[0].astype(BF16)
        wd_bf[...] = wd_ref[0].astype(BF16)

    @pl.when(i < nb_ref[0])
    def _():
        x = x_ref[...]
        g = jnp.dot(x, wg_bf[...], preferred_element_type=F32) + bg_ref[0]
        up = jnp.dot(x, wu_bf[...], preferred_element_type=F32) + bu_ref[0]
        g = jnp.minimum(g, SWIGLU_LIMIT)
        up = jnp.clip(up, -SWIGLU_LIMIT, SWIGLU_LIMIT)
        hid = (up + 1.0) * g * jax.nn.sigmoid(SWIGLU_ALPHA * g)
        y = jnp.dot(hid.astype(BF16), wd_bf[...], preferred_element_type=F32) + bd_ref[0]
        o_ref[...] = y.astype(o_ref.dtype)

    @pl.when(i >= nb_ref[0])
    def _():
        o_ref[...] = jnp.zeros(o_ref.shape, o_ref.dtype)


def _moe_experts(xb, block_e, first, n_used, wg, bg, wu, bu, wd, bd):
    n_rows, d = xb.shape
    e, _, f = wg.shape
    nb = n_rows // MOE_BLOCK
    wmap = lambda i, be, fi, nu: (be[i], 0, 0)
    grid_spec = pltpu.PrefetchScalarGridSpec(
        num_scalar_prefetch=3,
        grid=(nb,),
        in_specs=[pl.BlockSpec((MOE_BLOCK, d), lambda i, be, fi, nu: (i, 0)),
                  pl.BlockSpec((1, d, f), wmap), pl.BlockSpec((1, 1, f), wmap),
                  pl.BlockSpec((1, d, f), wmap), pl.BlockSpec((1, 1, f), wmap),
                  pl.BlockSpec((1, f, d), wmap), pl.BlockSpec((1, 1, d), wmap)],
        out_specs=pl.BlockSpec((MOE_BLOCK, d), lambda i, be, fi, nu: (i, 0)),
        scratch_shapes=[pltpu.VMEM((d, f), BF16), pltpu.VMEM((d, f), BF16), pltpu.VMEM((f, d), BF16)],
    )
    return pl.pallas_call(
        _moe_kernel,
        grid_spec=grid_spec,
        out_shape=jax.ShapeDtypeStruct((n_rows, d), BF16),
        compiler_params=_cparams("arbitrary"),
    )(block_e, first, n_used, xb, wg, bg.reshape(e, 1, f), wu, bu.reshape(e, 1, f), wd, bd.reshape(e, 1, d))


def _final_kernel(x1_ref, mod_ref, gate_ref, *rest):
    yg_refs, (gf_ref, o_ref) = rest[:-2], rest[-2:]
    gate = gate_ref[...]
    acc = jnp.zeros(x1_ref.shape, F32)
    k = 0
    for yg_ref in yg_refs:
        for j in range(yg_ref.shape[0]):
            acc = acc + gate[:, k:k + 1] * yg_ref[j].astype(F32)
            k += 1
    x = x1_ref[...] + mod_ref[0, 5] * acc
    o_ref[...] = _rms(x) * gf_ref[...]


def _final(x1, mod, gates, yg_parts, g_final, tm, rows_per_mod):
    n, d = x1.shape
    tiles_per_mod = rows_per_mod // tm
    assert sum(p.shape[0] for p in yg_parts) == TOP_K
    return pl.pallas_call(
        _final_kernel,
        grid=(n // tm,),
        in_specs=[pl.BlockSpec((tm, d), lambda i: (i, 0)),
                  pl.BlockSpec((1, 6, 1, d), lambda i: (i // tiles_per_mod, 0, 0, 0)),
                  pl.BlockSpec((tm, LANES), lambda i: (i, 0))]
        + [pl.BlockSpec((p.shape[0], tm, d), lambda i: (0, i, 0)) for p in yg_parts]
        + [pl.BlockSpec((1, d), lambda i: (0, 0))],
        out_specs=pl.BlockSpec((tm, d), lambda i: (i, 0)),
        out_shape=jax.ShapeDtypeStruct((n, d), F32),
        compiler_params=_cparams("arbitrary"),
    )(x1, mod, gates, *yg_parts, g_final.reshape(1, d))


def _moe_decode_kernel(h2_ref, gate_ref, idx_ref, x1_ref, mod_ref, gf_ref,
                       wg_ref, bg_ref, wu_ref, bu_ref, wd_ref, bd_ref, o_ref, acc):
    e = pl.program_id(0)

    @pl.when(e == 0)
    def _():
        acc[...] = jnp.zeros(acc.shape, F32)

    gate_e = jnp.sum(jnp.where(idx_ref[...] == e, gate_ref[...], 0.0), axis=1, keepdims=True)
    x = h2_ref[...]
    g = jnp.dot(x, wg_ref[0].astype(BF16), preferred_element_type=F32) + bg_ref[0]
    up = jnp.dot(x, wu_ref[0].astype(BF16), preferred_element_type=F32) + bu_ref[0]
    g = jnp.minimum(g, SWIGLU_LIMIT)
    up = jnp.clip(up, -SWIGLU_LIMIT, SWIGLU_LIMIT)
    hid = (up + 1.0) * g * jax.nn.sigmoid(SWIGLU_ALPHA * g)
    y = jnp.dot(hid.astype(BF16), wd_ref[0].astype(BF16), preferred_element_type=F32) + bd_ref[0]
    acc[...] += gate_e * y

    @pl.when(e == pl.num_programs(0) - 1)
    def _():
        x1 = x1_ref[...] + mod_ref[0, 5] * acc[...]
        o_ref[...] = _rms(x1) * gf_ref[...]


def _moe_decode(h2, gates, idx, x1, mod, g_final, wg, bg, wu, bu, wd, bd):
    n, d = x1.shape
    e, _, f = wg.shape
    full = lambda a: pl.BlockSpec(a.shape, lambda i: (0,) * a.ndim)
    wmap = lambda i: (i, 0, 0)
    return pl.pallas_call(
        _moe_decode_kernel,
        grid=(e,),
        in_specs=[full(h2), full(gates), full(idx), full(x1), full(mod), pl.BlockSpec((1, d), lambda i: (0, 0)),
                  pl.BlockSpec((1, d, f), wmap), pl.BlockSpec((1, 1, f), wmap),
                  pl.BlockSpec((1, d, f), wmap), pl.BlockSpec((1, 1, f), wmap),
                  pl.BlockSpec((1, f, d), wmap), pl.BlockSpec((1, 1, d), wmap)],
        out_specs=pl.BlockSpec((n, d), lambda i: (0, 0)),
        out_shape=jax.ShapeDtypeStruct((n, d), F32),
        scratch_shapes=[pltpu.VMEM((n, d), F32)],
        compiler_params=_cparams("arbitrary"),
    )(h2, gates, idx, x1, mod, g_final.reshape(1, d), wg, bg.reshape(e, 1, f), wu, bu.reshape(e, 1, f),
      wd, bd.reshape(e, 1, d))


def _dispatch(top_e, rank, counts):
    n_exp = counts.shape[0]
    n_assign = top_e.shape[0] * TOP_K
    padded = (counts + MOE_BLOCK - 1) // MOE_BLOCK * MOE_BLOCK
    pend = jnp.cumsum(padded)
    pstart = pend - padded
    dest = pstart[top_e] + rank
    nb = -(-n_assign // MOE_BLOCK) + n_exp
    block_start = jnp.arange(nb, dtype=jnp.int32) * MOE_BLOCK
    block_e = jnp.minimum(jnp.sum((pend[None, :] <= block_start[:, None]).astype(jnp.int32), axis=1), n_exp - 1)
    bits = max(n_assign - 1, 1).bit_length()
    assert (n_exp << bits) < 2 ** 31
    packed = (top_e.reshape(-1) << bits) | jnp.arange(n_assign, dtype=jnp.int32)
    order = jnp.sort(packed) & ((1 << bits) - 1)
    ustart = jnp.cumsum(counts) - counts
    first_src = ustart[block_e] + block_start - pstart[block_e]
    last_src = ustart[block_e] + counts[block_e] - 1
    src = jnp.minimum(first_src[:, None] + jnp.arange(MOE_BLOCK, dtype=jnp.int32)[None, :], last_src[:, None])
    row_tok = order[jnp.clip(src.reshape(-1), 0, n_assign - 1)] // TOP_K
    first = jnp.concatenate([jnp.ones((1,), jnp.int32), (block_e[1:] != block_e[:-1]).astype(jnp.int32)])
    n_used = (pend[-1] // MOE_BLOCK).astype(jnp.int32).reshape(1)
    return row_tok, dest, block_e, first, n_used


def kernel(x_prompt, x_sample, c_prompt, c_sample, cache_k, cache_v, state_ssm_re, state_ssm_im, w_ada, b_ada, g_norm1, g_norm2, w_in, lambda_re, lambda_im, log_dt, b_ssm_re, b_ssm_im, c_ssm_re, c_ssm_im, d_ssm, w_glu, b_glu, g_attn_out, g_ssm_out, w_out, w_router, b_router, w_gate, b_gate, w_up, b_up, w_down, b_down, g_final):
    assert w_ada.shape[0] == 1, "one layer"
    b, t, d = x_prompt.shape
    bs = x_sample.shape[0]
    assert x_sample.shape[1] == 1
    wbuf, n_heads = cache_k.shape[2], cache_k.shape[3]
    aw = n_heads * HEAD_DIM
    n_groups, n_state = lambda_re.shape[1:]
    sw = n_groups * SSM_GROUP
    n_exp = w_router.shape[2]
    keep = min(max(w for w, _ in DILATED_PATTERNS), t)
    tm = min(ROW_TILE, t)

    mod = _ada(jnp.concatenate([c_prompt, c_sample], axis=0), w_ada[0], b_ada[0])
    mod_p = mod[:b].reshape(b, 6, 1, d)
    mod_s = mod[b:].reshape(bs, 6, d).transpose(1, 0, 2)[None]

    w_in_bf = w_in[0].astype(BF16)
    g1 = g_norm1[0].reshape(1, d)
    ssm_params = (lambda_re[0], lambda_im[0], log_dt[0], b_ssm_re[0], b_ssm_im[0], c_ssm_re[0], c_ssm_im[0])
    wr_pad = jnp.zeros((d, LANES), F32).at[:, :n_exp].set(w_router[0])
    wr_hi = wr_pad.astype(BF16)
    wr_lo = (wr_pad - wr_hi.astype(F32)).astype(BF16)
    br_pad = jnp.full((1, LANES), NEG_BIG, F32).at[0, :n_exp].set(b_router[0])
    mix_w = (d_ssm[0].reshape(1, sw), w_glu[0].astype(BF16), b_glu[0].reshape(1, sw),
             g_attn_out[0].reshape(1, aw), g_ssm_out[0].reshape(1, sw), w_out[0].astype(BF16),
             g_norm2[0].reshape(1, d), jnp.concatenate([wr_hi, wr_lo], axis=1), br_pad)
    experts = (w_gate[0], b_gate[0], w_up[0], b_up[0], w_down[0], b_down[0])

    cos_p, sin_p = _rope_tables(jnp.arange(t), n_heads)
    q_p, kv_p, kt_p, vt_p, u_p = _inproj_prompt(x_prompt, mod_p, g1, w_in_bf, cos_p, sin_p, tm)
    flat_t = lambda a: a.reshape(b, a.shape[1], t, LANES)
    attn_p = _dilated_attention(flat_t(q_p), flat_t(kv_p)).reshape(q_p.shape)
    ys_p, hr_p, hi_p = _s5_prompt(u_p, _s5_chunk_matrices(*ssm_params))
    x1_p, h2_p, gate_p, idx_p, rank_p, count_p = _mix(x_prompt.reshape(b * t, d), mod_p, attn_p, ys_p, u_p,
                                                      mix_w, tm, t)

    row_tok, dest, block_e, first, n_used = _dispatch(idx_p[:, :TOP_K], rank_p[:, :TOP_K], count_p[0, :n_exp])
    xb = h2_p.at[row_tok].get(mode="promise_in_bounds")

    cos_s, sin_s = _rope_tables(jnp.full((1,), PAST_LEN), n_heads)
    qt_s, kt_s, vt_s, u_s = _inproj_decode(x_sample.reshape(bs, d), mod_s, g1, w_in_bf, cos_s, sin_s)
    to_hdp = lambda c: jnp.transpose(c[0], (0, 2, 3, 1))
    from_hdp = lambda c: jnp.transpose(c, (0, 3, 1, 2))[None]
    ck_new, cv_new, attn_t = _decode_attention(qt_s, kt_s, vt_s, to_hdp(cache_k), to_hdp(cache_v))
    ys_s, hr_s, hi_s = _s5_step(u_s, state_ssm_re[0].reshape(bs, n_groups * n_state),
                                state_ssm_im[0].reshape(bs, n_groups * n_state),
                                _s5_step_matrices(*ssm_params))
    x1_s, h2_s, gate_s, idx_s, _, _ = _mix(x_sample.reshape(bs, d), mod_s, attn_t.T, ys_s, u_s, mix_w, bs, 1)

    yb = _moe_experts(xb, block_e, first, n_used, *experts)
    yg = [yb.at[dest.T].get(mode="promise_in_bounds")]
    y_sample = _moe_decode(h2_s, gate_s, idx_s, x1_s, mod_s, g_final, *experts).reshape(bs, 1, d)
    y_prompt = _final(x1_p, mod_p, gate_p, yg, g_final, FINAL_TILE, t).reshape(b, t, d)

    k_win_p = from_hdp(kt_p[:, :, t - keep:].reshape(b, n_heads, HEAD_DIM, keep))
    v_win_p = from_hdp(vt_p[:, :, t - keep:].reshape(b, n_heads, HEAD_DIM, keep))
    st = lambda a, n: a.reshape(1, n, n_groups, n_state)
    return (y_prompt, y_sample, k_win_p, v_win_p, st(hr_p, b), st(hi_p, b),
            from_hdp(ck_new), from_hdp(cv_new), st(hr_s, bs), st(hi_s, bs))
```

```python
import functools

import jax
import jax.numpy as jnp
import numpy as np
from jax import lax
from jax.experimental import pallas as pl
from jax.experimental.pallas import tpu as pltpu

F32 = jnp.float32
BF16 = jnp.bfloat16
HIGHEST = lax.Precision.HIGHEST

HEAD_DIM = 64
DILATED_PATTERNS = ((128, 1), (512, 4), (2048, 16))
ROPE_THETA = 10000.0
PAST_LEN = 8192
SSM_GROUP = 16
SSM_STATE = 64
TOP_K = 4
SWIGLU_LIMIT = 7.0
SWIGLU_ALPHA = 1.702
RMS_EPS = 1e-6

LANES = 128
HEADS_PER_LANE_TILE = LANES // HEAD_DIM
ATTN_BLOCK = 128
ATTN_UNROLL = 32
ATTN_RESIDUES = 16
DECODE_HEADS = 8
SSM_CHUNK = 16
S5_SLAB = 2048
ROW_TILE = 512
MOE_BLOCK = 512
FINAL_TILE = 1024
VMEM_LIMIT = 52 * 1024 * 1024
NEG_BIG = -1e30


def _cparams(*sem):
    return pltpu.CompilerParams(dimension_semantics=sem, vmem_limit_bytes=VMEM_LIMIT)


def _rms(x):
    return x * lax.rsqrt(jnp.mean(x * x, axis=-1, keepdims=True) + RMS_EPS)


def _ada_kernel(c_ref, w_ref, b_ref, o_ref):
    c = c_ref[...]
    s = c * jax.nn.sigmoid(c)
    o_ref[...] = jnp.dot(s, w_ref[...], precision=HIGHEST, preferred_element_type=F32) + b_ref[...]


def _ada(c, w, b):
    n, d = c.shape
    nout = w.shape[1]
    return pl.pallas_call(
        _ada_kernel,
        grid=(nout // d,),
        in_specs=[pl.BlockSpec((n, d), lambda j: (0, 0)),
                  pl.BlockSpec((d, d), lambda j: (0, j)),
                  pl.BlockSpec((1, d), lambda j: (0, j))],
        out_specs=pl.BlockSpec((n, d), lambda j: (0, j)),
        out_shape=jax.ShapeDtypeStruct((n, nout), F32),
        compiler_params=_cparams("arbitrary"),
    )(c, w, b.reshape(1, nout))


def _project(x, mod_ref, g_ref, w_ref, cos_ref, sin_ref, aw):
    h = _rms(x) * g_ref[...]
    h = h * (1.0 + mod_ref[0, 1]) + mod_ref[0, 0]
    proj = jnp.dot(h.astype(BF16), w_ref[...], preferred_element_type=F32)
    cos = cos_ref[...]
    sin = sin_ref[...]
    lane = lax.broadcasted_iota(jnp.int32, (1, aw), 1)
    first_half = (lane & (HEAD_DIM - 1)) < (HEAD_DIM // 2)

    def rope(t):
        rot = jnp.where(first_half, pltpu.roll(t, aw - HEAD_DIM // 2, 1), pltpu.roll(t, HEAD_DIM // 2, 1))
        return t * cos + rot * sin

    q = rope(proj[:, :aw]) * (HEAD_DIM ** -0.5)
    k = rope(proj[:, aw:2 * aw])
    return q, k, proj[:, 2 * aw:3 * aw], proj[:, 3 * aw:]


def _inproj_prompt_kernel(x_ref, mod_ref, g_ref, w_ref, cos_ref, sin_ref, regroup_ref,
                          q_ref, kv_ref, kt_ref, vt_ref, u_ref, *, aw):
    q, k, v, u = _project(x_ref[0], mod_ref, g_ref, w_ref, cos_ref, sin_ref, aw)
    kt_ref[0] = k.T
    vt_ref[0] = v.T
    g = jnp.dot(regroup_ref[...], jnp.concatenate([q, k, v, u], axis=1).astype(BF16), preferred_element_type=F32)
    kv = pltpu.bitcast(g[:, aw:2 * aw], jnp.uint32) | (pltpu.bitcast(g[:, 2 * aw:3 * aw], jnp.uint32) >> 16)
    n_res, rows = q_ref.shape[2], q_ref.shape[3]
    for hp in range(aw // LANES):
        cols = slice(hp * LANES, (hp + 1) * LANES)
        for r in range(n_res):
            q_ref[0, hp, r] = g[r * rows:(r + 1) * rows, cols]
            kv_ref[0, hp, r] = kv[r * rows:(r + 1) * rows, cols]
    u = g[:, 3 * aw:].astype(BF16)
    for lt in range(u_ref.shape[1]):
        u_ref[0, lt] = u[:, lt * LANES:(lt + 1) * LANES]


def _regroup_matrix(tm):
    lc = SSM_CHUNK
    dst = jnp.arange(tm)
    src = (dst % (tm // lc)) * lc + dst // (tm // lc)
    return (src[:, None] == jnp.arange(tm)[None, :]).astype(BF16)


def _inproj_prompt(x, mod, g, w_bf, cos, sin, tm):
    b, t, d = x.shape
    nproj = w_bf.shape[1]
    aw = cos.shape[1]
    sw = nproj - 3 * aw
    n_lt = aw // LANES
    tok = lambda i, j: (j, i, 0)
    hp_major = lambda i, j: (j, 0, i, 0)
    transposed = lambda i, j: (j, 0, i)
    assert tm // SSM_CHUNK * SSM_CHUNK == tm and SSM_CHUNK == ATTN_RESIDUES
    nr = ATTN_RESIDUES
    res_major = lambda i, j: (j, 0, 0, i, 0)
    out_shapes = [jax.ShapeDtypeStruct((b, n_lt, nr, t // nr, LANES), F32),
                  jax.ShapeDtypeStruct((b, n_lt, nr, t // nr, LANES), jnp.uint32),
        jax.ShapeDtypeStruct((b, aw, t), F32), jax.ShapeDtypeStruct((b, aw, t), F32),
        jax.ShapeDtypeStruct((b, sw // LANES, t, LANES), BF16)]
    return pl.pallas_call(
        functools.partial(_inproj_prompt_kernel, aw=aw),
        grid=(t // tm, b),
        in_specs=[pl.BlockSpec((1, tm, d), tok),
                  pl.BlockSpec((1, 6, 1, d), lambda i, j: (j, 0, 0, 0)),
                  pl.BlockSpec((1, d), lambda i, j: (0, 0)),
                  pl.BlockSpec((d, nproj), lambda i, j: (0, 0)),
                  pl.BlockSpec((tm, aw), lambda i, j: (i, 0)),
                  pl.BlockSpec((tm, aw), lambda i, j: (i, 0)),
                  pl.BlockSpec((tm, tm), lambda i, j: (0, 0))],
        out_specs=[pl.BlockSpec((1, n_lt, nr, tm // nr, LANES), res_major)] * 2 + [
            pl.BlockSpec((1, aw, tm), transposed), pl.BlockSpec((1, aw, tm), transposed),
            pl.BlockSpec((1, sw // LANES, tm, LANES), hp_major)],
        out_shape=out_shapes,
        compiler_params=_cparams("arbitrary", "arbitrary"),
    )(x, mod, g, w_bf, cos, sin, _regroup_matrix(tm))


def _inproj_decode_kernel(x_ref, mod_ref, g_ref, w_ref, cos_ref, sin_ref, qt_ref, kt_ref, vt_ref, u_ref, *, aw):
    q, k, v, u = _project(x_ref[...], mod_ref, g_ref, w_ref, cos_ref, sin_ref, aw)
    qt_ref[...] = q.T
    kt_ref[...] = k.T
    vt_ref[...] = v.T
    u_ref[...] = u


def _inproj_decode(x, mod, g, w_bf, cos, sin):
    n, d = x.shape
    nproj = w_bf.shape[1]
    aw = cos.shape[1]
    return pl.pallas_call(
        functools.partial(_inproj_decode_kernel, aw=aw),
        out_shape=[jax.ShapeDtypeStruct((aw, n), F32)] * 3 + [jax.ShapeDtypeStruct((n, nproj - 3 * aw), F32)],
        compiler_params=pltpu.CompilerParams(vmem_limit_bytes=VMEM_LIMIT),
    )(x, mod, g, w_bf, cos, sin)


def _rope_tables(pos, n_heads):
    half = HEAD_DIM // 2
    inv_freq = ROPE_THETA ** (-jnp.arange(half, dtype=F32) / half)
    ang = pos.astype(F32)[:, None] * inv_freq[None, :]
    cos = jnp.cos(ang)
    sin = jnp.sin(ang)
    cos_h = jnp.concatenate([cos, cos], axis=-1)
    sin_h = jnp.concatenate([-sin, sin], axis=-1)
    return jnp.tile(cos_h, (1, n_heads)), jnp.tile(sin_h, (1, n_heads))


def _dilated_attn_kernel(q_ref, kv_ref, o_ref, acc, mrun, lrun):
    t = q_ref.shape[2]
    blk = ATTN_BLOCK
    l16 = t // ATTN_RESIDUES
    lane = lax.broadcasted_iota(jnp.int32, (1, LANES), 1)
    head_mask = [(lane < HEAD_DIM).astype(F32), (lane >= HEAD_DIM).astype(F32)]
    first = lax.broadcasted_iota(jnp.int32, (blk, LANES), 1) < HEAD_DIM
    row = lax.broadcasted_iota(jnp.int32, (blk, 2 * blk), 0)
    col = lax.broadcasted_iota(jnp.int32, (blk, 2 * blk), 1)
    order = sorted(DILATED_PATTERNS, key=lambda p: p[1])
    for pi, (window, dil) in enumerate(order):
        n_keys = window // dil
        n_piece = ATTN_RESIDUES // dil
        pr = blk // n_piece
        nblk = t // (dil * blk)
        member = lambda p: (p % pr) * n_piece + p // pr
        dist = member(row) + blk - ((col // blk) * blk + member(col % blk))
        bias_std = jnp.where(dist >= 0, jnp.where(dist <= n_keys, 0.0, NEG_BIG), NEG_BIG)
        bias_first = jnp.where(col < blk, NEG_BIG, bias_std)

        def block(idx, carry, pi=pi, dil=dil, n_piece=n_piece, pr=pr, nblk=nblk,
                  bias_std=bias_std, bias_first=bias_first):
            c = idx // nblk
            a = idx % nblk
            a_prev = jnp.maximum(a - 1, 0)
            rows_of = lambda blk_i: [pl.ds(pl.multiple_of((c + dil * j) * l16 + blk_i * pr, pr), pr)
                                     for j in range(n_piece)]
            q_rows = rows_of(a)
            gather = lambda ref, rs: jnp.concatenate([ref[0, 0, r, :] for r in rs], axis=0)
            q2 = gather(q_ref, q_rows)
            kv2 = gather(kv_ref, rows_of(a_prev) + q_rows)
            k2 = pltpu.bitcast(kv2 & jnp.uint32(0xFFFF0000), F32).astype(BF16)
            v2 = pltpu.bitcast(kv2 << 16, F32).astype(BF16)
            bias = jnp.where(a == 0, bias_first, bias_std)
            os, ms, ls = [], [], []
            for half in range(HEADS_PER_LANE_TILE):
                qh = (q2 * head_mask[half]).astype(BF16)
                s = lax.dot_general(qh, k2, (((1,), (1,)), ((), ())), preferred_element_type=F32) + bias
                m = jnp.max(s, axis=1, keepdims=True)
                p = jnp.exp(s - m)
                ls.append(jnp.sum(p, axis=1, keepdims=True))
                ms.append(m)
                os.append(jnp.dot(p.astype(BF16), v2, preferred_element_type=F32))
            o_t = jnp.where(first, os[0], os[1])
            m_t = jnp.where(first, ms[0], ms[1])
            l_t = jnp.where(first, ls[0], ls[1])
            load = lambda ref: jnp.concatenate([ref[r, :] for r in q_rows], axis=0)

            def store(ref, val):
                for j, r in enumerate(q_rows):
                    ref[r, :] = val[j * pr:(j + 1) * pr]

            if pi == 0:
                store(acc, o_t)
                store(mrun, m_t)
                store(lrun, l_t)
            else:
                m_o = load(mrun)
                m_n = jnp.maximum(m_o, m_t)
                a_o = jnp.exp(m_o - m_n)
                a_t = jnp.exp(m_t - m_n)
                acc_n = a_o * load(acc) + a_t * o_t
                l_n = a_o * load(lrun) + a_t * l_t
                if pi == len(order) - 1:
                    assert n_piece == 1
                    o_ref[0, 0, q_rows[0], :] = (acc_n / l_n).astype(o_ref.dtype)
                else:
                    store(acc, acc_n)
                    store(mrun, m_n)
                    store(lrun, l_n)
            return carry

        lax.fori_loop(0, dil * nblk, block, 0, unroll=ATTN_UNROLL)


def _dilated_attention(q, kv):
    b, n_lt, t, _ = q.shape
    for window, dil in DILATED_PATTERNS:
        assert window // dil == ATTN_BLOCK and t % (dil * 2 * ATTN_BLOCK) == 0 and ATTN_RESIDUES % dil == 0
    assert max(dl for _, dl in DILATED_PATTERNS) == ATTN_RESIDUES
    spec = pl.BlockSpec((1, 1, t, LANES), lambda i, j: (i, j, 0, 0))
    return pl.pallas_call(
        _dilated_attn_kernel,
        grid=(b, n_lt),
        in_specs=[spec, spec],
        out_specs=spec,
        out_shape=jax.ShapeDtypeStruct((b, n_lt, t, LANES), BF16),
        scratch_shapes=[pltpu.VMEM((t, LANES), F32) for _ in range(3)],
        compiler_params=_cparams("arbitrary", "arbitrary"),
    )(q, kv)


def _decode_kernel(qt_ref, kt_ref, vt_ref, ck_ref, cv_ref, ok_ref, ov_ref, at_ref, *, hb):
    i = pl.program_id(0)
    hg = pl.program_id(1)
    w = ck_ref.shape[3]
    bs = qt_ref.shape[1]
    sel = lax.broadcasted_iota(jnp.int32, (HEAD_DIM, bs), 1) == i

    def column(ref, rs):
        return jnp.sum(jnp.where(sel, ref[rs, :], 0.0), axis=1, keepdims=True)

    pos = lax.broadcasted_iota(jnp.int32, (1, w), 1)
    dist = w - pos
    cnt = jnp.zeros((1, w), F32)
    for window, dil in DILATED_PATTERNS:
        cnt = cnt + jnp.where((dist & (dil - 1)) == 0, jnp.where(dist <= window, 1.0, 0.0), 0.0)
    n_pat = float(len(DILATED_PATTERNS))
    last = lax.broadcasted_iota(jnp.int32, (HEAD_DIM, w), 1) == w - 1

    @pl.when((i == 0) & (hg == 0))
    def _():
        at_ref[...] = jnp.zeros(at_ref.shape, F32)

    for h in range(hb):
        rs = pl.ds(pl.multiple_of((hg * hb + h) * HEAD_DIM, HEAD_DIM), HEAD_DIM)
        qc, kc, vc = column(qt_ref, rs), column(kt_ref, rs), column(vt_ref, rs)
        kk = ck_ref[0, h]
        vv = cv_ref[0, h]
        s = jnp.sum(kk * qc, axis=0, keepdims=True)
        s_new = jnp.sum(kc * qc, axis=0, keepdims=True)
        s = jnp.where(cnt > 0.0, s, NEG_BIG)
        m = jnp.maximum(jnp.max(s, axis=1, keepdims=True), s_new)
        p = cnt * jnp.exp(s - m)
        p_new = n_pat * jnp.exp(s_new - m)
        l = jnp.sum(p, axis=1, keepdims=True) + p_new
        o = (jnp.sum(vv * p, axis=1, keepdims=True) + p_new * vc) / l
        at_ref[rs, :] = jnp.where(sel, o, at_ref[rs, :])
        ok_ref[0, h] = jnp.where(last, kc, pltpu.roll(kk, w - 1, 1))
        ov_ref[0, h] = jnp.where(last, vc, pltpu.roll(vv, w - 1, 1))


def _decode_attention(qt, kt, vt, ck, cv):
    bs, n_heads, hd, w = ck.shape
    aw = n_heads * hd
    for window, dil in DILATED_PATTERNS:
        assert window <= w and dil & (dil - 1) == 0
    hb = DECODE_HEADS
    full = pl.BlockSpec((aw, bs), lambda i, j: (0, 0))
    buf = pl.BlockSpec((1, hb, hd, w), lambda i, j: (i, j, 0, 0))
    return pl.pallas_call(
        functools.partial(_decode_kernel, hb=hb),
        grid=(bs, n_heads // hb),
        in_specs=[full, full, full, buf, buf],
        out_specs=[buf, buf, full],
        out_shape=[jax.ShapeDtypeStruct(ck.shape, F32), jax.ShapeDtypeStruct(cv.shape, F32),
                   jax.ShapeDtypeStruct((aw, bs), F32)],
        compiler_params=_cparams("arbitrary", "arbitrary"),
    )(qt, kt, vt, ck, cv)


def _s5_discretise(lam_re, lam_im, log_dt, b_re, b_im):
    dt = jnp.exp(log_dt)[:, None]
    mag = jnp.exp(lam_re * dt)
    ar = mag * jnp.cos(lam_im * dt)
    ai = mag * jnp.sin(lam_im * dt)
    den = lam_re * lam_re + lam_im * lam_im
    fr = ((ar - 1.0) * lam_re + ai * lam_im) / den
    fi = (ai * lam_re - (ar - 1.0) * lam_im) / den
    bbr = fr[..., None] * b_re - fi[..., None] * b_im
    bbi = fr[..., None] * b_im + fi[..., None] * b_re
    return dt, ar, ai, bbr, bbi


def _s5_chunk_matrices(lam_re, lam_im, log_dt, b_re, b_im, c_re, c_im):
    lc = SSM_CHUNK
    g, p, c = b_re.shape
    dt, _, _, bbr, bbi = _s5_discretise(lam_re, lam_im, log_dt, b_re, b_im)
    kk = jnp.arange(lc + 1, dtype=F32)[:, None, None]
    mag = jnp.exp(kk * lam_re * dt)
    apr = mag * jnp.cos(kk * lam_im * dt)
    api = mag * jnp.sin(kk * lam_im * dt)
    akb_r = apr[:lc, :, :, None] * bbr - api[:lc, :, :, None] * bbi
    akb_i = apr[:lc, :, :, None] * bbi + api[:lc, :, :, None] * bbr
    kern = (jnp.einsum('gop,kgpc->kgoc', c_re, akb_r, precision=HIGHEST)
            - jnp.einsum('gop,kgpc->kgoc', c_im, akb_i, precision=HIGHEST))
    ii = jnp.arange(lc)
    lag = ii[None, :] - ii[:, None]
    toe = jnp.where((lag >= 0)[:, :, None, None, None], kern[jnp.clip(lag, 0, lc - 1)], 0.0)
    toe = toe.transpose(2, 0, 4, 1, 3).reshape(g, lc * c, lc * c)
    rev = lc - 1 - ii
    ws_r = akb_r[rev].transpose(1, 0, 3, 2).reshape(g, lc * c, p)
    ws_i = akb_i[rev].transpose(1, 0, 3, 2).reshape(g, lc * c, p)
    a1r, a1i = apr[1:], api[1:]
    ca_r = c_re[None] * a1r[:, :, None, :] - c_im[None] * a1i[:, :, None, :]
    ca_i = c_re[None] * a1i[:, :, None, :] + c_im[None] * a1r[:, :, None, :]
    wy_r = ca_r.transpose(1, 3, 0, 2).reshape(g, p, lc * c)
    wy_i = (-ca_i).transpose(1, 3, 0, 2).reshape(g, p, lc * c)

    def pair_diag(m):
        r, s = m.shape[1:]
        m2 = m.reshape(g // 2, 2, r, s)
        z = jnp.zeros((g // 2, r, s), m.dtype)
        top = jnp.concatenate([m2[:, 0], z], axis=2)
        bot = jnp.concatenate([z, m2[:, 1]], axis=2)
        return jnp.concatenate([top, bot], axis=1)

    alc_r = apr[lc].reshape(1, g * p)
    alc_i = api[lc].reshape(1, g * p)
    return (toe.astype(BF16), pair_diag(ws_r).astype(BF16), pair_diag(ws_i).astype(BF16),
            pair_diag(wy_r).astype(BF16), pair_diag(wy_i).astype(BF16), alc_r, alc_i)


def _s5_step_matrices(lam_re, lam_im, log_dt, b_re, b_im, c_re, c_im):
    g, p, c = b_re.shape
    _, ar, ai, bbr, bbi = _s5_discretise(lam_re, lam_im, log_dt, b_re, b_im)
    eye = jnp.eye(g, dtype=F32)
    bd_br = jnp.einsum('gpc,gh->gchp', bbr, eye).reshape(g * c, g * p)
    bd_bi = jnp.einsum('gpc,gh->gchp', bbi, eye).reshape(g * c, g * p)
    bd_cr = jnp.einsum('gcp,gh->gphc', c_re, eye).reshape(g * p, g * c)
    bd_ci = jnp.einsum('gcp,gh->gphc', c_im, eye).reshape(g * p, g * c)
    return bd_br, bd_bi, bd_cr, bd_ci, ar.reshape(1, g * p), ai.reshape(1, g * p)


def _s5_prompt_kernel(u_ref, toe_ref, wsr_ref, wsi_ref, wyr_ref, wyi_ref, ar_ref, ai_ref,
                      y_ref, hr_ref, hi_ref, uperm, yperm, sr, si, hr_hist, hi_hist, hcar_r, hcar_i):
    j = pl.program_id(1)
    nc = uperm.shape[0]
    lc, gc = SSM_CHUNK, SSM_GROUP
    gpt = LANES // gc
    n_lt = u_ref.shape[1]
    rb = ROW_TILE // lc
    gw = lc * gc
    lane_grp = lax.broadcasted_iota(jnp.int32, (1, LANES), 1) // gc

    def group_transpose(vs):
        d = 1
        while d < gpt:
            hi = (lane_grp & d) != 0
            nxt = list(vs)
            for a in range(gpt):
                if a & d == 0:
                    nxt[a] = jnp.where(hi, pltpu.roll(vs[a + d], d * gc, 1), vs[a])
                    nxt[a + d] = jnp.where(hi, vs[a + d], pltpu.roll(vs[a], LANES - d * gc, 1))
            vs = nxt
            d *= 2
        return vs

    def permute_in(rc, carry):
        r0 = pl.multiple_of(rc * rb, rb)
        for lt in range(n_lt):
            for h in range(lc // gpt):
                xs = [u_ref[0, lt, pl.ds(pl.multiple_of(r0 * lc + (h * gpt + il) * rb, rb), rb), :].astype(F32)
                      for il in range(gpt)]
                for gl, d in enumerate(group_transpose(xs)):
                    c0 = (lt * gpt + gl) * gw + h * LANES
                    uperm[pl.ds(r0, rb), c0:c0 + LANES] = d.astype(BF16)
        return carry

    lax.fori_loop(0, nc // rb, permute_in, 0, unroll=True)

    n_pairs = wsr_ref.shape[0]
    pw = wsr_ref.shape[1]
    sw = wsr_ref.shape[2]
    for pr in range(n_pairs):
        up = uperm[:, pr * pw:(pr + 1) * pw]
        sr[:, pr * sw:(pr + 1) * sw] = jnp.dot(up, wsr_ref[pr], preferred_element_type=F32)
        si[:, pr * sw:(pr + 1) * sw] = jnp.dot(up, wsi_ref[pr], preferred_element_type=F32)

    ar = ar_ref[...]
    ai = ai_ref[...]
    nstate = ar.shape[1]
    rowid = lax.broadcasted_iota(jnp.int32, (8, nstate), 0)

    @pl.when(j == 0)
    def _():
        hcar_r[...] = jnp.zeros(hcar_r.shape, F32)
        hcar_i[...] = jnp.zeros(hcar_i.shape, F32)

    def eight_chunks(c8, carry):
        h_r, h_i = carry
        base = pl.multiple_of(c8 * 8, 8)
        s_r8 = sr[pl.ds(base, 8), :]
        s_i8 = si[pl.ds(base, 8), :]
        t_r = jnp.zeros((8, nstate), F32)
        t_i = jnp.zeros((8, nstate), F32)
        for r in range(8):
            t_r = jnp.where(rowid == r, h_r, t_r)
            t_i = jnp.where(rowid == r, h_i, t_i)
            n_r = ar * h_r - ai * h_i + s_r8[r:r + 1, :]
            n_i = ar * h_i + ai * h_r + s_i8[r:r + 1, :]
            h_r, h_i = n_r, n_i
        hr_hist[pl.ds(base, 8), :] = t_r
        hi_hist[pl.ds(base, 8), :] = t_i
        return h_r, h_i

    h_r, h_i = lax.fori_loop(0, nc // 8, eight_chunks, (hcar_r[...], hcar_i[...]))
    hcar_r[...] = h_r
    hcar_i[...] = h_i
    hr_ref[0] = h_r
    hi_ref[0] = h_i

    for pr in range(n_pairs):
        st = slice(pr * sw, (pr + 1) * sw)
        y2 = (jnp.dot(hr_hist[:, st].astype(BF16), wyr_ref[pr], preferred_element_type=F32)
              + jnp.dot(hi_hist[:, st].astype(BF16), wyi_ref[pr], preferred_element_type=F32))
        for gg in range(2):
            g = pr * 2 + gg
            cols = slice(g * gw, (g + 1) * gw)
            yi = jnp.dot(uperm[:, cols], toe_ref[g], preferred_element_type=F32)
            yperm[:, cols] = yi + y2[:, gg * gw:(gg + 1) * gw]

    def permute_out(rc, carry):
        r0 = pl.multiple_of(rc * rb, rb)
        for lt in range(n_lt):
            for h in range(lc // gpt):
                ds_ = [yperm[pl.ds(r0, rb), (lt * gpt + gl) * gw + h * LANES:(lt * gpt + gl) * gw + (h + 1) * LANES]
                       for gl in range(gpt)]
                for il, yv in enumerate(group_transpose(ds_)):
                    y_ref[0, lt, pl.ds(pl.multiple_of(r0 * lc + (h * gpt + il) * rb, rb), rb), :] = yv.astype(y_ref.dtype)
        return carry

    lax.fori_loop(0, nc // rb, permute_out, 0, unroll=True)


def _s5_prompt(u, mats):
    b, n_lt, t, _ = u.shape
    wid = n_lt * LANES
    nstate = mats[-1].shape[1]
    ts = min(t, S5_SLAB)
    nc = ts // SSM_CHUNK
    assert t % ts == 0 and ts % ROW_TILE == 0
    full = lambda a: pl.BlockSpec(a.shape, lambda i, j: (0,) * a.ndim)
    return pl.pallas_call(
        _s5_prompt_kernel,
        grid=(b, t // ts),
        in_specs=[pl.BlockSpec((1, n_lt, ts, LANES), lambda i, j: (i, 0, j, 0))] + [full(a) for a in mats],
        out_specs=[pl.BlockSpec((1, n_lt, ts, LANES), lambda i, j: (i, 0, j, 0)),
                   pl.BlockSpec((1, 1, nstate), lambda i, j: (i, 0, 0)),
                   pl.BlockSpec((1, 1, nstate), lambda i, j: (i, 0, 0))],
        out_shape=[jax.ShapeDtypeStruct(u.shape, BF16),
                   jax.ShapeDtypeStruct((b, 1, nstate), F32),
                   jax.ShapeDtypeStruct((b, 1, nstate), F32)],
        scratch_shapes=[pltpu.VMEM((nc, wid * SSM_CHUNK), BF16), pltpu.VMEM((nc, wid * SSM_CHUNK), F32)]
        + [pltpu.VMEM((nc, nstate), F32) for _ in range(4)]
        + [pltpu.VMEM((1, nstate), F32) for _ in range(2)],
        compiler_params=_cparams("arbitrary", "arbitrary"),
    )(u, *mats)


def _s5_step_kernel(u_ref, h0r_ref, h0i_ref, bbr_ref, bbi_ref, cr_ref, ci_ref, ar_ref, ai_ref,
                    y_ref, xr_ref, xi_ref):
    u = u_ref[...]
    ar = ar_ref[...]
    ai = ai_ref[...]
    h0r = h0r_ref[...]
    h0i = h0i_ref[...]
    xr = ar * h0r - ai * h0i + jnp.dot(u, bbr_ref[...], precision=HIGHEST, preferred_element_type=F32)
    xi = ar * h0i + ai * h0r + jnp.dot(u, bbi_ref[...], precision=HIGHEST, preferred_element_type=F32)
    xr_ref[...] = xr
    xi_ref[...] = xi
    y_ref[...] = (jnp.dot(xr, cr_ref[...], precision=HIGHEST, preferred_element_type=F32)
                  - jnp.dot(xi, ci_ref[...], precision=HIGHEST, preferred_element_type=F32))


def _s5_step(u, h0r, h0i, mats):
    n, wid = u.shape
    nstate = h0r.shape[1]
    return pl.pallas_call(
        _s5_step_kernel,
        out_shape=[jax.ShapeDtypeStruct((n, wid), F32),
                   jax.ShapeDtypeStruct((n, nstate), F32),
                   jax.ShapeDtypeStruct((n, nstate), F32)],
        compiler_params=pltpu.CompilerParams(vmem_limit_bytes=VMEM_LIMIT),
    )(u, h0r, h0i, *mats)


def _split_bf16(a):
    hi = a.astype(BF16)
    return hi, (a - hi.astype(F32)).astype(BF16)


def _mix_kernel(x_ref, mod_ref, attn_ref, ys_ref, u_ref, d_ref, wglu_ref, bglu_ref, ga_ref, gs_ref,
                wout_ref, g2_ref, wr_ref, br_ref, ungroup_ref, below_ref,
                x1_ref, h2_ref, gate_ref, idx_ref, rank_ref, count_ref, seen, *, lane_tiled):
    def rows(ref):
        if lane_tiled:
            tile = lambda i: ref[0, i].reshape(-1, LANES)
            return jnp.concatenate([tile(i) for i in range(ref.shape[1])], axis=1).astype(F32)
        return ref[...]

    a_n = (_rms(rows(attn_ref)) * ga_ref[...]).astype(BF16)

    y = rows(ys_ref) + d_ref[...] * rows(u_ref)
    y = 0.5 * y * (1.0 + jnp.tanh(np.float32(np.sqrt(2.0 / np.pi)) * (y + 0.044715 * (y * y * y))))
    z = jnp.dot(y.astype(BF16), wglu_ref[...], preferred_element_type=F32) + bglu_ref[...]
    ssm = y * jax.nn.sigmoid(z)
    s_n = (_rms(ssm) * gs_ref[...]).astype(BF16)
    both = jnp.concatenate([a_n, s_n], axis=1)
    if lane_tiled:
        both = jnp.dot(ungroup_ref[...], both, preferred_element_type=F32).astype(BF16)
    mixed = jnp.dot(both, wout_ref[...], preferred_element_type=F32)
    x1 = x_ref[...] + mod_ref[0, 2] * mixed
    x1_ref[...] = x1
    h2 = _rms(x1) * g2_ref[...] * (1.0 + mod_ref[0, 4]) + mod_ref[0, 3]
    h2_ref[...] = h2.astype(BF16)

    h_hi, h_lo = _split_bf16(h2)
    hh = jnp.dot(h_hi, wr_ref[...], preferred_element_type=F32)
    logits = (hh[:, :LANES] + hh[:, LANES:]
              + jnp.dot(h_lo, wr_ref[:, 0:LANES], preferred_element_type=F32)) + br_ref[...]
    lane = lax.broadcasted_iota(jnp.int32, logits.shape, 1)
    lane_f = lane.astype(F32)
    cur = logits
    vals, idxs = [], []
    for _ in range(TOP_K):
        mx = jnp.max(cur, axis=1, keepdims=True)
        ix = jnp.min(jnp.where(cur == mx, lane_f, float(LANES)), axis=1, keepdims=True)
        vals.append(mx)
        idxs.append(ix)
        cur = jnp.where(lane_f == ix, -jnp.inf, cur)
    exps = [jnp.exp(v - vals[0]) for v in vals]
    den = exps[0]
    for e in exps[1:]:
        den = den + e
    gate = jnp.zeros(logits.shape, F32)
    idx = jnp.zeros(logits.shape, F32)
    for k in range(TOP_K):
        gate = jnp.where(lane == k, exps[k] / den, gate)
        idx = jnp.where(lane == k, idxs[k], idx)
    gate_ref[...] = gate
    idx_ref[...] = idx.astype(jnp.int32)

    @pl.when(pl.program_id(0) == 0)
    def _():
        seen[...] = jnp.zeros(seen.shape, F32)

    chosen = jnp.zeros(logits.shape, F32)
    for k in range(TOP_K):
        chosen = jnp.where(lane_f == idxs[k], 1.0, chosen)
    earlier = jnp.dot(below_ref[...], chosen.astype(BF16), preferred_element_type=F32) + seen[...]
    rank = jnp.zeros(logits.shape, F32)
    for k in range(TOP_K):
        r_k = jnp.sum(jnp.where(lane_f == idxs[k], earlier, 0.0), axis=1, keepdims=True)
        rank = jnp.where(lane == k, r_k, rank)
    rank_ref[...] = rank.astype(jnp.int32)
    seen[...] += jnp.sum(chosen, axis=0, keepdims=True)
    count_ref[...] = seen[...].astype(jnp.int32)


def _mix(x2d, mod, attn, ys, u, weights, tm, rows_per_mod):
    n, d = x2d.shape
    r = mod.shape[2]
    tiles_per_mod = max(rows_per_mod // tm, 1)
    mod_map = (lambda i: (i // tiles_per_mod, 0, 0, 0)) if r == 1 else (lambda i: (0, 0, i, 0))
    rowspec = lambda w: pl.BlockSpec((tm, w), lambda i: (i, 0))
    full = lambda a: pl.BlockSpec(a.shape, lambda i: (0,) * a.ndim)
    lane_tiled = attn.ndim == 5
    below = (jnp.arange(tm)[:, None] > jnp.arange(tm)[None, :]).astype(BF16)
    weights = tuple(weights) + (_regroup_matrix(tm).T, below)
    if lane_tiled:
        def act_spec(a):
            if a.ndim == 5:
                return pl.BlockSpec((1, a.shape[1], a.shape[2], tm // a.shape[2], LANES),
                                    lambda i: (i // tiles_per_mod, 0, 0, i % tiles_per_mod, 0))
            return pl.BlockSpec((1, a.shape[1], tm, LANES), lambda i: (i // tiles_per_mod, 0, i % tiles_per_mod, 0))
    else:
        act_spec = lambda a: rowspec(a.shape[1])
    return pl.pallas_call(
        functools.partial(_mix_kernel, lane_tiled=lane_tiled),
        grid=(n // tm,),
        in_specs=[rowspec(d), pl.BlockSpec((1, 6, r, d), mod_map), act_spec(attn), act_spec(ys), act_spec(u)]
        + [full(w) for w in weights],
        out_specs=[rowspec(d), rowspec(d), rowspec(LANES), rowspec(LANES), rowspec(LANES),
                   pl.BlockSpec((1, LANES), lambda i: (0, 0))],
        out_shape=[jax.ShapeDtypeStruct((n, d), F32), jax.ShapeDtypeStruct((n, d), BF16),
                   jax.ShapeDtypeStruct((n, LANES), F32), jax.ShapeDtypeStruct((n, LANES), jnp.int32),
                   jax.ShapeDtypeStruct((n, LANES), jnp.int32), jax.ShapeDtypeStruct((1, LANES), jnp.int32)],
        scratch_shapes=[pltpu.VMEM((1, LANES), F32)],
        compiler_params=_cparams("arbitrary"),
    )(x2d, mod, attn, ys, u, *weights)


def _moe_kernel(be_ref, first_ref, nb_ref, x_ref, wg_ref, bg_ref, wu_ref, bu_ref, wd_ref, bd_ref, o_ref,
                wg_bf, wu_bf, wd_bf):
    i = pl.program_id(0)

    @pl.when(first_ref[i] == 1)
    def _():
        wg_bf[...] = wg_ref[0].astype(BF16)
        wu_bf[...] = wu_ref[0].astype(BF16)
        wd_bf[...] = wd_ref[0].astype(BF16)

    @pl.when(i < nb_ref[0])
    def _():
        x = x_ref[...]
        g = jnp.dot(x, wg_bf[...], preferred_element_type=F32) + bg_ref[0]
        up = jnp.dot(x, wu_bf[...], preferred_element_type=F32) + bu_ref[0]
        g = jnp.minimum(g, SWIGLU_LIMIT)
        up = jnp.clip(up, -SWIGLU_LIMIT, SWIGLU_LIMIT)
        hid = (up + 1.0) * g * jax.nn.sigmoid(SWIGLU_ALPHA * g)
        y = jnp.dot(hid.astype(BF16), wd_bf[...], preferred_element_type=F32) + bd_ref[0]
        o_ref[...] = y.astype(o_ref.dtype)

    @pl.when(i >= nb_ref[0])
    def _():
        o_ref[...] = jnp.zeros(o_ref.shape, o_ref.dtype)


def _moe_experts(xb, block_e, first, n_used, wg, bg, wu, bu, wd, bd):
    n_rows, d = xb.shape
    e, _, f = wg.shape
    nb = n_rows // MOE_BLOCK
    wmap = lambda i, be, fi, nu: (be[i], 0, 0)
    grid_spec = pltpu.PrefetchScalarGridSpec(
        num_scalar_prefetch=3,
        grid=(nb,),
        in_specs=[pl.BlockSpec((MOE_BLOCK, d), lambda i, be, fi, nu: (i, 0)),
                  pl.BlockSpec((1, d, f), wmap), pl.BlockSpec((1, 1, f), wmap),
                  pl.BlockSpec((1, d, f), wmap), pl.BlockSpec((1, 1, f), wmap),
                  pl.BlockSpec((1, f, d), wmap), pl.BlockSpec((1, 1, d), wmap)],
        out_specs=pl.BlockSpec((MOE_BLOCK, d), lambda i, be, fi, nu: (i, 0)),
        scratch_shapes=[pltpu.VMEM((d, f), BF16), pltpu.VMEM((d, f), BF16), pltpu.VMEM((f, d), BF16)],
    )
    return pl.pallas_call(
        _moe_kernel,
        grid_spec=grid_spec,
        out_shape=jax.ShapeDtypeStruct((n_rows, d), BF16),
        compiler_params=_cparams("arbitrary"),
    )(block_e, first, n_used, xb, wg, bg.reshape(e, 1, f), wu, bu.reshape(e, 1, f), wd, bd.reshape(e, 1, d))


def _final_kernel(x1_ref, mod_ref, gate_ref, *rest):
    yg_refs, (gf_ref, o_ref) = rest[:-2], rest[-2:]
    gate = gate_ref[...]
    acc = jnp.zeros(x1_ref.shape, F32)
    k = 0
    for yg_ref in yg_refs:
        for j in range(yg_ref.shape[0]):
            acc = acc + gate[:, k:k + 1] * yg_ref[j].astype(F32)
            k += 1
    x = x1_ref[...] + mod_ref[0, 5] * acc
    o_ref[...] = _rms(x) * gf_ref[...]


def _final(x1, mod, gates, yg_parts, g_final, tm, rows_per_mod):
    n, d = x1.shape
    tiles_per_mod = rows_per_mod // tm
    assert sum(p.shape[0] for p in yg_parts) == TOP_K
    return pl.pallas_call(
        _final_kernel,
        grid=(n // tm,),
        in_specs=[pl.BlockSpec((tm, d), lambda i: (i, 0)),
                  pl.BlockSpec((1, 6, 1, d), lambda i: (i // tiles_per_mod, 0, 0, 0)),
                  pl.BlockSpec((tm, LANES), lambda i: (i, 0))]
        + [pl.BlockSpec((p.shape[0], tm, d), lambda i: (0, i, 0)) for p in yg_parts]
        + [pl.BlockSpec((1, d), lambda i: (0, 0))],
        out_specs=pl.BlockSpec((tm, d), lambda i: (i, 0)),
        out_shape=jax.ShapeDtypeStruct((n, d), F32),
        compiler_params=_cparams("arbitrary"),
    )(x1, mod, gates, *yg_parts, g_final.reshape(1, d))


def _moe_decode_kernel(h2_ref, gate_ref, idx_ref, x1_ref, mod_ref, gf_ref,
                       wg_ref, bg_ref, wu_ref, bu_ref, wd_ref, bd_ref, o_ref, acc):
    e = pl.program_id(0)

    @pl.when(e == 0)
    def _():
        acc[...] = jnp.zeros(acc.shape, F32)

    gate_e = jnp.sum(jnp.where(idx_ref[...] == e, gate_ref[...], 0.0), axis=1, keepdims=True)
    x = h2_ref[...]
    g = jnp.dot(x, wg_ref[0].astype(BF16), preferred_element_type=F32) + bg_ref[0]
    up = jnp.dot(x, wu_ref[0].astype(BF16), preferred_element_type=F32) + bu_ref[0]
    g = jnp.minimum(g, SWIGLU_LIMIT)
    up = jnp.clip(up, -SWIGLU_LIMIT, SWIGLU_LIMIT)
    hid = (up + 1.0) * g * jax.nn.sigmoid(SWIGLU_ALPHA * g)
    y = jnp.dot(hid.astype(BF16), wd_ref[0].astype(BF16), preferred_element_type=F32) + bd_ref[0]
    acc[...] += gate_e * y

    @pl.when(e == pl.num_programs(0) - 1)
    def _():
        x1 = x1_ref[...] + mod_ref[0, 5] * acc[...]
        o_ref[...] = _rms(x1) * gf_ref[...]


def _moe_decode(h2, gates, idx, x1, mod, g_final, wg, bg, wu, bu, wd, bd):
    n, d = x1.shape
    e, _, f = wg.shape
    full = lambda a: pl.BlockSpec(a.shape, lambda i: (0,) * a.ndim)
    wmap = lambda i: (i, 0, 0)
    return pl.pallas_call(
        _moe_decode_kernel,
        grid=(e,),
        in_specs=[full(h2), full(gates), full(idx), full(x1), full(mod), pl.BlockSpec((1, d), lambda i: (0, 0)),
                  pl.BlockSpec((1, d, f), wmap), pl.BlockSpec((1, 1, f), wmap),
                  pl.BlockSpec((1, d, f), wmap), pl.BlockSpec((1, 1, f), wmap),
                  pl.BlockSpec((1, f, d), wmap), pl.BlockSpec((1, 1, d), wmap)],
        out_specs=pl.BlockSpec((n, d), lambda i: (0, 0)),
        out_shape=jax.ShapeDtypeStruct((n, d), F32),
        scratch_shapes=[pltpu.VMEM((n, d), F32)],
        compiler_params=_cparams("arbitrary"),
    )(h2, gates, idx, x1, mod, g_final.reshape(1, d), wg, bg.reshape(e, 1, f), wu, bu.reshape(e, 1, f),
      wd, bd.reshape(e, 1, d))


def _dispatch(top_e, rank, counts):
    n_exp = counts.shape[0]
    n_assign = top_e.shape[0] * TOP_K
    padded = (counts + MOE_BLOCK - 1) // MOE_BLOCK * MOE_BLOCK
    pend = jnp.cumsum(padded)
    pstart = pend - padded
    dest = pstart[top_e] + rank
    nb = -(-n_assign // MOE_BLOCK) + n_exp
    block_start = jnp.arange(nb, dtype=jnp.int32) * MOE_BLOCK
    block_e = jnp.minimum(jnp.sum((pend[None, :] <= block_start[:, None]).astype(jnp.int32), axis=1), n_exp - 1)
    bits = max(n_assign - 1, 1).bit_length()
    assert (n_exp << bits) < 2 ** 31
    packed = (top_e.reshape(-1) << bits) | jnp.arange(n_assign, dtype=jnp.int32)
    order = jnp.sort(packed) & ((1 << bits) - 1)
    ustart = jnp.cumsum(counts) - counts
    first_src = ustart[block_e] + block_start - pstart[block_e]
    last_src = ustart[block_e] + counts[block_e] - 1
    src = jnp.minimum(first_src[:, None] + jnp.arange(MOE_BLOCK, dtype=jnp.int32)[None, :], last_src[:, None])
    row_tok = order[jnp.clip(src.reshape(-1), 0, n_assign - 1)] // TOP_K
    first = jnp.concatenate([jnp.ones((1,), jnp.int32), (block_e[1:] != block_e[:-1]).astype(jnp.int32)])
    n_used = (pend[-1] // MOE_BLOCK).astype(jnp.int32).reshape(1)
    return row_tok, dest, block_e, first, n_used


def kernel(x_prompt, x_sample, c_prompt, c_sample, cache_k, cache_v, state_ssm_re, state_ssm_im, w_ada, b_ada, g_norm1, g_norm2, w_in, lambda_re, lambda_im, log_dt, b_ssm_re, b_ssm_im, c_ssm_re, c_ssm_im, d_ssm, w_glu, b_glu, g_attn_out, g_ssm_out, w_out, w_router, b_router, w_gate, b_gate, w_up, b_up, w_down, b_down, g_final):
    assert w_ada.shape[0] == 1, "one layer"
    b, t, d = x_prompt.shape
    bs = x_sample.shape[0]
    assert x_sample.shape[1] == 1
    wbuf, n_heads = cache_k.shape[2], cache_k.shape[3]
    aw = n_heads * HEAD_DIM
    n_groups, n_state = lambda_re.shape[1:]
    sw = n_groups * SSM_GROUP
    n_exp = w_router.shape[2]
    keep = min(max(w for w, _ in DILATED_PATTERNS), t)
    tm = min(ROW_TILE, t)

    mod = _ada(jnp.concatenate([c_prompt, c_sample], axis=0), w_ada[0], b_ada[0])
    mod_p = mod[:b].reshape(b, 6, 1, d)
    mod_s = mod[b:].reshape(bs, 6, d).transpose(1, 0, 2)[None]

    w_in_bf = w_in[0].astype(BF16)
    g1 = g_norm1[0].reshape(1, d)
    ssm_params = (lambda_re[0], lambda_im[0], log_dt[0], b_ssm_re[0], b_ssm_im[0], c_ssm_re[0], c_ssm_im[0])
    wr_pad = jnp.zeros((d, LANES), F32).at[:, :n_exp].set(w_router[0])
    wr_hi = wr_pad.astype(BF16)
    wr_lo = (wr_pad - wr_hi.astype(F32)).astype(BF16)
    br_pad = jnp.full((1, LANES), NEG_BIG, F32).at[0, :n_exp].set(b_router[0])
    mix_w = (d_ssm[0].reshape(1, sw), w_glu[0].astype(BF16), b_glu[0].reshape(1, sw),
             g_attn_out[0].reshape(1, aw), g_ssm_out[0].reshape(1, sw), w_out[0].astype(BF16),
             g_norm2[0].reshape(1, d), jnp.concatenate([wr_hi, wr_lo], axis=1), br_pad)
    experts = (w_gate[0], b_gate[0], w_up[0], b_up[0], w_down[0], b_down[0])

    cos_p, sin_p = _rope_tables(jnp.arange(t), n_heads)
    q_p, kv_p, kt_p, vt_p, u_p = _inproj_prompt(x_prompt, mod_p, g1, w_in_bf, cos_p, sin_p, tm)
    flat_t = lambda a: a.reshape(b, a.shape[1], t, LANES)
    attn_p = _dilated_attention(flat_t(q_p), flat_t(kv_p)).reshape(q_p.shape)
    ys_p, hr_p, hi_p = _s5_prompt(u_p, _s5_chunk_matrices(*ssm_params))
    x1_p, h2_p, gate_p, idx_p, rank_p, count_p = _mix(x_prompt.reshape(b * t, d), mod_p, attn_p, ys_p, u_p,
                                                      mix_w, tm, t)

    row_tok, dest, block_e, first, n_used = _dispatch(idx_p[:, :TOP_K], rank_p[:, :TOP_K], count_p[0, :n_exp])
    xb = h2_p.at[row_tok].get(mode="promise_in_bounds")

    cos_s, sin_s = _rope_tables(jnp.full((1,), PAST_LEN), n_heads)
    qt_s, kt_s, vt_s, u_s = _inproj_decode(x_sample.reshape(bs, d), mod_s, g1, w_in_bf, cos_s, sin_s)
    to_hdp = lambda c: jnp.transpose(c[0], (0, 2, 3, 1))
    from_hdp = lambda c: jnp.transpose(c, (0, 3, 1, 2))[None]
    ck_new, cv_new, attn_t = _decode_attention(qt_s, kt_s, vt_s, to_hdp(cache_k), to_hdp(cache_v))
    ys_s, hr_s, hi_s = _s5_step(u_s, state_ssm_re[0].reshape(bs, n_groups * n_state),
                                state_ssm_im[0].reshape(bs, n_groups * n_state),
                                _s5_step_matrices(*ssm_params))
    x1_s, h2_s, gate_s, idx_s, _, _ = _mix(x_sample.reshape(bs, d), mod_s, attn_t.T, ys_s, u_s, mix_w, bs, 1)

    yb = _moe_experts(xb, block_e, first, n_used, *experts)
    yg = [yb.at[dest.T].get(mode="promise_in_bounds")]
    y_sample = _moe_decode(h2_s, gate_s, idx_s, x1_s, mod_s, g_final, *experts).reshape(bs, 1, d)
    y_prompt = _final(x1_p, mod_p, gate_p, yg, g_final, FINAL_TILE, t).reshape(b, t, d)

    k_win_p = from_hdp(kt_p[:, :, t - keep:].reshape(b, n_heads, HEAD_DIM, keep))
    v_win_p = from_hdp(vt_p[:, :, t - keep:].reshape(b, n_heads, HEAD_DIM, keep))
    st = lambda a, n: a.reshape(1, n, n_groups, n_state)
    return (y_prompt, y_sample, k_win_p, v_win_p, st(hr_p, b), st(hi_p, b),
            from_hdp(ck_new), from_hdp(cv_new), st(hr_s, bs), st(hi_s, bs))
```

```python
import functools

import jax
import jax.numpy as jnp
import numpy as np
from jax import lax
from jax.experimental import pallas as pl
from jax.experimental.pallas import tpu as pltpu

F32 = jnp.float32
BF16 = jnp.bfloat16
HIGHEST = lax.Precision.HIGHEST

HEAD_DIM = 64
DILATED_PATTERNS = ((128, 1), (512, 4), (2048, 16))
ROPE_THETA = 10000.0
PAST_LEN = 8192
SSM_GROUP = 16
SSM_STATE = 64
TOP_K = 4
SWIGLU_LIMIT = 7.0
SWIGLU_ALPHA = 1.702
RMS_EPS = 1e-6

LANES = 128
HEADS_PER_LANE_TILE = LANES // HEAD_DIM
ATTN_BLOCK = 128
ATTN_UNROLL = 32
ATTN_RESIDUES = 16
DECODE_HEADS = 8
SSM_CHUNK = 16
S5_SLAB = 2048
ROW_TILE = 512
MOE_BLOCK = 512
FINAL_TILE = 1024
VMEM_LIMIT = 52 * 1024 * 1024
NEG_BIG = -1e30


def _cparams(*sem):
    return pltpu.CompilerParams(dimension_semantics=sem, vmem_limit_bytes=VMEM_LIMIT)


def _rms(x):
    return x * lax.rsqrt(jnp.mean(x * x, axis=-1, keepdims=True) + RMS_EPS)


def _ada_kernel(c_ref, w_ref, b_ref, o_ref):
    c = c_ref[...]
    s = c * jax.nn.sigmoid(c)
    o_ref[...] = jnp.dot(s, w_ref[...], precision=HIGHEST, preferred_element_type=F32) + b_ref[...]


def _ada(c, w, b):
    n, d = c.shape
    nout = w.shape[1]
    return pl.pallas_call(
        _ada_kernel,
        grid=(nout // d,),
        in_specs=[pl.BlockSpec((n, d), lambda j: (0, 0)),
                  pl.BlockSpec((d, d), lambda j: (0, j)),
                  pl.BlockSpec((1, d), lambda j: (0, j))],
        out_specs=pl.BlockSpec((n, d), lambda j: (0, j)),
        out_shape=jax.ShapeDtypeStruct((n, nout), F32),
        compiler_params=_cparams("arbitrary"),
    )(c, w, b.reshape(1, nout))


def _project(x, mod_ref, g_ref, w_ref, cos_ref, sin_ref, aw):
    h = _rms(x) * g_ref[...]
    h = h * (1.0 + mod_ref[0, 1]) + mod_ref[0, 0]
    proj = jnp.dot(h.astype(BF16), w_ref[...], preferred_element_type=F32)
    cos = cos_ref[...]
    sin = sin_ref[...]
    lane = lax.broadcasted_iota(jnp.int32, (1, aw), 1)
    first_half = (lane & (HEAD_DIM - 1)) < (HEAD_DIM // 2)

    def rope(t):
        rot = jnp.where(first_half, pltpu.roll(t, aw - HEAD_DIM // 2, 1), pltpu.roll(t, HEAD_DIM // 2, 1))
        return t * cos + rot * sin

    q = rope(proj[:, :aw]) * (HEAD_DIM ** -0.5)
    k = rope(proj[:, aw:2 * aw])
    return q, k, proj[:, 2 * aw:3 * aw], proj[:, 3 * aw:]


def _inproj_prompt_kernel(x_ref, mod_ref, g_ref, w_ref, cos_ref, sin_ref, regroup_ref,
                          q_ref, kv_ref, kt_ref, vt_ref, u_ref, *, aw):
    q, k, v, u = _project(x_ref[0], mod_ref, g_ref, w_ref, cos_ref, sin_ref, aw)
    kt_ref[0, 0] = k.T
    vt_ref[0, 0] = v.T
    g = jnp.dot(regroup_ref[...], jnp.concatenate([q, k, v, u], axis=1).astype(BF16), preferred_element_type=F32)
    kv = pltpu.bitcast(g[:, aw:2 * aw], jnp.uint32) | (pltpu.bitcast(g[:, 2 * aw:3 * aw], jnp.uint32) >> 16)
    n_res, rows = q_ref.shape[2], q_ref.shape[3]
    for hp in range(aw // LANES):
        cols = slice(hp * LANES, (hp + 1) * LANES)
        for r in range(n_res):
            q_ref[0, hp, r] = g[r * rows:(r + 1) * rows, cols]
            kv_ref[0, hp, r] = kv[r * rows:(r + 1) * rows, cols]
    u = g[:, 3 * aw:].astype(BF16)
    for lt in range(u_ref.shape[1]):
        u_ref[0, lt] = u[:, lt * LANES:(lt + 1) * LANES]


def _regroup_matrix(tm):
    lc = SSM_CHUNK
    dst = jnp.arange(tm)
    src = (dst % (tm // lc)) * lc + dst // (tm // lc)
    return (src[:, None] == jnp.arange(tm)[None, :]).astype(BF16)


def _inproj_prompt(x, mod, g, w_bf, cos, sin, tm, keep):
    b, t, d = x.shape
    nproj = w_bf.shape[1]
    aw = cos.shape[1]
    sw = nproj - 3 * aw
    n_lt = aw // LANES
    tok = lambda i, j: (j, i, 0)
    hp_major = lambda i, j: (j, 0, i, 0)
    assert t % keep == 0 and keep % tm == 0
    tiles_per_win = keep // tm
    transposed = lambda i, j: (i // tiles_per_win, j, 0, i % tiles_per_win)
    assert tm // SSM_CHUNK * SSM_CHUNK == tm and SSM_CHUNK == ATTN_RESIDUES
    nr = ATTN_RESIDUES
    res_major = lambda i, j: (j, 0, 0, i, 0)
    out_shapes = [jax.ShapeDtypeStruct((b, n_lt, nr, t // nr, LANES), F32),
                  jax.ShapeDtypeStruct((b, n_lt, nr, t // nr, LANES), jnp.uint32),
        jax.ShapeDtypeStruct((t // keep, b, aw, keep), F32), jax.ShapeDtypeStruct((t // keep, b, aw, keep), F32),
        jax.ShapeDtypeStruct((b, sw // LANES, t, LANES), BF16)]
    return pl.pallas_call(
        functools.partial(_inproj_prompt_kernel, aw=aw),
        grid=(t // tm, b),
        in_specs=[pl.BlockSpec((1, tm, d), tok),
                  pl.BlockSpec((1, 6, 1, d), lambda i, j: (j, 0, 0, 0)),
                  pl.BlockSpec((1, d), lambda i, j: (0, 0)),
                  pl.BlockSpec((d, nproj), lambda i, j: (0, 0)),
                  pl.BlockSpec((tm, aw), lambda i, j: (i, 0)),
                  pl.BlockSpec((tm, aw), lambda i, j: (i, 0)),
                  pl.BlockSpec((tm, tm), lambda i, j: (0, 0))],
        out_specs=[pl.BlockSpec((1, n_lt, nr, tm // nr, LANES), res_major)] * 2 + [
            pl.BlockSpec((1, 1, aw, tm), transposed), pl.BlockSpec((1, 1, aw, tm), transposed),
            pl.BlockSpec((1, sw // LANES, tm, LANES), hp_major)],
        out_shape=out_shapes,
        compiler_params=_cparams("arbitrary", "arbitrary"),
    )(x, mod, g, w_bf, cos, sin, _regroup_matrix(tm))


def _inproj_decode_kernel(x_ref, mod_ref, g_ref, w_ref, cos_ref, sin_ref, qt_ref, kt_ref, vt_ref, u_ref, *, aw):
    q, k, v, u = _project(x_ref[...], mod_ref, g_ref, w_ref, cos_ref, sin_ref, aw)
    qt_ref[...] = q.T
    kt_ref[...] = k.T
    vt_ref[...] = v.T
    u_ref[...] = u


def _inproj_decode(x, mod, g, w_bf, cos, sin):
    n, d = x.shape
    nproj = w_bf.shape[1]
    aw = cos.shape[1]
    return pl.pallas_call(
        functools.partial(_inproj_decode_kernel, aw=aw),
        out_shape=[jax.ShapeDtypeStruct((aw, n), F32)] * 3 + [jax.ShapeDtypeStruct((n, nproj - 3 * aw), F32)],
        compiler_params=pltpu.CompilerParams(vmem_limit_bytes=VMEM_LIMIT),
    )(x, mod, g, w_bf, cos, sin)


def _rope_tables(pos, n_heads):
    half = HEAD_DIM // 2
    inv_freq = ROPE_THETA ** (-jnp.arange(half, dtype=F32) / half)
    ang = pos.astype(F32)[:, None] * inv_freq[None, :]
    cos = jnp.cos(ang)
    sin = jnp.sin(ang)
    cos_h = jnp.concatenate([cos, cos], axis=-1)
    sin_h = jnp.concatenate([-sin, sin], axis=-1)
    return jnp.tile(cos_h, (1, n_heads)), jnp.tile(sin_h, (1, n_heads))


def _dilated_attn_kernel(q_ref, kv_ref, o_ref, acc, mrun, lrun):
    t = q_ref.shape[2]
    blk = ATTN_BLOCK
    l16 = t // ATTN_RESIDUES
    lane = lax.broadcasted_iota(jnp.int32, (1, LANES), 1)
    head_mask = [(lane < HEAD_DIM).astype(F32), (lane >= HEAD_DIM).astype(F32)]
    first = lax.broadcasted_iota(jnp.int32, (blk, LANES), 1) < HEAD_DIM
    row = lax.broadcasted_iota(jnp.int32, (blk, 2 * blk), 0)
    col = lax.broadcasted_iota(jnp.int32, (blk, 2 * blk), 1)
    order = sorted(DILATED_PATTERNS, key=lambda p: p[1])
    for pi, (window, dil) in enumerate(order):
        n_keys = window // dil
        n_piece = ATTN_RESIDUES // dil
        pr = blk // n_piece
        nblk = t // (dil * blk)
        member = lambda p: (p % pr) * n_piece + p // pr
        dist = member(row) + blk - ((col // blk) * blk + member(col % blk))
        bias_std = jnp.where(dist >= 0, jnp.where(dist <= n_keys, 0.0, NEG_BIG), NEG_BIG)
        bias_first = jnp.where(col < blk, NEG_BIG, bias_std)

        def block(idx, carry, pi=pi, dil=dil, n_piece=n_piece, pr=pr, nblk=nblk,
                  bias_std=bias_std, bias_first=bias_first):
            c = idx // nblk
            a = idx % nblk
            a_prev = jnp.maximum(a - 1, 0)
            rows_of = lambda blk_i: [pl.ds(pl.multiple_of((c + dil * j) * l16 + blk_i * pr, pr), pr)
                                     for j in range(n_piece)]
            q_rows = rows_of(a)
            gather = lambda ref, rs: jnp.concatenate([ref[0, 0, r, :] for r in rs], axis=0)
            q2 = gather(q_ref, q_rows)
            kv2 = gather(kv_ref, rows_of(a_prev) + q_rows)
            k2 = pltpu.bitcast(kv2 & jnp.uint32(0xFFFF0000), F32).astype(BF16)
            v2 = pltpu.bitcast(kv2 << 16, F32).astype(BF16)
            bias = jnp.where(a == 0, bias_first, bias_std)
            os, ms, ls = [], [], []
            for half in range(HEADS_PER_LANE_TILE):
                qh = (q2 * head_mask[half]).astype(BF16)
                s = lax.dot_general(qh, k2, (((1,), (1,)), ((), ())), preferred_element_type=F32) + bias
                m = jnp.max(s, axis=1, keepdims=True)
                p = jnp.exp(s - m)
                ls.append(jnp.sum(p, axis=1, keepdims=True))
                ms.append(m)
                os.append(jnp.dot(p.astype(BF16), v2, preferred_element_type=F32))
            o_t = jnp.where(first, os[0], os[1])
            m_t = jnp.where(first, ms[0], ms[1])
            l_t = jnp.where(first, ls[0], ls[1])
            load = lambda ref: jnp.concatenate([ref[r, :] for r in q_rows], axis=0)

            def store(ref, val):
                for j, r in enumerate(q_rows):
                    ref[r, :] = val[j * pr:(j + 1) * pr]

            if pi == 0:
                store(acc, o_t)
                store(mrun, m_t)
                store(lrun, l_t)
            else:
                m_o = load(mrun)
                m_n = jnp.maximum(m_o, m_t)
                a_o = jnp.exp(m_o - m_n)
                a_t = jnp.exp(m_t - m_n)
                acc_n = a_o * load(acc) + a_t * o_t
                l_n = a_o * load(lrun) + a_t * l_t
                if pi == len(order) - 1:
                    assert n_piece == 1
                    o_ref[0, 0, q_rows[0], :] = (acc_n / l_n).astype(o_ref.dtype)
                else:
                    store(acc, acc_n)
                    store(mrun, m_n)
                    store(lrun, l_n)
            return carry

        lax.fori_loop(0, dil * nblk, block, 0, unroll=ATTN_UNROLL)


def _dilated_attention(q, kv):
    b, n_lt, t, _ = q.shape
    for window, dil in DILATED_PATTERNS:
        assert window // dil == ATTN_BLOCK and t % (dil * 2 * ATTN_BLOCK) == 0 and ATTN_RESIDUES % dil == 0
    assert max(dl for _, dl in DILATED_PATTERNS) == ATTN_RESIDUES
    spec = pl.BlockSpec((1, 1, t, LANES), lambda i, j: (i, j, 0, 0))
    return pl.pallas_call(
        _dilated_attn_kernel,
        grid=(b, n_lt),
        in_specs=[spec, spec],
        out_specs=spec,
        out_shape=jax.ShapeDtypeStruct((b, n_lt, t, LANES), BF16),
        scratch_shapes=[pltpu.VMEM((t, LANES), F32) for _ in range(3)],
        compiler_params=_cparams("arbitrary", "arbitrary"),
    )(q, kv)


def _decode_kernel(qt_ref, kt_ref, vt_ref, ck_ref, cv_ref, ok_ref, ov_ref, at_ref, *, hb):
    i = pl.program_id(0)
    hg = pl.program_id(1)
    w = ck_ref.shape[3]
    bs = qt_ref.shape[1]
    sel = lax.broadcasted_iota(jnp.int32, (HEAD_DIM, bs), 1) == i

    def column(ref, rs):
        return jnp.sum(jnp.where(sel, ref[rs, :], 0.0), axis=1, keepdims=True)

    pos = lax.broadcasted_iota(jnp.int32, (1, w), 1)
    dist = w - pos
    cnt = jnp.zeros((1, w), F32)
    for window, dil in DILATED_PATTERNS:
        cnt = cnt + jnp.where((dist & (dil - 1)) == 0, jnp.where(dist <= window, 1.0, 0.0), 0.0)
    n_pat = float(len(DILATED_PATTERNS))
    last = lax.broadcasted_iota(jnp.int32, (HEAD_DIM, w), 1) == w - 1

    @pl.when((i == 0) & (hg == 0))
    def _():
        at_ref[...] = jnp.zeros(at_ref.shape, F32)

    for h in range(hb):
        rs = pl.ds(pl.multiple_of((hg * hb + h) * HEAD_DIM, HEAD_DIM), HEAD_DIM)
        qc, kc, vc = column(qt_ref, rs), column(kt_ref, rs), column(vt_ref, rs)
        kk = ck_ref[0, h]
        vv = cv_ref[0, h]
        s = jnp.sum(kk * qc, axis=0, keepdims=True)
        s_new = jnp.sum(kc * qc, axis=0, keepdims=True)
        s = jnp.where(cnt > 0.0, s, NEG_BIG)
        m = jnp.maximum(jnp.max(s, axis=1, keepdims=True), s_new)
        p = cnt * jnp.exp(s - m)
        p_new = n_pat * jnp.exp(s_new - m)
        l = jnp.sum(p, axis=1, keepdims=True) + p_new
        o = (jnp.sum(vv * p, axis=1, keepdims=True) + p_new * vc) / l
        at_ref[rs, :] = jnp.where(sel, o, at_ref[rs, :])
        ok_ref[0, h] = jnp.where(last, kc, pltpu.roll(kk, w - 1, 1))
        ov_ref[0, h] = jnp.where(last, vc, pltpu.roll(vv, w - 1, 1))


def _decode_attention(qt, kt, vt, ck, cv):
    bs, n_heads, hd, w = ck.shape
    aw = n_heads * hd
    for window, dil in DILATED_PATTERNS:
        assert window <= w and dil & (dil - 1) == 0
    hb = DECODE_HEADS
    full = pl.BlockSpec((aw, bs), lambda i, j: (0, 0))
    buf = pl.BlockSpec((1, hb, hd, w), lambda i, j: (i, j, 0, 0))
    return pl.pallas_call(
        functools.partial(_decode_kernel, hb=hb),
        grid=(bs, n_heads // hb),
        in_specs=[full, full, full, buf, buf],
        out_specs=[buf, buf, full],
        out_shape=[jax.ShapeDtypeStruct(ck.shape, F32), jax.ShapeDtypeStruct(cv.shape, F32),
                   jax.ShapeDtypeStruct((aw, bs), F32)],
        compiler_params=_cparams("arbitrary", "arbitrary"),
    )(qt, kt, vt, ck, cv)


def _s5_discretise(lam_re, lam_im, log_dt, b_re, b_im):
    dt = jnp.exp(log_dt)[:, None]
    mag = jnp.exp(lam_re * dt)
    ar = mag * jnp.cos(lam_im * dt)
    ai = mag * jnp.sin(lam_im * dt)
    den = lam_re * lam_re + lam_im * lam_im
    fr = ((ar - 1.0) * lam_re + ai * lam_im) / den
    fi = (ai * lam_re - (ar - 1.0) * lam_im) / den
    bbr = fr[..., None] * b_re - fi[..., None] * b_im
    bbi = fr[..., None] * b_im + fi[..., None] * b_re
    return dt, ar, ai, bbr, bbi


def _s5_chunk_matrices(lam_re, lam_im, log_dt, b_re, b_im, c_re, c_im):
    lc = SSM_CHUNK
    g, p, c = b_re.shape
    dt, _, _, bbr, bbi = _s5_discretise(lam_re, lam_im, log_dt, b_re, b_im)
    kk = jnp.arange(lc + 1, dtype=F32)[:, None, None]
    mag = jnp.exp(kk * lam_re * dt)
    apr = mag * jnp.cos(kk * lam_im * dt)
    api = mag * jnp.sin(kk * lam_im * dt)
    akb_r = apr[:lc, :, :, None] * bbr - api[:lc, :, :, None] * bbi
    akb_i = apr[:lc, :, :, None] * bbi + api[:lc, :, :, None] * bbr
    kern = (jnp.einsum('gop,kgpc->kgoc', c_re, akb_r, precision=HIGHEST)
            - jnp.einsum('gop,kgpc->kgoc', c_im, akb_i, precision=HIGHEST))
    ii = jnp.arange(lc)
    lag = ii[None, :] - ii[:, None]
    toe = jnp.where((lag >= 0)[:, :, None, None, None], kern[jnp.clip(lag, 0, lc - 1)], 0.0)
    toe = toe.transpose(2, 0, 4, 1, 3).reshape(g, lc * c, lc * c)
    rev = lc - 1 - ii
    ws_r = akb_r[rev].transpose(1, 0, 3, 2).reshape(g, lc * c, p)
    ws_i = akb_i[rev].transpose(1, 0, 3, 2).reshape(g, lc * c, p)
    a1r, a1i = apr[1:], api[1:]
    ca_r = c_re[None] * a1r[:, :, None, :] - c_im[None] * a1i[:, :, None, :]
    ca_i = c_re[None] * a1i[:, :, None, :] + c_im[None] * a1r[:, :, None, :]
    wy_r = ca_r.transpose(1, 3, 0, 2).reshape(g, p, lc * c)
    wy_i = (-ca_i).transpose(1, 3, 0, 2).reshape(g, p, lc * c)

    def pair_diag(m):
        r, s = m.shape[1:]
        m2 = m.reshape(g // 2, 2, r, s)
        z = jnp.zeros((g // 2, r, s), m.dtype)
        top = jnp.concatenate([m2[:, 0], z], axis=2)
        bot = jnp.concatenate([z, m2[:, 1]], axis=2)
        return jnp.concatenate([top, bot], axis=1)

    alc_r = apr[lc].reshape(1, g * p)
    alc_i = api[lc].reshape(1, g * p)
    return (toe.astype(BF16), pair_diag(ws_r).astype(BF16), pair_diag(ws_i).astype(BF16),
            pair_diag(wy_r).astype(BF16), pair_diag(wy_i).astype(BF16), alc_r, alc_i)


def _s5_step_matrices(lam_re, lam_im, log_dt, b_re, b_im, c_re, c_im):
    g, p, c = b_re.shape
    _, ar, ai, bbr, bbi = _s5_discretise(lam_re, lam_im, log_dt, b_re, b_im)
    eye = jnp.eye(g, dtype=F32)
    bd_br = jnp.einsum('gpc,gh->gchp', bbr, eye).reshape(g * c, g * p)
    bd_bi = jnp.einsum('gpc,gh->gchp', bbi, eye).reshape(g * c, g * p)
    bd_cr = jnp.einsum('gcp,gh->gphc', c_re, eye).reshape(g * p, g * c)
    bd_ci = jnp.einsum('gcp,gh->gphc', c_im, eye).reshape(g * p, g * c)
    return bd_br, bd_bi, bd_cr, bd_ci, ar.reshape(1, g * p), ai.reshape(1, g * p)


def _s5_prompt_kernel(u_ref, toe_ref, wsr_ref, wsi_ref, wyr_ref, wyi_ref, ar_ref, ai_ref,
                      y_ref, hr_ref, hi_ref, uperm, yperm, sr, si, hr_hist, hi_hist, hcar_r, hcar_i):
    j = pl.program_id(1)
    nc = uperm.shape[0]
    lc, gc = SSM_CHUNK, SSM_GROUP
    gpt = LANES // gc
    n_lt = u_ref.shape[1]
    rb = ROW_TILE // lc
    gw = lc * gc
    lane_grp = lax.broadcasted_iota(jnp.int32, (1, LANES), 1) // gc

    def group_transpose(vs):
        d = 1
        while d < gpt:
            hi = (lane_grp & d) != 0
            nxt = list(vs)
            for a in range(gpt):
                if a & d == 0:
                    nxt[a] = jnp.where(hi, pltpu.roll(vs[a + d], d * gc, 1), vs[a])
                    nxt[a + d] = jnp.where(hi, vs[a + d], pltpu.roll(vs[a], LANES - d * gc, 1))
            vs = nxt
            d *= 2
        return vs

    def permute_in(rc, carry):
        r0 = pl.multiple_of(rc * rb, rb)
        for lt in range(n_lt):
            for h in range(lc // gpt):
                xs = [u_ref[0, lt, pl.ds(pl.multiple_of(r0 * lc + (h * gpt + il) * rb, rb), rb), :].astype(F32)
                      for il in range(gpt)]
                for gl, d in enumerate(group_transpose(xs)):
                    c0 = (lt * gpt + gl) * gw + h * LANES
                    uperm[pl.ds(r0, rb), c0:c0 + LANES] = d.astype(BF16)
        return carry

    lax.fori_loop(0, nc // rb, permute_in, 0, unroll=True)

    n_pairs = wsr_ref.shape[0]
    pw = wsr_ref.shape[1]
    sw = wsr_ref.shape[2]
    for pr in range(n_pairs):
        up = uperm[:, pr * pw:(pr + 1) * pw]
        sr[:, pr * sw:(pr + 1) * sw] = jnp.dot(up, wsr_ref[pr], preferred_element_type=F32)
        si[:, pr * sw:(pr + 1) * sw] = jnp.dot(up, wsi_ref[pr], preferred_element_type=F32)

    ar = ar_ref[...]
    ai = ai_ref[...]
    nstate = ar.shape[1]
    rowid = lax.broadcasted_iota(jnp.int32, (8, nstate), 0)

    @pl.when(j == 0)
    def _():
        hcar_r[...] = jnp.zeros(hcar_r.shape, F32)
        hcar_i[...] = jnp.zeros(hcar_i.shape, F32)

    def eight_chunks(c8, carry):
        h_r, h_i = carry
        base = pl.multiple_of(c8 * 8, 8)
        s_r8 = sr[pl.ds(base, 8), :]
        s_i8 = si[pl.ds(base, 8), :]
        t_r = jnp.zeros((8, nstate), F32)
        t_i = jnp.zeros((8, nstate), F32)
        for r in range(8):
            t_r = jnp.where(rowid == r, h_r, t_r)
            t_i = jnp.where(rowid == r, h_i, t_i)
            n_r = ar * h_r - ai * h_i + s_r8[r:r + 1, :]
            n_i = ar * h_i + ai * h_r + s_i8[r:r + 1, :]
            h_r, h_i = n_r, n_i
        hr_hist[pl.ds(base, 8), :] = t_r
        hi_hist[pl.ds(base, 8), :] = t_i
        return h_r, h_i

    h_r, h_i = lax.fori_loop(0, nc // 8, eight_chunks, (hcar_r[...], hcar_i[...]))
    hcar_r[...] = h_r
    hcar_i[...] = h_i
    hr_ref[0] = h_r
    hi_ref[0] = h_i

    for pr in range(n_pairs):
        st = slice(pr * sw, (pr + 1) * sw)
        y2 = (jnp.dot(hr_hist[:, st].astype(BF16), wyr_ref[pr], preferred_element_type=F32)
              + jnp.dot(hi_hist[:, st].astype(BF16), wyi_ref[pr], preferred_element_type=F32))
        for gg in range(2):
            g = pr * 2 + gg
            cols = slice(g * gw, (g + 1) * gw)
            yi = jnp.dot(uperm[:, cols], toe_ref[g], preferred_element_type=F32)
            yperm[:, cols] = yi + y2[:, gg * gw:(gg + 1) * gw]

    def permute_out(rc, carry):
        r0 = pl.multiple_of(rc * rb, rb)
        for lt in range(n_lt):
            for h in range(lc // gpt):
                ds_ = [yperm[pl.ds(r0, rb), (lt * gpt + gl) * gw + h * LANES:(lt * gpt + gl) * gw + (h + 1) * LANES]
                       for gl in range(gpt)]
                for il, yv in enumerate(group_transpose(ds_)):
                    y_ref[0, lt, pl.ds(pl.multiple_of(r0 * lc + (h * gpt + il) * rb, rb), rb), :] = yv.astype(y_ref.dtype)
        return carry

    lax.fori_loop(0, nc // rb, permute_out, 0, unroll=True)


def _s5_prompt(u, mats):
    b, n_lt, t, _ = u.shape
    wid = n_lt * LANES
    nstate = mats[-1].shape[1]
    ts = min(t, S5_SLAB)
    nc = ts // SSM_CHUNK
    assert t % ts == 0 and ts % ROW_TILE == 0
    full = lambda a: pl.BlockSpec(a.shape, lambda i, j: (0,) * a.ndim)
    return pl.pallas_call(
        _s5_prompt_kernel,
        grid=(b, t // ts),
        in_specs=[pl.BlockSpec((1, n_lt, ts, LANES), lambda i, j: (i, 0, j, 0))] + [full(a) for a in mats],
        out_specs=[pl.BlockSpec((1, n_lt, ts, LANES), lambda i, j: (i, 0, j, 0)),
                   pl.BlockSpec((1, 1, nstate), lambda i, j: (i, 0, 0)),
                   pl.BlockSpec((1, 1, nstate), lambda i, j: (i, 0, 0))],
        out_shape=[jax.ShapeDtypeStruct(u.shape, BF16),
                   jax.ShapeDtypeStruct((b, 1, nstate), F32),
                   jax.ShapeDtypeStruct((b, 1, nstate), F32)],
        scratch_shapes=[pltpu.VMEM((nc, wid * SSM_CHUNK), BF16), pltpu.VMEM((nc, wid * SSM_CHUNK), F32)]
        + [pltpu.VMEM((nc, nstate), F32) for _ in range(4)]
        + [pltpu.VMEM((1, nstate), F32) for _ in range(2)],
        compiler_params=_cparams("arbitrary", "arbitrary"),
    )(u, *mats)


def _s5_step_kernel(u_ref, h0r_ref, h0i_ref, bbr_ref, bbi_ref, cr_ref, ci_ref, ar_ref, ai_ref,
                    y_ref, xr_ref, xi_ref):
    u = u_ref[...]
    ar = ar_ref[...]
    ai = ai_ref[...]
    h0r = h0r_ref[...]
    h0i = h0i_ref[...]
    xr = ar * h0r - ai * h0i + jnp.dot(u, bbr_ref[...], precision=HIGHEST, preferred_element_type=F32)
    xi = ar * h0i + ai * h0r + jnp.dot(u, bbi_ref[...], precision=HIGHEST, preferred_element_type=F32)
    xr_ref[...] = xr
    xi_ref[...] = xi
    y_ref[...] = (jnp.dot(xr, cr_ref[...], precision=HIGHEST, preferred_element_type=F32)
                  - jnp.dot(xi, ci_ref[...], precision=HIGHEST, preferred_element_type=F32))


def _s5_step(u, h0r, h0i, mats):
    n, wid = u.shape
    nstate = h0r.shape[1]
    return pl.pallas_call(
        _s5_step_kernel,
        out_shape=[jax.ShapeDtypeStruct((n, wid), F32),
                   jax.ShapeDtypeStruct((n, nstate), F32),
                   jax.ShapeDtypeStruct((n, nstate), F32)],
        compiler_params=pltpu.CompilerParams(vmem_limit_bytes=VMEM_LIMIT),
    )(u, h0r, h0i, *mats)


def _split_bf16(a):
    hi = a.astype(BF16)
    return hi, (a - hi.astype(F32)).astype(BF16)


def _mix_kernel(x_ref, mod_ref, attn_ref, ys_ref, u_ref, d_ref, wglu_ref, bglu_ref, ga_ref, gs_ref,
                wout_ref, g2_ref, wr_ref, br_ref, ungroup_ref, below_ref,
                x1_ref, h2_ref, gate_ref, idx_ref, rank_ref, count_ref, seen, *, lane_tiled):
    def rows(ref):
        if lane_tiled:
            tile = lambda i: ref[0, i].reshape(-1, LANES)
            return jnp.concatenate([tile(i) for i in range(ref.shape[1])], axis=1).astype(F32)
        return ref[...]

    a_n = (_rms(rows(attn_ref)) * ga_ref[...]).astype(BF16)

    y = rows(ys_ref) + d_ref[...] * rows(u_ref)
    y = 0.5 * y * (1.0 + jnp.tanh(np.float32(np.sqrt(2.0 / np.pi)) * (y + 0.044715 * (y * y * y))))
    z = jnp.dot(y.astype(BF16), wglu_ref[...], preferred_element_type=F32) + bglu_ref[...]
    ssm = y * jax.nn.sigmoid(z)
    s_n = (_rms(ssm) * gs_ref[...]).astype(BF16)
    both = jnp.concatenate([a_n, s_n], axis=1)
    if lane_tiled:
        both = jnp.dot(ungroup_ref[...], both, preferred_element_type=F32).astype(BF16)
    mixed = jnp.dot(both, wout_ref[...], preferred_element_type=F32)
    x1 = x_ref[...] + mod_ref[0, 2] * mixed
    x1_ref[...] = x1
    h2 = _rms(x1) * g2_ref[...] * (1.0 + mod_ref[0, 4]) + mod_ref[0, 3]
    h2_ref[...] = h2.astype(BF16)

    h_hi, h_lo = _split_bf16(h2)
    hh = jnp.dot(h_hi, wr_ref[...], preferred_element_type=F32)
    logits = (hh[:, :LANES] + hh[:, LANES:]
              + jnp.dot(h_lo, wr_ref[:, 0:LANES], preferred_element_type=F32)) + br_ref[...]
    lane = lax.broadcasted_iota(jnp.int32, logits.shape, 1)
    lane_f = lane.astype(F32)
    cur = logits
    vals, idxs = [], []
    for _ in range(TOP_K):
        mx = jnp.max(cur, axis=1, keepdims=True)
        ix = jnp.min(jnp.where(cur == mx, lane_f, float(LANES)), axis=1, keepdims=True)
        vals.append(mx)
        idxs.append(ix)
        cur = jnp.where(lane_f == ix, -jnp.inf, cur)
    exps = [jnp.exp(v - vals[0]) for v in vals]
    den = exps[0]
    for e in exps[1:]:
        den = den + e
    gate = jnp.zeros(logits.shape, F32)
    idx = jnp.zeros(logits.shape, F32)
    for k in range(TOP_K):
        gate = jnp.where(lane == k, exps[k] / den, gate)
        idx = jnp.where(lane == k, idxs[k], idx)
    gate_ref[...] = gate
    idx_ref[...] = idx.astype(jnp.int32)

    @pl.when(pl.program_id(0) == 0)
    def _():
        seen[...] = jnp.zeros(seen.shape, F32)

    chosen = jnp.zeros(logits.shape, F32)
    for k in range(TOP_K):
        chosen = jnp.where(lane_f == idxs[k], 1.0, chosen)
    earlier = jnp.dot(below_ref[...], chosen.astype(BF16), preferred_element_type=F32) + seen[...]
    rank = jnp.zeros(logits.shape, F32)
    for k in range(TOP_K):
        r_k = jnp.sum(jnp.where(lane_f == idxs[k], earlier, 0.0), axis=1, keepdims=True)
        rank = jnp.where(lane == k, r_k, rank)
    rank_ref[...] = rank.astype(jnp.int32)
    seen[...] += jnp.sum(chosen, axis=0, keepdims=True)
    count_ref[...] = seen[...].astype(jnp.int32)


def _mix(x2d, mod, attn, ys, u, weights, tm, rows_per_mod):
    n, d = x2d.shape
    r = mod.shape[2]
    tiles_per_mod = max(rows_per_mod // tm, 1)
    mod_map = (lambda i: (i // tiles_per_mod, 0, 0, 0)) if r == 1 else (lambda i: (0, 0, i, 0))
    rowspec = lambda w: pl.BlockSpec((tm, w), lambda i: (i, 0))
    full = lambda a: pl.BlockSpec(a.shape, lambda i: (0,) * a.ndim)
    lane_tiled = attn.ndim == 5
    below = (jnp.arange(tm)[:, None] > jnp.arange(tm)[None, :]).astype(BF16)
    weights = tuple(weights) + (_regroup_matrix(tm).T, below)
    if lane_tiled:
        def act_spec(a):
            if a.ndim == 5:
                return pl.BlockSpec((1, a.shape[1], a.shape[2], tm // a.shape[2], LANES),
                                    lambda i: (i // tiles_per_mod, 0, 0, i % tiles_per_mod, 0))
            return pl.BlockSpec((1, a.shape[1], tm, LANES), lambda i: (i // tiles_per_mod, 0, i % tiles_per_mod, 0))
    else:
        act_spec = lambda a: rowspec(a.shape[1])
    return pl.pallas_call(
        functools.partial(_mix_kernel, lane_tiled=lane_tiled),
        grid=(n // tm,),
        in_specs=[rowspec(d), pl.BlockSpec((1, 6, r, d), mod_map), act_spec(attn), act_spec(ys), act_spec(u)]
        + [full(w) for w in weights],
        out_specs=[rowspec(d), rowspec(d), rowspec(LANES), rowspec(LANES), rowspec(LANES),
                   pl.BlockSpec((1, LANES), lambda i: (0, 0))],
        out_shape=[jax.ShapeDtypeStruct((n, d), F32), jax.ShapeDtypeStruct((n, d), BF16),
                   jax.ShapeDtypeStruct((n, LANES), F32), jax.ShapeDtypeStruct((n, LANES), jnp.int32),
                   jax.ShapeDtypeStruct((n, LANES), jnp.int32), jax.ShapeDtypeStruct((1, LANES), jnp.int32)],
        scratch_shapes=[pltpu.VMEM((1, LANES), F32)],
        compiler_params=_cparams("arbitrary"),
    )(x2d, mod, attn, ys, u, *weights)


def _moe_kernel(be_ref, first_ref, nb_ref, x_ref, wg_ref, bg_ref, wu_ref, bu_ref, wd_ref, bd_ref, o_ref,
                wg_bf, wu_bf, wd_bf):
    i = pl.program_id(0)

    @pl.when(first_ref[i] == 1)
    def _():
        wg_bf[...] = wg_ref[0].astype(BF16)
        wu_bf[...] = wu_ref[0].astype(BF16)
        wd_bf[...] = wd_ref[0].astype(BF16)

    @pl.when(i < nb_ref[0])
    def _():
        x = x_ref[...]
        g = jnp.dot(x, wg_bf[...], preferred_element_type=F32) + bg_ref[0]
        up = jnp.dot(x, wu_bf[...], preferred_element_type=F32) + bu_ref[0]
        g = jnp.minimum(g, SWIGLU_LIMIT)
        up = jnp.clip(up, -SWIGLU_LIMIT, SWIGLU_LIMIT)
        hid = (up + 1.0) * g * jax.nn.sigmoid(SWIGLU_ALPHA * g)
        y = jnp.dot(hid.astype(BF16), wd_bf[...], preferred_element_type=F32) + bd_ref[0]
        o_ref[...] = y.astype(o_ref.dtype)

    @pl.when(i >= nb_ref[0])
    def _():
        o_ref[...] = jnp.zeros(o_ref.shape, o_ref.dtype)


def _moe_experts(xb, block_e, first, n_used, wg, bg, wu, bu, wd, bd):
    n_rows, d = xb.shape
    e, _, f = wg.shape
    nb = n_rows // MOE_BLOCK
    wmap = lambda i, be, fi, nu: (be[i], 0, 0)
    grid_spec = pltpu.PrefetchScalarGridSpec(
        num_scalar_prefetch=3,
        grid=(nb,),
        in_specs=[pl.BlockSpec((MOE_BLOCK, d), lambda i, be, fi, nu: (i, 0)),
                  pl.BlockSpec((1, d, f), wmap), pl.BlockSpec((1, 1, f), wmap),
                  pl.BlockSpec((1, d, f), wmap), pl.BlockSpec((1, 1, f), wmap),
                  pl.BlockSpec((1, f, d), wmap), pl.BlockSpec((1, 1, d), wmap)],
        out_specs=pl.BlockSpec((MOE_BLOCK, d), lambda i, be, fi, nu: (i, 0)),
        scratch_shapes=[pltpu.VMEM((d, f), BF16), pltpu.VMEM((d, f), BF16), pltpu.VMEM((f, d), BF16)],
    )
    return pl.pallas_call(
        _moe_kernel,
        grid_spec=grid_spec,
        out_shape=jax.ShapeDtypeStruct((n_rows, d), BF16),
        compiler_params=_cparams("arbitrary"),
    )(block_e, first, n_used, xb, wg, bg.reshape(e, 1, f), wu, bu.reshape(e, 1, f), wd, bd.reshape(e, 1, d))


def _final_kernel(x1_ref, mod_ref, gate_ref, *rest):
    yg_refs, (gf_ref, o_ref) = rest[:-2], rest[-2:]
    gate = gate_ref[...]
    acc = jnp.zeros(x1_ref.shape, F32)
    k = 0
    for yg_ref in yg_refs:
        for j in range(yg_ref.shape[0]):
            acc = acc + gate[:, k:k + 1] * yg_ref[j].astype(F32)
            k += 1
    x = x1_ref[...] + mod_ref[0, 5] * acc
    o_ref[...] = _rms(x) * gf_ref[...]


def _final(x1, mod, gates, yg_parts, g_final, tm, rows_per_mod):
    n, d = x1.shape
    tiles_per_mod = rows_per_mod // tm
    assert sum(p.shape[0] for p in yg_parts) == TOP_K
    return pl.pallas_call(
        _final_kernel,
        grid=(n // tm,),
        in_specs=[pl.BlockSpec((tm, d), lambda i: (i, 0)),
                  pl.BlockSpec((1, 6, 1, d), lambda i: (i // tiles_per_mod, 0, 0, 0)),
                  pl.BlockSpec((tm, LANES), lambda i: (i, 0))]
        + [pl.BlockSpec((p.shape[0], tm, d), lambda i: (0, i, 0)) for p in yg_parts]
        + [pl.BlockSpec((1, d), lambda i: (0, 0))],
        out_specs=pl.BlockSpec((tm, d), lambda i: (i, 0)),
        out_shape=jax.ShapeDtypeStruct((n, d), F32),
        compiler_params=_cparams("arbitrary"),
    )(x1, mod, gates, *yg_parts, g_final.reshape(1, d))


def _moe_decode_kernel(h2_ref, gate_ref, idx_ref, x1_ref, mod_ref, gf_ref,
                       wg_ref, bg_ref, wu_ref, bu_ref, wd_ref, bd_ref, o_ref, acc):
    e = pl.program_id(0)

    @pl.when(e == 0)
    def _():
        acc[...] = jnp.zeros(acc.shape, F32)

    gate_e = jnp.sum(jnp.where(idx_ref[...] == e, gate_ref[...], 0.0), axis=1, keepdims=True)
    x = h2_ref[...]
    g = jnp.dot(x, wg_ref[0].astype(BF16), preferred_element_type=F32) + bg_ref[0]
    up = jnp.dot(x, wu_ref[0].astype(BF16), preferred_element_type=F32) + bu_ref[0]
    g = jnp.minimum(g, SWIGLU_LIMIT)
    up = jnp.clip(up, -SWIGLU_LIMIT, SWIGLU_LIMIT)
    hid = (up + 1.0) * g * jax.nn.sigmoid(SWIGLU_ALPHA * g)
    y = jnp.dot(hid.astype(BF16), wd_ref[0].astype(BF16), preferred_element_type=F32) + bd_ref[0]
    acc[...] += gate_e * y

    @pl.when(e == pl.num_programs(0) - 1)
    def _():
        x1 = x1_ref[...] + mod_ref[0, 5] * acc[...]
        o_ref[...] = _rms(x1) * gf_ref[...]


def _moe_decode(h2, gates, idx, x1, mod, g_final, wg, bg, wu, bu, wd, bd):
    n, d = x1.shape
    e, _, f = wg.shape
    full = lambda a: pl.BlockSpec(a.shape, lambda i: (0,) * a.ndim)
    wmap = lambda i: (i, 0, 0)
    return pl.pallas_call(
        _moe_decode_kernel,
        grid=(e,),
        in_specs=[full(h2), full(gates), full(idx), full(x1), full(mod), pl.BlockSpec((1, d), lambda i: (0, 0)),
                  pl.BlockSpec((1, d, f), wmap), pl.BlockSpec((1, 1, f), wmap),
                  pl.BlockSpec((1, d, f), wmap), pl.BlockSpec((1, 1, f), wmap),
                  pl.BlockSpec((1, f, d), wmap), pl.BlockSpec((1, 1, d), wmap)],
        out_specs=pl.BlockSpec((n, d), lambda i: (0, 0)),
        out_shape=jax.ShapeDtypeStruct((n, d), F32),
        scratch_shapes=[pltpu.VMEM((n, d), F32)],
        compiler_params=_cparams("arbitrary"),
    )(h2, gates, idx, x1, mod, g_final.reshape(1, d), wg, bg.reshape(e, 1, f), wu, bu.reshape(e, 1, f),
      wd, bd.reshape(e, 1, d))


def _dispatch(top_e, rank, counts):
    n_exp = counts.shape[0]
    n_assign = top_e.shape[0] * TOP_K
    padded = (counts + MOE_BLOCK - 1) // MOE_BLOCK * MOE_BLOCK
    pend = jnp.cumsum(padded)
    pstart = pend - padded
    dest = pstart[top_e] + rank
    nb = -(-n_assign // MOE_BLOCK) + n_exp
    block_start = jnp.arange(nb, dtype=jnp.int32) * MOE_BLOCK
    block_e = jnp.minimum(jnp.sum((pend[None, :] <= block_start[:, None]).astype(jnp.int32), axis=1), n_exp - 1)
    bits = max(n_assign - 1, 1).bit_length()
    assert (n_exp << bits) < 2 ** 31
    packed = (top_e.reshape(-1) << bits) | jnp.arange(n_assign, dtype=jnp.int32)
    order = jnp.sort(packed) & ((1 << bits) - 1)
    ustart = jnp.cumsum(counts) - counts
    first_src = ustart[block_e] + block_start - pstart[block_e]
    last_src = ustart[block_e] + counts[block_e] - 1
    src = jnp.minimum(first_src[:, None] + jnp.arange(MOE_BLOCK, dtype=jnp.int32)[None, :], last_src[:, None])
    row_tok = order[jnp.clip(src.reshape(-1), 0, n_assign - 1)] // TOP_K
    first = jnp.concatenate([jnp.ones((1,), jnp.int32), (block_e[1:] != block_e[:-1]).astype(jnp.int32)])
    n_used = (pend[-1] // MOE_BLOCK).astype(jnp.int32).reshape(1)
    return row_tok, dest, block_e, first, n_used


def kernel(x_prompt, x_sample, c_prompt, c_sample, cache_k, cache_v, state_ssm_re, state_ssm_im, w_ada, b_ada, g_norm1, g_norm2, w_in, lambda_re, lambda_im, log_dt, b_ssm_re, b_ssm_im, c_ssm_re, c_ssm_im, d_ssm, w_glu, b_glu, g_attn_out, g_ssm_out, w_out, w_router, b_router, w_gate, b_gate, w_up, b_up, w_down, b_down, g_final):
    assert w_ada.shape[0] == 1, "one layer"
    b, t, d = x_prompt.shape
    bs = x_sample.shape[0]
    assert x_sample.shape[1] == 1
    wbuf, n_heads = cache_k.shape[2], cache_k.shape[3]
    aw = n_heads * HEAD_DIM
    n_groups, n_state = lambda_re.shape[1:]
    sw = n_groups * SSM_GROUP
    n_exp = w_router.shape[2]
    keep = min(max(w for w, _ in DILATED_PATTERNS), t)
    tm = min(ROW_TILE, t)

    mod = _ada(jnp.concatenate([c_prompt, c_sample], axis=0), w_ada[0], b_ada[0])
    mod_p = mod[:b].reshape(b, 6, 1, d)
    mod_s = mod[b:].reshape(bs, 6, d).transpose(1, 0, 2)[None]

    w_in_bf = w_in[0].astype(BF16)
    g1 = g_norm1[0].reshape(1, d)
    ssm_params = (lambda_re[0], lambda_im[0], log_dt[0], b_ssm_re[0], b_ssm_im[0], c_ssm_re[0], c_ssm_im[0])
    wr_pad = jnp.zeros((d, LANES), F32).at[:, :n_exp].set(w_router[0])
    wr_hi = wr_pad.astype(BF16)
    wr_lo = (wr_pad - wr_hi.astype(F32)).astype(BF16)
    br_pad = jnp.full((1, LANES), NEG_BIG, F32).at[0, :n_exp].set(b_router[0])
    mix_w = (d_ssm[0].reshape(1, sw), w_glu[0].astype(BF16), b_glu[0].reshape(1, sw),
             g_attn_out[0].reshape(1, aw), g_ssm_out[0].reshape(1, sw), w_out[0].astype(BF16),
             g_norm2[0].reshape(1, d), jnp.concatenate([wr_hi, wr_lo], axis=1), br_pad)
    experts = (w_gate[0], b_gate[0], w_up[0], b_up[0], w_down[0], b_down[0])

    cos_p, sin_p = _rope_tables(jnp.arange(t), n_heads)
    q_p, kv_p, kt_p, vt_p, u_p = _inproj_prompt(x_prompt, mod_p, g1, w_in_bf, cos_p, sin_p, tm, keep)
    flat_t = lambda a: a.reshape(b, a.shape[1], t, LANES)
    attn_p = _dilated_attention(flat_t(q_p), flat_t(kv_p)).reshape(q_p.shape)
    ys_p, hr_p, hi_p = _s5_prompt(u_p, _s5_chunk_matrices(*ssm_params))
    x1_p, h2_p, gate_p, idx_p, rank_p, count_p = _mix(x_prompt.reshape(b * t, d), mod_p, attn_p, ys_p, u_p,
                                                      mix_w, tm, t)

    row_tok, dest, block_e, first, n_used = _dispatch(idx_p[:, :TOP_K], rank_p[:, :TOP_K], count_p[0, :n_exp])
    xb = h2_p.at[row_tok].get(mode="promise_in_bounds")

    cos_s, sin_s = _rope_tables(jnp.full((1,), PAST_LEN), n_heads)
    qt_s, kt_s, vt_s, u_s = _inproj_decode(x_sample.reshape(bs, d), mod_s, g1, w_in_bf, cos_s, sin_s)
    to_hdp = lambda c: jnp.transpose(c[0], (0, 2, 3, 1))
    from_hdp = lambda c: jnp.transpose(c, (0, 3, 1, 2))[None]
    ck_new, cv_new, attn_t = _decode_attention(qt_s, kt_s, vt_s, to_hdp(cache_k), to_hdp(cache_v))
    ys_s, hr_s, hi_s = _s5_step(u_s, state_ssm_re[0].reshape(bs, n_groups * n_state),
                                state_ssm_im[0].reshape(bs, n_groups * n_state),
                                _s5_step_matrices(*ssm_params))
    x1_s, h2_s, gate_s, idx_s, _, _ = _mix(x_sample.reshape(bs, d), mod_s, attn_t.T, ys_s, u_s, mix_w, bs, 1)

    yb = _moe_experts(xb, block_e, first, n_used, *experts)
    yg = [yb.at[dest.T].get(mode="promise_in_bounds")]
    y_sample = _moe_decode(h2_s, gate_s, idx_s, x1_s, mod_s, g_final, *experts).reshape(bs, 1, d)
    y_prompt = _final(x1_p, mod_p, gate_p, yg, g_final, FINAL_TILE, t).reshape(b, t, d)

    k_win_p = from_hdp(kt_p[-1].reshape(b, n_heads, HEAD_DIM, keep))
    v_win_p = from_hdp(vt_p[-1].reshape(b, n_heads, HEAD_DIM, keep))
    st = lambda a, n: a.reshape(1, n, n_groups, n_state)
    return (y_prompt, y_sample, k_win_p, v_win_p, st(hr_p, b), st(hi_p, b),
            from_hdp(ck_new), from_hdp(cv_new), st(hr_s, bs), st(hi_s, bs))
```
